```python
import math
import jax, jax.numpy as jnp
from jax import lax
import numpy as np

D_MODEL = 1024
BATCH = 16
SEQ = 2048
DEPTH = 1
DEC_BATCH = 128
DEC_SEQ = 4
PAST_LEN = 8192
PAGE_SIZE = 128

GLA_HEADS = 4
GLA_DK = D_MODEL // 2 // GLA_HEADS
GLA_DV = D_MODEL // GLA_HEADS
GLA_GATE_RANK = 16
GLA_TAU = 16.0
GLA_CHUNK = 64
NSA_HEADS = 16
NSA_HEAD_DIM = D_MODEL // NSA_HEADS
NSA_KV_HEADS = 4
NSA_GROUP = NSA_HEADS // NSA_KV_HEADS
NSA_CMP_BLOCK = 32
NSA_CMP_STRIDE = 16
NSA_CMP_RATIO = NSA_CMP_BLOCK // NSA_CMP_STRIDE
NSA_CMP_HIDDEN = 2 * NSA_HEAD_DIM
NSA_SEL_BLOCK = 64
NSA_TOPK = 16
NSA_WINDOW = 512
NSA_Q_BLOCK = 16
REL_BUCKETS = 32
REL_MAX_DIST = 128
D_FF = 4 * D_MODEL
RMS_EPS = 1e-6
NEG_INF = -1e30
SEL_FORCE = 1e4
GLA_QK_W = GLA_HEADS * GLA_DK
GLA_V_W = GLA_HEADS * GLA_DV
NSA_Q_W = NSA_HEADS * NSA_HEAD_DIM
NSA_KV_W = NSA_KV_HEADS * NSA_HEAD_DIM
IN_SIZES = (GLA_QK_W, GLA_QK_W, GLA_V_W, GLA_V_W, GLA_GATE_RANK,
            NSA_Q_W, NSA_KV_W, NSA_KV_W, NSA_KV_W, NSA_KV_W, NSA_KV_W, NSA_KV_W, 3 * NSA_HEADS,
            D_MODEL, D_MODEL)
IN_COLS = sum(IN_SIZES)

kernel_name = 'gla_nsa_parallel_adaln_decoder_step'


def rmsnorm(x, g):
    xf = x.astype(jnp.float32)
    y = xf * lax.rsqrt(jnp.mean(xf * xf, axis=-1, keepdims=True) + RMS_EPS)
    return (y * g.astype(jnp.float32)).astype(x.dtype)


def masked_softmax(s, mask, axis=-1):
    s = jnp.where(mask, s, NEG_INF)
    m = jnp.max(s, axis=axis, keepdims=True)
    e = jnp.where(mask, jnp.exp(s - m), 0.0)
    return e / jnp.maximum(jnp.sum(e, axis=axis, keepdims=True), 1e-30)


def rel_bucket(dist):
    n = jnp.maximum(dist, 0)
    max_exact = REL_BUCKETS // 2
    nf = jnp.maximum(n, 1).astype(jnp.float32)
    large = max_exact + (jnp.log(nf / max_exact) / math.log(REL_MAX_DIST / max_exact)
                         * (REL_BUCKETS - max_exact)).astype(jnp.int32)
    large = jnp.minimum(large, REL_BUCKETS - 1)
    return jnp.where(n < max_exact, n, large)


def gla_chunk(S, xs):
    q, k, v, la = xs
    L = q.shape[1]
    b = jnp.cumsum(la, axis=1)
    causal = jnp.tril(jnp.ones((L, L), dtype=bool))[None, :, :, None, None]
    decay = jnp.exp(jnp.where(causal, b[:, :, None] - b[:, None, :], -jnp.inf))
    qf, kf, vf = q.astype(jnp.float32), k.astype(jnp.float32), v.astype(jnp.float32)
    scores = jnp.einsum('bihd,bjhd,bijhd->bhij', qf, kf, decay)
    o = (jnp.einsum('bhij,bjhv->bihv', scores, vf)
         + jnp.einsum('bihd,bhdv->bihv', qf * jnp.exp(b), S))
    b_last = b[:, -1]
    S = (jnp.exp(b_last)[..., None] * S
         + jnp.einsum('bjhd,bjhv->bhdv', kf * jnp.exp(b_last[:, None] - b), vf))
    return S, o


def gla_scan(q, k, v, la, S0):
    B, T = q.shape[:2]
    L = GLA_CHUNK if T % GLA_CHUNK == 0 else T
    n = T // L
    to_chunks = lambda a: jnp.swapaxes(a.reshape(B, n, L, *a.shape[2:]), 0, 1)
    S, o = lax.scan(gla_chunk, S0, (to_chunks(q), to_chunks(k), to_chunks(v), to_chunks(la)))
    return jnp.swapaxes(o, 0, 1).reshape(B, T, GLA_HEADS, GLA_DV), S


def cmp_segments(kx, w1):
    B, T = kx.shape[:2]
    segs = kx.reshape(B, T // NSA_CMP_STRIDE, NSA_CMP_STRIDE, NSA_KV_HEADS, NSA_HEAD_DIM)
    w = w1.reshape(NSA_CMP_RATIO, NSA_CMP_STRIDE, NSA_HEAD_DIM, NSA_CMP_HIDDEN)
    return jnp.einsum('bsegd,redh->rbsgh', segs, w)


def cmp_finish(parts, pe, w1, w2):
    n = parts.shape[2] - NSA_CMP_RATIO + 1
    acc = jnp.einsum('ld,ldh->h', pe, w1)
    for r in range(NSA_CMP_RATIO):
        acc = acc + parts[r, :, r:r + n]
    return jax.nn.silu(acc) @ w2


def cmp_to_sel(n_cmp, n_sel):
    cs = jnp.arange(n_cmp)[:, None] * NSA_CMP_STRIDE
    ss = jnp.arange(n_sel)[None, :] * NSA_SEL_BLOCK
    ov = jnp.minimum(cs + NSA_CMP_BLOCK, ss + NSA_SEL_BLOCK) - jnp.maximum(cs, ss)
    return jnp.clip(ov, 0, None).astype(jnp.float32) / NSA_CMP_BLOCK


def nsa_core(q, q_pos, gates, kc, vc, c_end, n_sel, fetch_sel, kw, vw, w_pos, rel_bias):
    B, Tq = q.shape[:2]
    G, Hg, dh, SEL = NSA_KV_HEADS, NSA_GROUP, NSA_HEAD_DIM, NSA_SEL_BLOCK
    qg = q.reshape(B, Tq, G, Hg, dh) * NSA_HEAD_DIM ** -0.5
    tab = rel_bias.reshape(REL_BUCKETS, G, Hg)
    dist_c = q_pos[:, None] - c_end[None, :]
    bias_c = jnp.moveaxis(tab[rel_bucket(dist_c)], (2, 3), (0, 1))
    s_c = jnp.einsum('bqghd,bcgd->bghqc', qg, kc).astype(jnp.float32) + bias_c
    p_c = masked_softmax(s_c, dist_c >= 0)
    o_cmp = jnp.einsum('bghqc,bcgd->bqghd', p_c.astype(vc.dtype), vc)
    imp = jnp.einsum('bghqc,cs->bgqs', p_c, cmp_to_sel(kc.shape[1], n_sel))
    blk = jnp.arange(n_sel)[None, :]
    cur = (q_pos // SEL)[:, None]
    forced = (blk == 0) | (blk == cur) | (blk == cur - 1)
    valid = blk * SEL <= q_pos[:, None]
    score = jnp.where(forced, SEL_FORCE, jnp.where(valid, imp, -SEL_FORCE))
    _, idx = lax.top_k(score, min(NSA_TOPK, n_sel))
    k_sel, v_sel = fetch_sel(idx)
    key_pos = idx[..., None] * SEL + jnp.arange(SEL)
    dist_s = q_pos[:, None, None] - key_pos
    gi5 = jnp.arange(G)[None, :, None, None, None]
    bias_s = jnp.swapaxes(tab, 0, 1)[gi5, rel_bucket(dist_s)]
    s_s = jnp.einsum('bqghd,bgqnkd->bgqnkh', qg, k_sel).astype(jnp.float32) + bias_s
    nk = idx.shape[-1] * SEL
    p_s = masked_softmax(s_s.reshape(B, G, Tq, nk, Hg), (dist_s >= 0).reshape(B, G, Tq, nk, 1), axis=3)
    o_slc = jnp.einsum('bgqkh,bgqkd->bqghd', p_s.astype(v_sel.dtype), v_sel.reshape(B, G, Tq, nk, dh))
    dist_w = q_pos[:, None] - w_pos[None, :]
    mask_w = (dist_w >= 0) & (dist_w < NSA_WINDOW) & (w_pos[None, :] >= 0)
    bias_w = jnp.moveaxis(tab[rel_bucket(dist_w)], (2, 3), (0, 1))
    s_w = jnp.einsum('bqghd,bkgd->bghqk', qg, kw).astype(jnp.float32) + bias_w
    p_w = masked_softmax(s_w, mask_w)
    o_win = jnp.einsum('bghqk,bkgd->bqghd', p_w.astype(vw.dtype), vw)
    g = gates.reshape(B, Tq, 3, G, Hg, 1)
    o = g[:, :, 0] * o_cmp + g[:, :, 1] * o_slc + g[:, :, 2] * o_win
    return o.reshape(B, Tq, G * Hg * dh)


def nsa_prompt(p, cmp_w, rel_bias):
    pe_k, w1_k, w2_k, pe_v, w1_v, w2_v = cmp_w
    q, gates = p['nsa_q'], p['nsa_gates']
    B, T = q.shape[:2]
    G, dh, SEL = NSA_KV_HEADS, NSA_HEAD_DIM, NSA_SEL_BLOCK
    kc = cmp_finish(cmp_segments(p['k_cmp'], w1_k), pe_k, w1_k, w2_k)
    vc = cmp_finish(cmp_segments(p['v_cmp'], w1_v), pe_v, w1_v, w2_v)
    c_end = jnp.arange(kc.shape[1]) * NSA_CMP_STRIDE + NSA_CMP_BLOCK - 1
    n_sel = T // SEL
    kb = p['k_slc'].reshape(B, n_sel, SEL, G, dh).transpose(0, 3, 1, 2, 4)
    vb = p['v_slc'].reshape(B, n_sel, SEL, G, dh).transpose(0, 3, 1, 2, 4)
    bi = jnp.arange(B)[:, None, None, None]
    gi = jnp.arange(G)[None, :, None, None]

    def fetch(idx):
        return kb[bi, gi, idx], vb[bi, gi, idx]

    pad = ((0, 0), (NSA_WINDOW, 0), (0, 0), (0, 0))
    kwp = jnp.pad(p['k_win'], pad)
    vwp = jnp.pad(p['v_win'], pad)
    qb_len = min(NSA_Q_BLOCK, T)
    span = NSA_WINDOW + qb_len

    def one_block(i):
        s0 = i * qb_len
        q_pos = s0 + jnp.arange(qb_len)
        w_pos = s0 - NSA_WINDOW + jnp.arange(span)
        return nsa_core(lax.dynamic_slice_in_dim(q, s0, qb_len, 1), q_pos,
                        lax.dynamic_slice_in_dim(gates, s0, qb_len, 1), kc, vc, c_end, n_sel, fetch,
                        lax.dynamic_slice_in_dim(kwp, s0, span, 1), lax.dynamic_slice_in_dim(vwp, s0, span, 1),
                        w_pos, rel_bias)

    o = lax.map(one_block, jnp.arange(T // qb_len))
    return jnp.swapaxes(o, 0, 1).reshape(B, T, -1)


def nsa_sample(p, l, cache_k_cmp, cache_v_cmp, cache_k_slc, cache_v_slc, cache_k_win, cache_v_win,
               page_table, cmp_w, rel_bias):
    pe_k, w1_k, w2_k, pe_v, w1_v, w2_v = cmp_w
    q, gates = p['nsa_q'], p['nsa_gates']
    B, T = q.shape[:2]
    G, dh, SEL = NSA_KV_HEADS, NSA_HEAD_DIM, NSA_SEL_BLOCK
    n_pages = PAST_LEN // PAGE_SIZE
    pad = (-T) % NSA_CMP_STRIDE

    def compress(pool, new, pe, w1, w2):
        past = pool[l, page_table].reshape(B, PAST_LEN, G, dh)
        newp = jnp.pad(new, ((0, 0), (0, pad), (0, 0), (0, 0)))
        parts = jnp.concatenate([cmp_segments(past, w1), cmp_segments(newp, w1)], axis=2)
        return cmp_finish(parts, pe, w1, w2)

    kc = compress(cache_k_cmp, p['k_cmp'], pe_k, w1_k, w2_k)
    vc = compress(cache_v_cmp, p['v_cmp'], pe_v, w1_v, w2_v)
    c_end = jnp.arange(kc.shape[1]) * NSA_CMP_STRIDE + NSA_CMP_BLOCK - 1
    n_past_blk = PAST_LEN // SEL
    n_tail = -(-T // SEL)
    blocks_per_page = PAGE_SIZE // SEL
    bi = jnp.arange(B)[:, None, None, None]
    gi = jnp.arange(G)[None, :, None, None]
    rows = jnp.arange(SEL)

    def fetch(idx):
        page = page_table[bi, jnp.clip(idx // blocks_per_page, 0, n_pages - 1)][..., None]
        off = ((idx % blocks_per_page) * SEL)[..., None] + rows
        tail_i = jnp.clip(idx - n_past_blk, 0, n_tail - 1)
        is_tail = (idx >= n_past_blk)[..., None, None]

        def one(pool, new):
            tb = jnp.pad(new, ((0, 0), (0, n_tail * SEL - T), (0, 0), (0, 0)))
            tb = tb.reshape(B, n_tail, SEL, G, dh).transpose(0, 3, 1, 2, 4)
            return jnp.where(is_tail, tb[bi, gi, tail_i], pool[l, page, off, gi[..., None], :])

        return one(cache_k_slc, p['k_slc']), one(cache_v_slc, p['v_slc'])

    wb = cache_k_win.shape[2]
    kw = jnp.concatenate([cache_k_win[l], p['k_win']], axis=1)
    vw = jnp.concatenate([cache_v_win[l], p['v_win']], axis=1)
    w_pos = PAST_LEN - wb + jnp.arange(wb + T)
    q_pos = PAST_LEN + jnp.arange(T)
    o = nsa_core(q, q_pos, gates, kc, vc, c_end, n_past_blk + n_tail, fetch, kw, vw, w_pos, rel_bias)
    return o, kw[:, T:], vw[:, T:]


def project_heads(h, w_in, gla_a2, gla_a_b):
    B, T = h.shape[:2]
    offs = np.cumsum(IN_SIZES)[:-1].tolist()
    (gq, gk, gv, gr, ga, nq, kc, vc, ks, vs, kw, vw, ng, ma, mb) = jnp.split(h @ w_in, offs, axis=-1)
    kv = lambda a: a.reshape(B, T, NSA_KV_HEADS, NSA_HEAD_DIM)
    la = jax.nn.log_sigmoid((ga @ gla_a2 + gla_a_b).astype(jnp.float32)) / GLA_TAU
    return {
        'gla_q': gq.reshape(B, T, GLA_HEADS, GLA_DK) * GLA_DK ** -0.5,
        'gla_k': gk.reshape(B, T, GLA_HEADS, GLA_DK),
        'gla_v': gv.reshape(B, T, GLA_HEADS, GLA_DV),
        'gla_r': gr,
        'gla_la': la.reshape(B, T, GLA_HEADS, GLA_DK),
        'nsa_q': nq.reshape(B, T, NSA_HEADS, NSA_HEAD_DIM),
        'k_cmp': kv(kc), 'v_cmp': kv(vc), 'k_slc': kv(ks), 'v_slc': kv(vs), 'k_win': kv(kw), 'v_win': kv(vw),
        'nsa_gates': jax.nn.sigmoid(ng).reshape(B, T, 3, NSA_HEADS),
        'merge_a': ma, 'merge_b': mb,
    }


def run_layer(x, c, mix, ada_w, ada_b, norm1_g, norm2_g, w_in, gla_a2, gla_a_b, gla_norm_g, w_o, w_up, w_down):
    B, T = x.shape[:2]
    shift1, scale1, gate1, shift2, scale2, gate2 = jnp.split((jax.nn.silu(c) @ ada_w + ada_b)[:, None, :], 6, axis=-1)
    h = rmsnorm(x, norm1_g) * (1 + scale1) + shift1
    p = project_heads(h, w_in, gla_a2, gla_a_b)
    o_gla, o_nsa, new_state = mix(p)
    o_gla = rmsnorm(o_gla.astype(x.dtype), gla_norm_g).reshape(B, T, -1) * jax.nn.silu(p['gla_r'])
    merged = jax.nn.sigmoid(p['merge_a']) * o_gla + jax.nn.sigmoid(p['merge_b']) * o_nsa
    x = x + gate1 * (merged @ w_o)
    h2 = rmsnorm(x, norm2_g) * (1 + scale2) + shift2
    x = x + gate2 * (jnp.square(jax.nn.relu(h2 @ w_up)) @ w_down)
    return x, new_state


def prompt_mix(p, cmp_w, rel_bias):
    B, T = p['nsa_q'].shape[:2]
    S0 = jnp.zeros((B, GLA_HEADS, GLA_DK, GLA_DV), jnp.float32)
    o_gla, S = gla_scan(p['gla_q'], p['gla_k'], p['gla_v'], p['gla_la'], S0)
    o_nsa = nsa_prompt(p, cmp_w, rel_bias)
    wp = min(NSA_WINDOW, T)
    return o_gla, o_nsa, (S.astype(p['nsa_q'].dtype), p['k_cmp'], p['v_cmp'], p['k_slc'], p['v_slc'],
                          p['k_win'][:, T - wp:], p['v_win'][:, T - wp:])


def sample_mix(p, l, state_gla, cache_k_cmp, cache_v_cmp, cache_k_slc, cache_v_slc, cache_k_win, cache_v_win,
               page_table, cmp_w, rel_bias):
    o_gla, S = gla_scan(p['gla_q'], p['gla_k'], p['gla_v'], p['gla_la'], state_gla[l].astype(jnp.float32))
    o_nsa, kw_new, vw_new = nsa_sample(p, l, cache_k_cmp, cache_v_cmp, cache_k_slc, cache_v_slc,
                                       cache_k_win, cache_v_win, page_table, cmp_w, rel_bias)
    return o_gla, o_nsa, (S.astype(p['nsa_q'].dtype), p['k_cmp'], p['v_cmp'], p['k_slc'], p['v_slc'],
                          kw_new, vw_new)


def setup_inputs(seed: int = 0) -> dict:
    key = jax.random.key(seed)
    keys = iter(jax.random.split(key, 40))

    def nrm(shape, scale):
        return jax.random.normal(next(keys), shape, jnp.float32) * scale

    G, dh = NSA_KV_HEADS, NSA_HEAD_DIM
    n_pages = PAST_LEN // PAGE_SIZE
    n_phys = (DEC_BATCH * n_pages * 5) // 4
    wb = min(NSA_WINDOW, PAST_LEN)
    page_table = jax.random.permutation(next(keys), n_phys)[: DEC_BATCH * n_pages].reshape(
        DEC_BATCH, n_pages).astype(jnp.int32)
    return {
        'x_prompt': nrm((BATCH, SEQ, D_MODEL), 1.0),
        'x_sample': nrm((DEC_BATCH, DEC_SEQ, D_MODEL), 1.0),
        'c_prompt': nrm((BATCH, D_MODEL), 1.0),
        'c_sample': nrm((DEC_BATCH, D_MODEL), 1.0),
        'state_gla': nrm((DEPTH, DEC_BATCH, GLA_HEADS, GLA_DK, GLA_DV), 0.5),
        'cache_k_cmp': nrm((DEPTH, n_phys, PAGE_SIZE, G, dh), 1.0),
        'cache_v_cmp': nrm((DEPTH, n_phys, PAGE_SIZE, G, dh), 1.0),
        'cache_k_slc': nrm((DEPTH, n_phys, PAGE_SIZE, G, dh), 1.0),
        'cache_v_slc': nrm((DEPTH, n_phys, PAGE_SIZE, G, dh), 1.0),
        'cache_k_win': nrm((DEPTH, DEC_BATCH, wb, G, dh), 1.0),
        'cache_v_win': nrm((DEPTH, DEC_BATCH, wb, G, dh), 1.0),
        'page_table': page_table,
        'ada_w': nrm((DEPTH, D_MODEL, 6 * D_MODEL), 0.5 * D_MODEL ** -0.5),
        'ada_b': nrm((DEPTH, 6 * D_MODEL), 0.1),
        'norm1_g': 1.0 + nrm((DEPTH, D_MODEL), 0.1),
        'norm2_g': 1.0 + nrm((DEPTH, D_MODEL), 0.1),
        'w_in': nrm((DEPTH, D_MODEL, IN_COLS), D_MODEL ** -0.5),
        'gla_a2': nrm((DEPTH, GLA_GATE_RANK, GLA_QK_W), GLA_GATE_RANK ** -0.5),
        'gla_a_b': nrm((DEPTH, GLA_QK_W), 0.1),
        'gla_norm_g': 1.0 + nrm((DEPTH, GLA_DV), 0.1),
        'cmp_pe_k': nrm((DEPTH, NSA_CMP_BLOCK, dh), 0.1),
        'cmp_w1_k': nrm((DEPTH, NSA_CMP_BLOCK, dh, NSA_CMP_HIDDEN), (NSA_CMP_BLOCK * dh) ** -0.5),
        'cmp_w2_k': nrm((DEPTH, NSA_CMP_HIDDEN, dh), NSA_CMP_HIDDEN ** -0.5),
        'cmp_pe_v': nrm((DEPTH, NSA_CMP_BLOCK, dh), 0.1),
        'cmp_w1_v': nrm((DEPTH, NSA_CMP_BLOCK, dh, NSA_CMP_HIDDEN), (NSA_CMP_BLOCK * dh) ** -0.5),
        'cmp_w2_v': nrm((DEPTH, NSA_CMP_HIDDEN, dh), NSA_CMP_HIDDEN ** -0.5),
        'w_o': nrm((DEPTH, D_MODEL, D_MODEL), D_MODEL ** -0.5),
        'w_up': nrm((DEPTH, D_MODEL, D_FF), D_MODEL ** -0.5),
        'w_down': nrm((DEPTH, D_FF, D_MODEL), D_FF ** -0.5),
        'rel_bias': nrm((REL_BUCKETS, NSA_HEADS), 0.5),
        'final_g': 1.0 + nrm((D_MODEL,), 0.1),
    }


def reference(x_prompt, x_sample, c_prompt, c_sample, state_gla, cache_k_cmp, cache_v_cmp, cache_k_slc,
              cache_v_slc, cache_k_win, cache_v_win, page_table, ada_w, ada_b, norm1_g, norm2_g, w_in,
              gla_a2, gla_a_b, gla_norm_g, cmp_pe_k, cmp_w1_k, cmp_w2_k, cmp_pe_v, cmp_w1_v, cmp_w2_v,
              w_o, w_up, w_down, rel_bias, final_g):
    xp, xs = x_prompt, x_sample
    p_states, s_states = [], []
    for l in range(DEPTH):
        cmp_w = (cmp_pe_k[l], cmp_w1_k[l], cmp_w2_k[l], cmp_pe_v[l], cmp_w1_v[l], cmp_w2_v[l])
        shared = (ada_w[l], ada_b[l], norm1_g[l], norm2_g[l], w_in[l], gla_a2[l], gla_a_b[l], gla_norm_g[l],
                  w_o[l], w_up[l], w_down[l])
        xp, st = run_layer(xp, c_prompt, lambda p: prompt_mix(p, cmp_w, rel_bias), *shared)
        p_states.append(st)
        xs, st = run_layer(xs, c_sample,
                           lambda p: sample_mix(p, l, state_gla, cache_k_cmp, cache_v_cmp, cache_k_slc,
                                                cache_v_slc, cache_k_win, cache_v_win, page_table,
                                                cmp_w, rel_bias), *shared)
        s_states.append(st)
    y_prompt = rmsnorm(xp, final_g)
    y_sample = rmsnorm(xs, final_g)
    p_gla, p_kc, p_vc, p_ks, p_vs, p_kw, p_vw = [jnp.stack(a) for a in zip(*p_states)]
    s_gla, s_kc, s_vc, s_ks, s_vs, s_kw, s_vw = [jnp.stack(a) for a in zip(*s_states)]
    return (y_prompt, y_sample, p_gla, p_kc, p_vc, p_ks, p_vs, p_kw, p_vw,
            s_gla, s_kc, s_vc, s_ks, s_vs, s_kw, s_vw)
```

```python
import functools
import math

import numpy as np
import jax
import jax.numpy as jnp
from jax import lax
from jax.experimental import pallas as pl
from jax.experimental.pallas import tpu as pltpu

F32 = jnp.float32
BF16 = jnp.bfloat16
HIGHEST = lax.Precision.HIGHEST

D_MODEL = 1024
GLA_HEADS = 4
GLA_DK = 128
GLA_DV = 256
GLA_RANK = 16
GLA_TAU = 16.0
GLA_CHUNK = 64
GLA_SUB = 16
NSA_HEADS = 16
DH = 64
NSA_G = 4
NSA_HG = 4
CMP_BLOCK = 32
CMP_STRIDE = 16
CMP_RATIO = 2
CMP_HID = 128
SEL = 64
SEL_SHIFT = 6
TOPK = 16
WIN = 512
REL_BUCKETS = 32
REL_MAX = 128
D_FF = 4096
EPS = 1e-6
NEG = -1e30
FORCE = 1e4
PAGE = 128

QK_W = GLA_HEADS * GLA_DK
V_W = GLA_HEADS * GLA_DV
KV_W = NSA_G * DH
LANES = 128
TQ = 128
VMEM_LIMIT = 56 * 1024 * 1024

C_GQ, C_GK, C_GV, C_GR, C_NQ = 0, 512, 1024, 2048, 3072
C_KV = 4096
C_MA, C_MB, C_Z = 5632, 6656, 7680
W_COLS = 7808
Z_GA = 48


def _cparams(sem):
    return pltpu.CompilerParams(dimension_semantics=sem, vmem_limit_bytes=VMEM_LIMIT)


def _const_spec(shape):
    return pl.BlockSpec(shape, lambda *a: (0,) * len(shape), pipeline_mode=pl.Buffered(1))


def _sigmoid(x):
    return 1.0 / (1.0 + jnp.exp(-x))


def _rms(x, g):
    return x * lax.rsqrt(jnp.mean(x * x, axis=-1, keepdims=True) + EPS) * g


def _dot(a, b, **kw):
    return jnp.dot(a, b, preferred_element_type=F32, **kw)


def _dot_nt(a, b):
    return lax.dot_general(a, b, (((1,), (1,)), ((), ())), preferred_element_type=F32)


def _dot_tn(a, b):
    return lax.dot_general(a, b, (((0,), (0,)), ((), ())), preferred_element_type=F32)


def _bucket_np(dist):
    n = np.maximum(dist, 0)
    nf = np.maximum(n, 1).astype(np.float64)
    large = 16 + (np.log(nf / 16.0) / math.log(REL_MAX / 16.0) * 16.0).astype(np.int64)
    large = np.minimum(large, REL_BUCKETS - 1)
    return np.where(n < 16, n, large).astype(np.int32)


def _ada_kernel(c_ref, w_ref, b_ref, o_ref):
    c = c_ref[...]
    o_ref[...] = _dot(c * _sigmoid(c), w_ref[...], precision=HIGHEST) + b_ref[...]


def _ada(c, w, b):
    r, d = c.shape
    n = w.shape[1]
    return pl.pallas_call(
        _ada_kernel,
        out_shape=jax.ShapeDtypeStruct((r, n), F32),
        grid=(n // d,),
        in_specs=[pl.BlockSpec((r, d), lambda j: (0, 0)),
                  pl.BlockSpec((d, d), lambda j: (0, j)),
                  pl.BlockSpec((1, d), lambda j: (0, j))],
        out_specs=pl.BlockSpec((r, d), lambda j: (0, j)),
        compiler_params=_cparams(("arbitrary",)),
        name="ada",
    )(c, w, b.reshape(1, n))


def _inproj_kernel(x_ref, sc_ref, sh_ref, g_ref, w_ref, a2_ref, ab_ref, *outs, head_major):
    (gq_o, gk_o, gv_o, sgr_o, la_o, kc_o, vc_o, ks_o, vs_o, kw_o, vw_o, gt_o, sma_o, smb_o, *rest) = outs
    h = _rms(x_ref[...], g_ref[...]) * (1.0 + sc_ref[0]) + sh_ref[0]
    hb = h.astype(BF16)

    def proj(lo, hi):
        return _dot(hb, w_ref[:, lo:hi])

    gq_o[...] = proj(C_GQ, C_GK) * (GLA_DK ** -0.5)
    gk_o[...] = proj(C_GK, C_GV)
    gv_o[...] = proj(C_GV, C_GR)
    gr = proj(C_GR, C_NQ)
    sgr_o[...] = gr * _sigmoid(gr)
    nq = proj(C_NQ, C_KV) * (DH ** -0.5)
    kv = proj(C_KV, C_MA)
    for n, o in enumerate((kc_o, vc_o, ks_o, vs_o, kw_o, vw_o)):
        o[...] = kv[:, n * KV_W:(n + 1) * KV_W]
    sma_o[...] = _sigmoid(proj(C_MA, C_MB))
    smb_o[...] = _sigmoid(proj(C_MB, C_Z))
    z = proj(C_Z, W_COLS)
    gt_o[...] = _sigmoid(z)
    pre = _dot(z.astype(BF16), a2_ref[...]) + ab_ref[...]
    la_o[...] = (jnp.minimum(pre, 0.0) - jnp.log(1.0 + jnp.exp(-jnp.abs(pre)))) * (1.0 / GLA_TAU)
    if head_major:
        qh_o, ksh_o, vsh_o, kwh_o, vwh_o = rest
        for hh in range(NSA_HEADS):
            qh_o[hh] = nq[:, hh * DH:(hh + 1) * DH].astype(BF16)
        for n, o in ((2, ksh_o), (3, vsh_o), (4, kwh_o), (5, vwh_o)):
            for gg in range(NSA_G):
                lo = n * KV_W + gg * DH
                o[gg] = kv[:, lo:lo + DH].astype(BF16)
    else:
        (nq_o,) = rest
        nq_o[...] = nq


def _inproj(x, scale, shift, g1, w, a2p, ab, *, head_major, tm):
    m = x.shape[0]
    nb, r, _ = scale.shape
    rows_per_mod = m // nb
    mod_spec = pl.BlockSpec((1, r, D_MODEL), lambda i: (i * tm // rows_per_mod, 0, 0))
    row = lambda wdt: pl.BlockSpec((tm, wdt), lambda i: (i, 0))
    widths = [QK_W, QK_W, V_W, V_W, QK_W] + [KV_W] * 6 + [LANES, D_MODEL, D_MODEL]
    out_shape = [jax.ShapeDtypeStruct((m, wd), F32) for wd in widths]
    out_specs = [row(wd) for wd in widths]
    if head_major:
        out_shape.append(jax.ShapeDtypeStruct((NSA_HEADS, m, DH), BF16))
        out_specs.append(pl.BlockSpec((NSA_HEADS, tm, DH), lambda i: (0, i, 0)))
        for _ in range(4):
            out_shape.append(jax.ShapeDtypeStruct((NSA_G, m, DH), BF16))
            out_specs.append(pl.BlockSpec((NSA_G, tm, DH), lambda i: (0, i, 0)))
    else:
        out_shape.append(jax.ShapeDtypeStruct((m, D_MODEL), F32))
        out_specs.append(row(D_MODEL))
    return pl.pallas_call(
        functools.partial(_inproj_kernel, head_major=head_major),
        out_shape=out_shape,
        grid=(m // tm,),
        in_specs=[row(D_MODEL), mod_spec, mod_spec, _const_spec((1, D_MODEL)), _const_spec((D_MODEL, W_COLS)),
                  _const_spec((LANES, QK_W)), _const_spec((1, QK_W))],
        out_specs=out_specs,
        compiler_params=_cparams(("parallel",)),
        name="inproj",
    )(x, scale, shift, g1, w, a2p, ab)


def _gla_kernel(*refs, chunk, sub, n_chunk, has_init):
    if has_init:
        q_ref, k_ref, v_ref, la_ref, sgr_ref, gn_ref, s0_ref, o_ref, so_ref, s_scr = refs
    else:
        q_ref, k_ref, v_ref, la_ref, sgr_ref, gn_ref, o_ref, so_ref, s_scr = refs
    t = pl.program_id(2)

    @pl.when(t == 0)
    def _():
        s_scr[...] = s0_ref[0, 0] if has_init else jnp.zeros_like(s_scr)

    tril = (lax.broadcasted_iota(jnp.int32, (chunk, chunk), 0)
            >= lax.broadcasted_iota(jnp.int32, (chunk, chunk), 1)).astype(F32)
    sub_row = lax.broadcasted_iota(jnp.int32, (sub, 1), 0)

    def one_chunk(c, carry):
        r0 = pl.multiple_of(c * chunk, chunk)
        q = q_ref[0, pl.ds(r0, chunk), :]
        k = k_ref[0, pl.ds(r0, chunk), :]
        v = v_ref[0, pl.ds(r0, chunk), :]
        b = _dot(tril, la_ref[0, pl.ds(r0, chunk), :], precision=HIGHEST)
        s_old = s_scr[...]
        vb = v.astype(BF16)
        o_inter = _dot((q * jnp.exp(b)).astype(BF16), s_old.astype(BF16))
        blocks = []
        for blk in range(chunk // sub):
            sl = slice(blk * sub, (blk + 1) * sub)
            q_i, k_i, v_i, b_i = q[sl], k[sl], v[sl], b[sl]
            acc = jnp.zeros((sub, GLA_DV), F32)
            for j in range(sub):
                w = q_i * k_i[j:j + 1] * jnp.exp(jnp.minimum(b_i - b_i[j:j + 1], 0.0))
                a = jnp.sum(w, axis=-1, keepdims=True)
                acc = acc + jnp.where(sub_row >= j, a, 0.0) * v_i[j:j + 1]
            if blk > 0:
                n = blk * sub
                ref_row = b_i[0:1]
                qp = (q_i * jnp.exp(b_i - ref_row)).astype(BF16)
                kp = (k[:n] * jnp.exp(ref_row - b[:n])).astype(BF16)
                acc = acc + _dot(_dot_nt(qp, kp).astype(BF16), vb[:n])
            blocks.append(acc)
        o = o_inter + (jnp.concatenate(blocks, axis=0) if len(blocks) > 1 else blocks[0])
        b_last = b[chunk - 1:chunk]
        kp = (k * jnp.exp(b_last - b)).astype(BF16)
        decay_col = jnp.transpose(jnp.broadcast_to(jnp.exp(b_last), (GLA_DK, GLA_DK)))[:, 0:1]
        s_scr[...] = decay_col * s_old + _dot_tn(kp, vb)
        o_ref[0, pl.ds(r0, chunk), :] = _rms(o, gn_ref[...]) * sgr_ref[0, pl.ds(r0, chunk), :]
        return carry

    lax.fori_loop(0, n_chunk, one_chunk, 0)

    @pl.when(t == pl.num_programs(2) - 1)
    def _():
        so_ref[0, 0] = s_scr[...]


def _gla(q, k, v, la, sgr, gn, s0, *, tb):
    bsz, t, _ = q.shape
    chunk = min(GLA_CHUNK, tb)
    sub = min(GLA_SUB, chunk)
    blk = lambda wd: pl.BlockSpec((1, tb, wd), lambda b, h, i: (b, i, h))
    st_spec = pl.BlockSpec((1, 1, GLA_DK, GLA_DV), lambda b, h, i: (b, h, 0, 0))
    in_specs = [blk(GLA_DK), blk(GLA_DK), blk(GLA_DV), blk(GLA_DK), blk(GLA_DV),
                pl.BlockSpec((1, GLA_DV), lambda b, h, i: (0, 0))]
    args = [q, k, v, la, sgr, gn]
    if s0 is not None:
        in_specs.append(st_spec)
        args.append(s0)
    return pl.pallas_call(
        functools.partial(_gla_kernel, chunk=chunk, sub=sub, n_chunk=tb // chunk, has_init=s0 is not None),
        out_shape=[jax.ShapeDtypeStruct((bsz, t, V_W), F32),
                   jax.ShapeDtypeStruct((bsz, GLA_HEADS, GLA_DK, GLA_DV), F32)],
        grid=(bsz, GLA_HEADS, t // tb),
        in_specs=in_specs,
        out_specs=[blk(GLA_DV), st_spec],
        scratch_shapes=[pltpu.VMEM((GLA_DK, GLA_DV), F32)],
        compiler_params=_cparams(("parallel", "parallel", "arbitrary")),
        name="gla",
    )(*args)


def _mm_kernel(*refs, n_prefetch=0):
    *x_refs, w_ref, o_ref = refs[n_prefetch:]
    if len(x_refs) == 1:
        x = x_refs[0][...]
    else:
        x = jnp.concatenate([r[0] for r in x_refs], axis=0)
    out = _dot(x.astype(BF16), w_ref[...])
    o_ref[...] = out.reshape(o_ref.shape)


def _mm(x, w, *, tm):
    m, kdim = x.shape
    n = w.shape[1]
    return pl.pallas_call(
        _mm_kernel,
        out_shape=jax.ShapeDtypeStruct((m, n), F32),
        grid=(m // tm,),
        in_specs=[pl.BlockSpec((tm, kdim), lambda i: (i, 0)), _const_spec((kdim, n))],
        out_specs=pl.BlockSpec((tm, n), lambda i: (i, 0)),
        compiler_params=_cparams(("parallel",)),
        name="cmp_mm",
    )(x, w)


def _mm_paged(pool, page_table, w, *, pages_per_step):
    _, rpp, kdim = pool.shape
    bsz, n_pages = page_table.shape
    n = w.shape[1]
    pps = pages_per_step

    def page_spec(j):
        return pl.BlockSpec((1, rpp, kdim), lambda b, s, pt: (pt[b, s * pps + j], 0, 0))

    grid_spec = pltpu.PrefetchScalarGridSpec(
        num_scalar_prefetch=1,
        grid=(bsz, n_pages // pps),
        in_specs=[page_spec(j) for j in range(pps)]
        + [pl.BlockSpec((kdim, n), lambda b, s, pt: (0, 0), pipeline_mode=pl.Buffered(1))],
        out_specs=pl.BlockSpec((1, pps * rpp, n), lambda b, s, pt: (b, s, 0)),
    )
    return pl.pallas_call(
        functools.partial(_mm_kernel, n_prefetch=1),
        out_shape=jax.ShapeDtypeStruct((bsz, n_pages * rpp, n), F32),
        grid_spec=grid_spec,
        compiler_params=_cparams(("parallel", "arbitrary")),
        name="cmp_mm_paged",
    )(page_table, *([pool] * pps), w)


def _cmpfin_kernel(*refs, n, has_new, flat):
    if has_new:
        p_ref, pn_ref, pe_ref, w1_ref, w2_ref, o_ref = refs
    else:
        p_ref, pe_ref, w1_ref, w2_ref, o_ref = refs
    pe_term = _dot(pe_ref[...], w1_ref[...], precision=HIGHEST)[0:1]
    w2 = w2_ref[...].astype(BF16)
    rowi = lax.broadcasted_iota(jnp.int32, (n, 1), 0)
    for g in range(NSA_G):
        base = g * CMP_RATIO * CMP_HID
        first = p_ref[0, :, base:base + CMP_HID]
        second = pltpu.roll(p_ref[0, :, base + CMP_HID:base + 2 * CMP_HID], n - 1, 0)
        if has_new:
            second = jnp.where(rowi == n - 1, pn_ref[0, :, base + CMP_HID:base + 2 * CMP_HID], second)
        acc = first + second + pe_term
        out = _dot((acc * _sigmoid(acc)).astype(BF16), w2)
        if flat:
            o_ref[0, :, g * DH:(g + 1) * DH] = out
        else:
            o_ref[0, g] = out


def _cmpfin(parts, parts_new, pe, w1, w2, *, flat):
    bsz, n, width = parts.shape
    in_specs = [pl.BlockSpec((1, n, width), lambda b: (b, 0, 0))]
    args = [parts]
    if parts_new is not None:
        in_specs.append(pl.BlockSpec((1, 1, width), lambda b: (b, 0, 0)))
        args.append(parts_new)
    in_specs += [_const_spec(pe.shape), _const_spec(w1.shape), _const_spec(w2.shape)]
    if flat:
        out_shape = jax.ShapeDtypeStruct((bsz, n, KV_W), F32)
        out_spec = pl.BlockSpec((1, n, KV_W), lambda b: (b, 0, 0))
    else:
        out_shape = jax.ShapeDtypeStruct((bsz, NSA_G, n, DH), F32)
        out_spec = pl.BlockSpec((1, NSA_G, n, DH), lambda b: (b, 0, 0, 0))
    return pl.pallas_call(
        functools.partial(_cmpfin_kernel, n=n, has_new=parts_new is not None, flat=flat),
        out_shape=out_shape,
        grid=(bsz,),
        in_specs=in_specs,
        out_specs=out_spec,
        compiler_params=_cparams(("parallel",)),
        name="cmp_finish",
    )(*args, pe, w1, w2)


def _masked_softmax(s, mask, axis):
    s = jnp.where(mask, s, NEG)
    m = jnp.max(s, axis=axis, keepdims=True)
    e = jnp.where(mask, jnp.exp(s - m), 0.0)
    return e / jnp.maximum(jnp.sum(e, axis=axis, keepdims=True), 1e-30)


def _nsa_prompt_kernel(qh_ref, kc_ref, vc_ref, ks_ref, vs_ref, kw_ref, vw_ref, gt_ref, bc_ref, bt_ref, c2s_ref,
                       e_ref, o_ref, selx, m_s, l_s, acc_s, *, n_cmp, n_sel):
    g = pl.program_id(1)
    i = pl.program_id(2)
    rows = NSA_HG * TQ
    q4 = qh_ref[...].reshape(rows, DH)
    lane = lax.broadcasted_iota(jnp.int32, (1, LANES), 1)
    qpos4 = i * TQ + (lax.broadcasted_iota(jnp.int32, (rows, 1), 0) & (TQ - 1))
    qpos = i * TQ + lax.broadcasted_iota(jnp.int32, (TQ, 1), 0)

    s_c = _dot_nt(q4, kc_ref[0, 0].astype(BF16)) + bc_ref[...].reshape(rows, LANES)
    mask_c = (qpos4 - (lane * CMP_STRIDE + CMP_BLOCK - 1) >= 0) & (lane < n_cmp)
    p_c = _masked_softmax(s_c, mask_c, -1)
    o_cmp = _dot(p_c.astype(BF16), vc_ref[0, 0].astype(BF16))

    p_sum = p_c[0:TQ] + p_c[TQ:2 * TQ] + p_c[2 * TQ:3 * TQ] + p_c[3 * TQ:4 * TQ]
    imp = _dot(p_sum, c2s_ref[...], precision=HIGHEST)
    cur = jnp.right_shift(qpos, SEL_SHIFT)
    forced = (lane == 0) | (lane == cur) | (lane == cur - 1)
    valid = lane * SEL <= qpos
    score = jnp.where(forced, FORCE, jnp.where(valid, imp, -FORCE))
    score = jnp.where(lane < n_sel, score, -3.0 * FORCE)
    cnt = jnp.zeros((TQ, LANES), jnp.int32)
    for b in range(n_sel):
        s_b = score[:, b:b + 1]
        ahead = (s_b > score) | ((s_b == score) & (lane > b))
        cnt = cnt + ahead.astype(jnp.int32)
    sel = jnp.where((cnt < min(TOPK, n_sel)) & (lane < n_sel), 1.0, 0.0).astype(BF16)
    selx[...] = _dot(sel, e_ref[...])

    def attend(k_ref, v_ref, lo, hi, use_sel):
        m_s[...] = jnp.full_like(m_s, NEG)
        l_s[...] = jnp.zeros_like(l_s)
        acc_s[...] = jnp.zeros_like(acc_s)

        def body(kt, carry):
            k0 = pl.multiple_of(kt * TQ, TQ)
            s = _dot_nt(q4, k_ref[0, 0, pl.ds(k0, TQ), :])
            s = s + bt_ref[jnp.minimum(i - kt, 2)].reshape(rows, LANES)
            dist = qpos4 - (k0 + lane)
            if use_sel:
                sel_t = selx[:, pl.ds(k0, TQ)]
                mask = (dist >= 0) & (jnp.concatenate([sel_t] * NSA_HG, axis=0) > 0.5)
            else:
                mask = (dist >= 0) & (dist < WIN)
            s = jnp.where(mask, s, NEG)
            m_old = m_s[...]
            m_new = jnp.maximum(m_old, jnp.max(s, axis=-1, keepdims=True))
            p = jnp.where(mask, jnp.exp(s - m_new), 0.0)
            alpha = jnp.exp(m_old - m_new)
            l_s[...] = alpha * l_s[...] + jnp.sum(p, axis=-1, keepdims=True)
            acc_s[...] = alpha * acc_s[...] + _dot(p.astype(BF16), v_ref[0, 0, pl.ds(k0, TQ), :])
            m_s[...] = m_new
            return carry

        lax.fori_loop(lo, hi, body, 0)
        return acc_s[...] / jnp.maximum(l_s[...], 1e-30)

    o_slc = attend(ks_ref, vs_ref, 0, i + 1, True)
    o_win = attend(kw_ref, vw_ref, jnp.maximum(i - WIN // TQ, 0), i + 1, False)

    gates = gt_ref[0]
    for h in range(NSA_HG):
        hs = slice(h * TQ, (h + 1) * TQ)
        o_h = jnp.zeros((TQ, DH), F32)
        for br, o_br in enumerate((o_cmp, o_slc, o_win)):
            col = br * NSA_HEADS + g * NSA_HG + h
            gate = jnp.sum(jnp.where(lane == col, gates, 0.0), axis=-1, keepdims=True)
            o_h = o_h + gate * o_br[hs]
        o_ref[0, :, h * DH:(h + 1) * DH] = o_h


def _nsa_prompt(qh, kc, vc, ksh, vsh, kwh, vwh, gates, bias_c, bias_t, c2s, expand, *, n_cmp):
    _, bsz, t, _ = qh.shape
    n_sel = t // SEL
    kv_spec = pl.BlockSpec((1, 1, t, DH), lambda b, g, i: (g, b, 0, 0))
    cmp_spec = pl.BlockSpec((1, 1, kc.shape[2], DH), lambda b, g, i: (b, g, 0, 0))
    rows = NSA_HG * TQ
    return pl.pallas_call(
        functools.partial(_nsa_prompt_kernel, n_cmp=n_cmp, n_sel=n_sel),
        out_shape=jax.ShapeDtypeStruct((bsz, t, D_MODEL), F32),
        grid=(bsz, NSA_G, t // TQ),
        in_specs=[pl.BlockSpec((NSA_HG, 1, TQ, DH), lambda b, g, i: (g, b, i, 0)),
                  cmp_spec, cmp_spec, kv_spec, kv_spec, kv_spec, kv_spec,
                  pl.BlockSpec((1, TQ, LANES), lambda b, g, i: (b, i, 0)),
                  pl.BlockSpec((NSA_HG, TQ, LANES), lambda b, g, i: (g, i, 0)),
                  pl.BlockSpec((3, NSA_HG, TQ, LANES), lambda b, g, i: (0, g, 0, 0)),
                  pl.BlockSpec(c2s.shape, lambda b, g, i: (0, 0)),
                  pl.BlockSpec(expand.shape, lambda b, g, i: (0, 0))],
        out_specs=pl.BlockSpec((1, TQ, NSA_HG * DH), lambda b, g, i: (b, i, g)),
        scratch_shapes=[pltpu.VMEM((TQ, t), F32), pltpu.VMEM((rows, 1), F32), pltpu.VMEM((rows, 1), F32),
                        pltpu.VMEM((rows, DH), F32)],
        compiler_params=_cparams(("parallel", "parallel", "arbitrary")),
        name="nsa_prompt",
    )(qh, kc, vc, ksh, vsh, kwh, vwh, gates, bias_c, bias_t, c2s, expand)


def _nsa_sample_kernel(*refs, pps, n_cmp, n_sel, n_sel_pad, past_len, t_new):
    pt_ref = refs[0]
    del pt_ref
    k_pages = refs[1:1 + pps]
    v_pages = refs[1 + pps:1 + 2 * pps]
    (q_ref, kc_ref, vc_ref, kn_ref, vn_ref, kwc_ref, vwc_ref, kwn_ref, vwn_ref, gt_ref, bc_ref, bs_ref, bn_ref,
     bw_ref, c2s_ref, hsum_ref, hexp_ref, e_ref, o_ref,
     sel_s, score_s, m_s, l_s, acc_s, ocmp_s, owin_s) = refs[1 + 2 * pps:]
    ch = pl.program_id(1)
    q = q_ref[0]
    col = lax.broadcasted_iota(jnp.int32, (1, LANES), 1)
    tok = col & (t_new - 1)
    qpos = past_len + tok

    def finish(p_t_list, v_list):
        out = None
        for p_t, v in zip(p_t_list, v_list):
            term = _dot(jnp.transpose(p_t).astype(BF16), v.astype(BF16))
            out = term if out is None else out + term
        return out

    def to_col(row_vec):
        return jnp.transpose(jnp.broadcast_to(row_vec, (LANES, LANES)))[:, 0:1]

    @pl.when(ch == 0)
    def _():
        crow = lax.broadcasted_iota(jnp.int32, (n_cmp, 1), 0)
        s_c = _dot(kc_ref[0].astype(BF16), q) + bc_ref[...]
        mask_c = (qpos - (crow * CMP_STRIDE + CMP_BLOCK - 1)) >= 0
        p_c = _masked_softmax(s_c, mask_c, 0)
        ocmp_s[...] = finish([p_c], [vc_ref[0]])
        p_sum = _dot(p_c, hsum_ref[...], precision=HIGHEST)
        imp = _dot(c2s_ref[...], p_sum, precision=HIGHEST)
        blk = lax.broadcasted_iota(jnp.int32, (n_sel_pad, 1), 0)
        cur = jnp.right_shift(qpos, SEL_SHIFT)
        forced = (blk == 0) | (blk == cur) | (blk == cur - 1)
        valid = blk * SEL <= qpos
        score = jnp.where(forced, FORCE, jnp.where(valid, imp, -FORCE))
        score = jnp.where(blk < n_sel, score, -3.0 * FORCE)
        score_s[...] = score

        def rank_body(b, cnt):
            s_b = score_s[pl.ds(b, 1), :]
            ahead = (s_b > score) | ((s_b == score) & (blk > b))
            return cnt + ahead.astype(jnp.int32)

        cnt = lax.fori_loop(0, n_sel, rank_body, jnp.zeros((n_sel_pad, LANES), jnp.int32))
        sel = jnp.where((cnt < min(TOPK, n_sel)) & (blk < n_sel), 1.0, 0.0).astype(BF16)
        sel_s[...] = _dot(sel, hexp_ref[...])

        jrow = lax.broadcasted_iota(jnp.int32, (WIN, 1), 0)
        nrow = lax.broadcasted_iota(jnp.int32, (PAGE, 1), 0)
        s_w = _dot(kwc_ref[0].astype(BF16), q) + bw_ref[...]
        dist_w = qpos - (past_len - WIN + jrow)
        mask_w = (dist_w >= 0) & (dist_w < WIN)
        s_n = _dot(kwn_ref[0].astype(BF16), q) + bn_ref[...]
        mask_n = (nrow < t_new) & (tok - nrow >= 0)
        s_w = jnp.where(mask_w, s_w, NEG)
        s_n = jnp.where(mask_n, s_n, NEG)
        m = jnp.maximum(jnp.max(s_w, axis=0, keepdims=True), jnp.max(s_n, axis=0, keepdims=True))
        e_w = jnp.where(mask_w, jnp.exp(s_w - m), 0.0)
        e_n = jnp.where(mask_n, jnp.exp(s_n - m), 0.0)
        den = jnp.maximum(jnp.sum(e_w, axis=0, keepdims=True) + jnp.sum(e_n, axis=0, keepdims=True), 1e-30)
        owin_s[...] = finish([e_w / den, e_n / den], [vwc_ref[0], vwn_ref[0]])

        s_t = _dot(kn_ref[0].astype(BF16), q) + bn_ref[...]
        tail_sel = sel_s[pl.ds(past_len // SEL, 1), :] > 0.5
        mask_t = mask_n & tail_sel
        s_t = jnp.where(mask_t, s_t, NEG)
        m0 = jnp.max(s_t, axis=0, keepdims=True)
        p0 = jnp.where(mask_t, jnp.exp(s_t - m0), 0.0)
        m_s[...] = m0
        l_s[...] = jnp.sum(p0, axis=0, keepdims=True)
        acc_s[...] = finish([p0], [vn_ref[0]])

    keys = pps * PAGE
    kc_all = jnp.concatenate([r[0] for r in k_pages], axis=0).astype(BF16)
    vc_all = jnp.concatenate([r[0] for r in v_pages], axis=0)
    s = _dot(kc_all, q) + bs_ref[0]
    blocks = keys // SEL
    sel_blk = sel_s[pl.ds(pl.multiple_of(ch * blocks, blocks), blocks), :].astype(BF16)
    mask = _dot(e_ref[...], sel_blk) > 0.5
    s = jnp.where(mask, s, NEG)
    m_old = m_s[...]
    m_new = jnp.maximum(m_old, jnp.max(s, axis=0, keepdims=True))
    p = jnp.where(mask, jnp.exp(s - m_new), 0.0)
    alpha = jnp.exp(m_old - m_new)
    l_s[...] = alpha * l_s[...] + jnp.sum(p, axis=0, keepdims=True)
    acc_s[...] = to_col(alpha) * acc_s[...] + finish([p], [vc_all])
    m_s[...] = m_new

    @pl.when(ch == pl.num_programs(1) - 1)
    def _():
        o_slc = acc_s[...] / jnp.maximum(to_col(l_s[...]), 1e-30)
        gates = gt_ref[0]
        hg_rows = NSA_HG * t_new
        for g in range(NSA_G):
            rs = slice(g * hg_rows, (g + 1) * hg_rows)
            cs = slice(g * DH, (g + 1) * DH)
            o_ref[0, rs, :] = (gates[rs, 0:1] * ocmp_s[rs, cs] + gates[rs, 1:2] * o_slc[rs, cs]
                               + gates[rs, 2:3] * owin_s[rs, cs])


def _nsa_sample(page_table, k_pool, v_pool, q_bd, kc, vc, k_new, v_new, kw_cache, vw_cache, kw_new, vw_new, gates,
                bias_c, bias_s, bias_n, bias_w, c2s_t, hsum, hexp, expand, *, pps, t_new):
    bsz, n_pages = page_table.shape
    past_len = n_pages * PAGE
    n_cmp = kc.shape[1]
    n_sel = past_len // SEL + 1
    n_sel_pad = c2s_t.shape[0]
    rows_out = NSA_HEADS * t_new

    def page_spec(j):
        return pl.BlockSpec((1, PAGE, KV_W), lambda b, s, pt: (pt[b, s * pps + j], 0, 0))

    per_b = lambda shape: pl.BlockSpec((1,) + shape, lambda b, s, pt: (b,) + (0,) * len(shape))
    const = lambda arr: pl.BlockSpec(arr.shape, lambda b, s, pt: (0,) * arr.ndim)
    in_specs = ([page_spec(j) for j in range(pps)] * 2
                + [per_b((KV_W, LANES)), per_b((n_cmp, KV_W)), per_b((n_cmp, KV_W)),
                   per_b((PAGE, KV_W)), per_b((PAGE, KV_W)), per_b((WIN, KV_W)), per_b((WIN, KV_W)),
                   per_b((PAGE, KV_W)), per_b((PAGE, KV_W)), per_b((rows_out, 3)),
                   const(bias_c),
                   pl.BlockSpec((1, pps * PAGE, LANES), lambda b, s, pt: (s, 0, 0)),
                   const(bias_n), const(bias_w), const(c2s_t), const(hsum), const(hexp), const(expand)])
    grid_spec = pltpu.PrefetchScalarGridSpec(
        num_scalar_prefetch=1,
        grid=(bsz, n_pages // pps),
        in_specs=in_specs,
        out_specs=pl.BlockSpec((1, rows_out, DH), lambda b, s, pt: (b, 0, 0)),
        scratch_shapes=[pltpu.VMEM((n_sel_pad, LANES), F32), pltpu.VMEM((n_sel_pad, LANES), F32),
                        pltpu.VMEM((1, LANES), F32), pltpu.VMEM((1, LANES), F32),
                        pltpu.VMEM((LANES, KV_W), F32), pltpu.VMEM((LANES, KV_W), F32),
                        pltpu.VMEM((LANES, KV_W), F32)],
    )
    return pl.pallas_call(
        functools.partial(_nsa_sample_kernel, pps=pps, n_cmp=n_cmp, n_sel=n_sel, n_sel_pad=n_sel_pad,
                          past_len=past_len, t_new=t_new),
        out_shape=jax.ShapeDtypeStruct((bsz, rows_out, DH), F32),
        grid_spec=grid_spec,
        compiler_params=_cparams(("parallel", "arbitrary")),
        name="nsa_sample",
    )(page_table, *([k_pool] * pps), *([v_pool] * pps), q_bd, kc, vc, k_new, v_new, kw_cache, vw_cache, kw_new,
      vw_new, gates, bias_c, bias_s, bias_n, bias_w, c2s_t, hsum, hexp, expand)


def _post_kernel(x_ref, og_ref, on_ref, sma_ref, smb_ref, gate1_ref, sc2_ref, sh2_ref, gate2_ref, n2_ref, fg_ref,
                 wo_ref, wu_ref, wd_ref, y_ref):
    merged = sma_ref[...] * og_ref[...] + smb_ref[...] * on_ref[...]
    x1 = x_ref[...] + gate1_ref[0] * _dot(merged.astype(BF16), wo_ref[...])
    h2 = _rms(x1, n2_ref[...]) * (1.0 + sc2_ref[0]) + sh2_ref[0]
    up = jnp.maximum(_dot(h2.astype(BF16), wu_ref[...]), 0.0)
    x2 = x1 + gate2_ref[0] * _dot((up * up).astype(BF16), wd_ref[...])
    y_ref[...] = _rms(x2, fg_ref[...])


def _post(x, o_gla, o_nsa, sma, smb, gate1, scale2, shift2, gate2, n2, fg, wo, wu, wd, *, tm):
    m = x.shape[0]
    nb, r, _ = gate1.shape
    rows_per_mod = m // nb
    mod_spec = pl.BlockSpec((1, r, D_MODEL), lambda i: (i * tm // rows_per_mod, 0, 0))
    row = pl.BlockSpec((tm, D_MODEL), lambda i: (i, 0))
    return pl.pallas_call(
        _post_kernel,
        out_shape=jax.ShapeDtypeStruct((m, D_MODEL), F32),
        grid=(m // tm,),
        in_specs=[row] * 5 + [mod_spec] * 4 + [_const_spec((1, D_MODEL))] * 2
        + [_const_spec(wo.shape), _const_spec(wu.shape), _const_spec(wd.shape)],
        out_specs=row,
        compiler_params=_cparams(("parallel",)),
        name="post",
    )(x, o_gla, o_nsa, sma, smb, gate1, scale2, shift2, gate2, n2, fg, wo, wu, wd)


def _block_diag_w1(w1):
    w = w1.reshape(CMP_RATIO, CMP_STRIDE, DH, CMP_HID)
    eye = jnp.eye(NSA_G, dtype=w1.dtype)
    bd = jnp.einsum('redh,pg->epdgrh', w, eye)
    return bd.reshape(CMP_STRIDE * NSA_G * DH, NSA_G * CMP_RATIO * CMP_HID).astype(BF16)


def _cmp_to_sel_np(n_cmp, n_sel):
    cs = np.arange(n_cmp)[:, None] * CMP_STRIDE
    ss = np.arange(n_sel)[None, :] * SEL
    ov = np.minimum(cs + CMP_BLOCK, ss + SEL) - np.maximum(cs, ss)
    return np.clip(ov, 0, None).astype(np.float32) / CMP_BLOCK


def _mod_split(mod, rows):
    parts = jnp.split(mod, 6, axis=-1)
    return [p.reshape(-1, rows, D_MODEL) for p in parts]


def kernel(x_prompt, x_sample, c_prompt, c_sample, state_gla, cache_k_cmp, cache_v_cmp, cache_k_slc, cache_v_slc, cache_k_win, cache_v_win, page_table, ada_w, ada_b, norm1_g, norm2_g, w_in, gla_a2, gla_a_b, gla_norm_g, cmp_pe_k, cmp_w1_k, cmp_w2_k, cmp_pe_v, cmp_w1_v, cmp_w2_v, w_o, w_up, w_down, rel_bias, final_g):
    depth = ada_w.shape[0]
    assert depth == 1, "single-layer trunk"
    bp, tp, _ = x_prompt.shape
    bs, ts, _ = x_sample.shape
    n_pages = page_table.shape[1]
    past_len = n_pages * PAGE
    n_phys = cache_k_cmp.shape[1]
    mp, ms = bp * tp, bs * ts
    tm = 256
    tm_s = min(tm, ms)

    w = w_in[0]
    w_perm = jnp.concatenate([w[:, :3072], w[:, 3088:5648], w[:, 5696:7744], w[:, 5648:5696], w[:, 3072:3088],
                              jnp.zeros((D_MODEL, LANES - 64), w.dtype)], axis=1).astype(BF16)
    a2p = jnp.zeros((LANES, QK_W), F32).at[Z_GA:Z_GA + GLA_RANK].set(gla_a2[0]).astype(BF16)
    ab = gla_a_b[0].reshape(1, QK_W)
    g1 = norm1_g[0].reshape(1, D_MODEL)
    n2 = norm2_g[0].reshape(1, D_MODEL)
    fg = final_g.reshape(1, D_MODEL)
    gn = gla_norm_g[0].reshape(1, GLA_DV)
    wo, wu, wd = w_o[0].astype(BF16), w_up[0].astype(BF16), w_down[0].astype(BF16)
    cmp_k = (_block_diag_w1(cmp_w1_k[0]), jnp.broadcast_to(cmp_pe_k[0].reshape(1, -1), (8, CMP_BLOCK * DH)),
             cmp_w1_k[0].reshape(CMP_BLOCK * DH, CMP_HID), cmp_w2_k[0])
    cmp_v = (_block_diag_w1(cmp_w1_v[0]), jnp.broadcast_to(cmp_pe_v[0].reshape(1, -1), (8, CMP_BLOCK * DH)),
             cmp_w1_v[0].reshape(CMP_BLOCK * DH, CMP_HID), cmp_w2_v[0])

    mod = _ada(jnp.concatenate([c_prompt, c_sample], axis=0), ada_w[0], ada_b[0])
    sh1_p, sc1_p, gt1_p, sh2_p, sc2_p, gt2_p = _mod_split(mod[:bp], 1)
    sh1_s, sc1_s, gt1_s, sh2_s, sc2_s, gt2_s = _mod_split(jnp.repeat(mod[bp:], ts, axis=0), tm_s)

    xp = x_prompt.reshape(mp, D_MODEL)
    (gq, gk, gv, sgr, la, kc, vc, ks, vs, kw, vw, gates, sma, smb, qh, ksh, vsh, kwh, vwh) = _inproj(
        xp, sc1_p, sh1_p, g1, w_perm, a2p, ab, head_major=True, tm=tm)

    r3 = lambda a: a.reshape(bp, tp, a.shape[-1])
    o_gla_p, s_gla_p = _gla(r3(gq), r3(gk), r3(gv), r3(la), r3(sgr), gn, None, tb=256)

    n_seg = tp // CMP_STRIDE
    assert n_seg == LANES, "prompt attention keeps one compressed block per lane"
    n_cmp_p = n_seg - CMP_RATIO + 1
    seg = lambda a: a.reshape(mp // CMP_STRIDE, CMP_STRIDE * KV_W)
    tm_seg = min(tm, mp // CMP_STRIDE)
    kcc = _cmpfin(_mm(seg(kc), cmp_k[0], tm=tm_seg).reshape(bp, n_seg, -1), None, *cmp_k[1:], flat=False)
    vcc = _cmpfin(_mm(seg(vc), cmp_v[0], tm=tm_seg).reshape(bp, n_seg, -1), None, *cmp_v[1:], flat=False)

    n_sel_p = tp // SEL
    ii = np.arange(TQ)
    dist_t = np.stack([ii[:, None] - ii[None, :], TQ + ii[:, None] - ii[None, :]])
    bias_t = jnp.concatenate([
        jnp.transpose(rel_bias[_bucket_np(dist_t)], (0, 3, 1, 2)),
        jnp.broadcast_to(rel_bias[REL_BUCKETS - 1][None, :, None, None], (1, NSA_HEADS, TQ, LANES))], axis=0)
    dist_c = np.arange(tp)[:, None] - (np.arange(LANES)[None, :] * CMP_STRIDE + CMP_BLOCK - 1)
    bias_c = jnp.transpose(rel_bias[_bucket_np(dist_c)], (2, 0, 1))
    c2s = np.zeros((n_seg, LANES), np.float32)
    c2s[:n_cmp_p, :n_sel_p] = _cmp_to_sel_np(n_cmp_p, n_sel_p)
    expand = (np.arange(tp)[None, :] // SEL == np.arange(LANES)[:, None]).astype(np.float32)
    h4 = lambda a, n: a.reshape(n, bp, tp, DH)
    o_nsa_p = _nsa_prompt(h4(qh, NSA_HEADS), kcc, vcc, h4(ksh, NSA_G), h4(vsh, NSA_G), h4(kwh, NSA_G),
                          h4(vwh, NSA_G), gates.reshape(bp, tp, LANES), bias_c, bias_t, jnp.asarray(c2s),
                          jnp.asarray(expand, dtype=BF16), n_cmp=n_cmp_p)

    y_p = _post(xp, o_gla_p.reshape(mp, V_W), o_nsa_p.reshape(mp, D_MODEL), sma, smb, gt1_p, sc2_p, sh2_p, gt2_p,
                n2, fg, wo, wu, wd, tm=tm)

    kv5 = lambda a, b, t: a.reshape(1, b, t, NSA_G, DH)
    wp = min(WIN, tp)
    p_out = (s_gla_p[None], kv5(kc, bp, tp), kv5(vc, bp, tp), kv5(ks, bp, tp), kv5(vs, bp, tp),
             kv5(kw, bp, tp)[:, :, tp - wp:], kv5(vw, bp, tp)[:, :, tp - wp:])

    xs = x_sample.reshape(ms, D_MODEL)
    (gq, gk, gv, sgr, la, kc, vc, ks, vs, kw, vw, gates, sma, smb, nq) = _inproj(
        xs, sc1_s, sh1_s, g1, w_perm, a2p, ab, head_major=False, tm=tm_s)

    t_pad = 8
    pad_t = lambda a: jnp.pad(a.reshape(bs, ts, a.shape[-1]), ((0, 0), (0, t_pad - ts), (0, 0)))
    o_gla_s, s_gla_s = _gla(pad_t(gq), pad_t(gk), pad_t(gv), pad_t(la), pad_t(sgr), gn, state_gla[0], tb=t_pad)
    o_gla_s = o_gla_s[:, :ts].reshape(ms, V_W)

    pool_seg = lambda a: a[0].reshape(n_phys, PAGE // CMP_STRIDE, CMP_STRIDE * KV_W)
    new_seg = lambda a: jnp.pad(a.reshape(bs, ts, KV_W), ((0, 0), (0, CMP_STRIDE - ts), (0, 0))).reshape(
        bs, CMP_STRIDE * KV_W)
    n_cmp_s = past_len // CMP_STRIDE
    kcc = _cmpfin(_mm_paged(pool_seg(cache_k_cmp), page_table, cmp_k[0], pages_per_step=min(32, n_pages)),
                  _mm(new_seg(kc), cmp_k[0], tm=bs).reshape(bs, 1, -1), *cmp_k[1:], flat=True)
    vcc = _cmpfin(_mm_paged(pool_seg(cache_v_cmp), page_table, cmp_v[0], pages_per_step=min(32, n_pages)),
                  _mm(new_seg(vc), cmp_v[0], tm=bs).reshape(bs, 1, -1), *cmp_v[1:], flat=True)

    q5 = nq.reshape(bs, ts, NSA_G, NSA_HG, DH)
    q_bd = jnp.einsum('btghd,pg->bpdght', q5, jnp.eye(NSA_G, dtype=F32)).reshape(bs, KV_W, NSA_HEADS * ts)
    q_bd = jnp.pad(q_bd, ((0, 0), (0, 0), (0, LANES - NSA_HEADS * ts))).astype(BF16)

    ncol = NSA_HEADS * ts
    col_head = np.arange(ncol) // ts
    col_tok = np.arange(ncol) % ts

    def bias_cols(key_pos):
        dist = past_len + col_tok[None, :] - key_pos[:, None]
        b = rel_bias[_bucket_np(dist), col_head[None, :]]
        return jnp.pad(b, ((0, 0), (0, LANES - ncol)))

    pps = min(16, n_pages)
    bias_c_s = bias_cols(np.arange(n_cmp_s) * CMP_STRIDE + CMP_BLOCK - 1)
    bias_s_s = bias_cols(np.arange(past_len)).reshape(n_pages // pps, pps * PAGE, LANES)
    bias_n_s = bias_cols(past_len + np.arange(PAGE))
    bias_w_s = bias_cols(past_len - WIN + np.arange(WIN))
    n_sel_s = past_len // SEL + 1
    n_sel_pad = -(-n_sel_s // 8) * 8
    c2s_t = np.zeros((n_sel_pad, n_cmp_s), np.float32)
    c2s_t[:n_sel_s] = _cmp_to_sel_np(n_cmp_s, n_sel_s).T
    gt_col = np.arange(LANES) // (NSA_HG * ts) * ts + np.arange(LANES) % ts
    hsum = ((gt_col[:, None] == np.arange(LANES)[None, :]) & (np.arange(LANES)[:, None] < ncol)).astype(np.float32)
    expand_s = (np.arange(pps * PAGE)[:, None] // SEL == np.arange(pps * PAGE // SEL)[None, :]).astype(np.float32)
    gates_s = jnp.transpose(gates[:, :3 * NSA_HEADS].reshape(bs, ts, 3, NSA_HEADS), (0, 3, 1, 2)).reshape(bs, ncol, 3)
    new_rows = lambda a: jnp.pad(a.reshape(bs, ts, KV_W), ((0, 0), (0, PAGE - ts), (0, 0)))
    pool = lambda a: a[0].reshape(n_phys, PAGE, KV_W)
    wb = cache_k_win.shape[2]
    o_s = _nsa_sample(page_table, pool(cache_k_slc), pool(cache_v_slc), q_bd, kcc, vcc, new_rows(ks), new_rows(vs),
                      cache_k_win[0].reshape(bs, wb, KV_W), cache_v_win[0].reshape(bs, wb, KV_W), new_rows(kw),
                      new_rows(vw), gates_s, bias_c_s, bias_s_s, bias_n_s, bias_w_s, jnp.asarray(c2s_t),
                      jnp.asarray(hsum), jnp.asarray(hsum.T, dtype=BF16), jnp.asarray(expand_s, dtype=BF16),
                      pps=pps, t_new=ts)
    o_nsa_s = jnp.transpose(o_s.reshape(bs, NSA_HEADS, ts, DH), (0, 2, 1, 3)).reshape(ms, D_MODEL)

    y_s = _post(xs, o_gla_s, o_nsa_s, sma, smb, gt1_s, sc2_s, sh2_s, gt2_s, n2, fg, wo, wu, wd, tm=tm_s)

    kw_new = jnp.concatenate([cache_k_win[0], kv5(kw, bs, ts)[0]], axis=1)[:, ts:][None]
    vw_new = jnp.concatenate([cache_v_win[0], kv5(vw, bs, ts)[0]], axis=1)[:, ts:][None]
    s_out = (s_gla_s[None], kv5(kc, bs, ts), kv5(vc, bs, ts), kv5(ks, bs, ts), kv5(vs, bs, ts), kw_new, vw_new)

    return (y_p.reshape(bp, tp, D_MODEL), y_s.reshape(bs, ts, D_MODEL)) + p_out + s_out
```

```python
import functools
import math

import numpy as np
import jax
import jax.numpy as jnp
from jax import lax
from jax.experimental import pallas as pl
from jax.experimental.pallas import tpu as pltpu

F32 = jnp.float32
BF16 = jnp.bfloat16
HIGHEST = lax.Precision.HIGHEST

D_MODEL = 1024
GLA_HEADS = 4
GLA_DK = 128
GLA_DV = 256
GLA_RANK = 16
GLA_TAU = 16.0
GLA_CHUNK = 64
GLA_SUB = 16
NSA_HEADS = 16
DH = 64
NSA_G = 4
NSA_HG = 4
CMP_BLOCK = 32
CMP_STRIDE = 16
CMP_RATIO = 2
CMP_HID = 128
SEL = 64
SEL_SHIFT = 6
TOPK = 16
WIN = 512
REL_BUCKETS = 32
REL_MAX = 128
D_FF = 4096
EPS = 1e-6
NEG = -1e30
FORCE = 1e4
PAGE = 128

QK_W = GLA_HEADS * GLA_DK
V_W = GLA_HEADS * GLA_DV
KV_W = NSA_G * DH
LANES = 128
TQ = 128
VMEM_LIMIT = 56 * 1024 * 1024

C_GQ, C_GK, C_GV, C_GR, C_NQ = 0, 512, 1024, 2048, 3072
C_KV = 4096
C_MA, C_MB, C_Z = 5632, 6656, 7680
W_COLS = 7808
Z_GA = 48


def _cparams(sem):
    return pltpu.CompilerParams(dimension_semantics=sem, vmem_limit_bytes=VMEM_LIMIT)


def _const_spec(shape):
    return pl.BlockSpec(shape, lambda *a: (0,) * len(shape), pipeline_mode=pl.Buffered(1))


def _sigmoid(x):
    return 1.0 / (1.0 + jnp.exp(-x))


def _rms(x, g):
    return x * lax.rsqrt(jnp.mean(x * x, axis=-1, keepdims=True) + EPS) * g


def _dot(a, b, **kw):
    return jnp.dot(a, b, preferred_element_type=F32, **kw)


def _dot_nt(a, b):
    return lax.dot_general(a, b, (((1,), (1,)), ((), ())), preferred_element_type=F32)


def _dot_tn(a, b):
    return lax.dot_general(a, b, (((0,), (0,)), ((), ())), preferred_element_type=F32)


def _bucket_np(dist):
    n = np.maximum(dist, 0)
    nf = np.maximum(n, 1).astype(np.float64)
    large = 16 + (np.log(nf / 16.0) / math.log(REL_MAX / 16.0) * 16.0).astype(np.int64)
    large = np.minimum(large, REL_BUCKETS - 1)
    return np.where(n < 16, n, large).astype(np.int32)


def _ada_kernel(c_ref, w_ref, b_ref, o_ref):
    c = c_ref[...]
    o_ref[...] = _dot(c * _sigmoid(c), w_ref[...], precision=HIGHEST) + b_ref[...]


def _ada(c, w, b):
    r, d = c.shape
    n = w.shape[1]
    return pl.pallas_call(
        _ada_kernel,
        out_shape=jax.ShapeDtypeStruct((r, n), F32),
        grid=(n // d,),
        in_specs=[pl.BlockSpec((r, d), lambda j: (0, 0)),
                  pl.BlockSpec((d, d), lambda j: (0, j)),
                  pl.BlockSpec((1, d), lambda j: (0, j))],
        out_specs=pl.BlockSpec((r, d), lambda j: (0, j)),
        compiler_params=_cparams(("arbitrary",)),
        name="ada",
    )(c, w, b.reshape(1, n))


def _inproj_kernel(x_ref, sc_ref, sh_ref, g_ref, w_ref, a2_ref, ab_ref, *outs, head_major):
    (gq_o, gk_o, gv_o, sgr_o, la_o, kc_o, vc_o, ks_o, vs_o, kw_o, vw_o, gt_o, sma_o, smb_o, *rest) = outs
    h = _rms(x_ref[...], g_ref[...]) * (1.0 + sc_ref[0]) + sh_ref[0]
    hb = h.astype(BF16)

    def proj(lo, hi):
        return _dot(hb, w_ref[:, lo:hi])

    gq_o[...] = proj(C_GQ, C_GK) * (GLA_DK ** -0.5)
    gk_o[...] = proj(C_GK, C_GV)
    gv_o[...] = proj(C_GV, C_GR)
    gr = proj(C_GR, C_NQ)
    sgr_o[...] = gr * _sigmoid(gr)
    nq = proj(C_NQ, C_KV) * (DH ** -0.5)
    kv = proj(C_KV, C_MA)
    for n, o in enumerate((kc_o, vc_o, ks_o, vs_o, kw_o, vw_o)):
        o[...] = kv[:, n * KV_W:(n + 1) * KV_W]
    sma_o[...] = _sigmoid(proj(C_MA, C_MB))
    smb_o[...] = _sigmoid(proj(C_MB, C_Z))
    z = proj(C_Z, W_COLS)
    gt_o[...] = _sigmoid(z)
    pre = _dot(z.astype(BF16), a2_ref[...]) + ab_ref[...]
    la_o[...] = (jnp.minimum(pre, 0.0) - jnp.log(1.0 + jnp.exp(-jnp.abs(pre)))) * (1.0 / GLA_TAU)
    if head_major:
        qt_o, ksh_o, vst_o, kwh_o, vwt_o, gtt_o = rest

        def put_transposed(o, x):
            for c in range(x.shape[1] // LANES):
                t = jnp.transpose(x[:, c * LANES:(c + 1) * LANES]).astype(BF16)
                o[2 * c] = t[:DH]
                o[2 * c + 1] = t[DH:]

        put_transposed(qt_o, nq)
        put_transposed(vst_o, kv[:, 3 * KV_W:4 * KV_W])
        put_transposed(vwt_o, kv[:, 5 * KV_W:6 * KV_W])
        for n, o in ((2, ksh_o), (4, kwh_o)):
            for gg in range(NSA_G):
                lo = n * KV_W + gg * DH
                o[gg] = kv[:, lo:lo + DH].astype(BF16)
        gtt_o[...] = jnp.transpose(_sigmoid(z))
    else:
        (nq_o,) = rest
        nq_o[...] = nq


def _inproj(x, scale, shift, g1, w, a2p, ab, *, head_major, tm):
    m = x.shape[0]
    nb, r, _ = scale.shape
    rows_per_mod = m // nb
    mod_spec = pl.BlockSpec((1, r, D_MODEL), lambda i: (i * tm // rows_per_mod, 0, 0))
    row = lambda wdt: pl.BlockSpec((tm, wdt), lambda i: (i, 0))
    widths = [QK_W, QK_W, V_W, V_W, QK_W] + [KV_W] * 6 + [LANES, D_MODEL, D_MODEL]
    out_shape = [jax.ShapeDtypeStruct((m, wd), F32) for wd in widths]
    out_specs = [row(wd) for wd in widths]
    if head_major:
        def rows_major(n):
            out_shape.append(jax.ShapeDtypeStruct((n, m, DH), BF16))
            out_specs.append(pl.BlockSpec((n, tm, DH), lambda i: (0, i, 0)))

        def lanes_major(n):
            out_shape.append(jax.ShapeDtypeStruct((n, DH, m), BF16))
            out_specs.append(pl.BlockSpec((n, DH, tm), lambda i: (0, 0, i)))

        lanes_major(NSA_HEADS)
        rows_major(NSA_G)
        lanes_major(NSA_G)
        rows_major(NSA_G)
        lanes_major(NSA_G)
        out_shape.append(jax.ShapeDtypeStruct((LANES, m), F32))
        out_specs.append(pl.BlockSpec((LANES, tm), lambda i: (0, i)))
    else:
        out_shape.append(jax.ShapeDtypeStruct((m, D_MODEL), F32))
        out_specs.append(row(D_MODEL))
    return pl.pallas_call(
        functools.partial(_inproj_kernel, head_major=head_major),
        out_shape=out_shape,
        grid=(m // tm,),
        in_specs=[row(D_MODEL), mod_spec, mod_spec, _const_spec((1, D_MODEL)), _const_spec((D_MODEL, W_COLS)),
                  _const_spec((LANES, QK_W)), _const_spec((1, QK_W))],
        out_specs=out_specs,
        compiler_params=_cparams(("parallel",)),
        name="inproj",
    )(x, scale, shift, g1, w, a2p, ab)


def _gla_kernel(*refs, chunk, sub, n_chunk, has_init):
    if has_init:
        q_ref, k_ref, v_ref, la_ref, sgr_ref, gn_ref, s0_ref, o_ref, so_ref, s_scr = refs
    else:
        q_ref, k_ref, v_ref, la_ref, sgr_ref, gn_ref, o_ref, so_ref, s_scr = refs
    t = pl.program_id(2)

    @pl.when(t == 0)
    def _():
        s_scr[...] = s0_ref[0, 0] if has_init else jnp.zeros_like(s_scr)

    tril = (lax.broadcasted_iota(jnp.int32, (chunk, chunk), 0)
            >= lax.broadcasted_iota(jnp.int32, (chunk, chunk), 1)).astype(F32)
    sub_row = lax.broadcasted_iota(jnp.int32, (sub, 1), 0)

    def one_chunk(c, carry):
        r0 = pl.multiple_of(c * chunk, chunk)
        q = q_ref[0, pl.ds(r0, chunk), :]
        k = k_ref[0, pl.ds(r0, chunk), :]
        v = v_ref[0, pl.ds(r0, chunk), :]
        b = _dot(tril, la_ref[0, pl.ds(r0, chunk), :], precision=HIGHEST)
        s_old = s_scr[...]
        vb = v.astype(BF16)
        o_inter = _dot((q * jnp.exp(b)).astype(BF16), s_old.astype(BF16))
        blocks = []
        for blk in range(chunk // sub):
            sl = slice(blk * sub, (blk + 1) * sub)
            q_i, k_i, v_i, b_i = q[sl], k[sl], v[sl], b[sl]
            acc = jnp.zeros((sub, GLA_DV), F32)
            for j in range(sub):
                w = q_i * k_i[j:j + 1] * jnp.exp(jnp.minimum(b_i - b_i[j:j + 1], 0.0))
                a = jnp.sum(w, axis=-1, keepdims=True)
                acc = acc + jnp.where(sub_row >= j, a, 0.0) * v_i[j:j + 1]
            if blk > 0:
                n = blk * sub
                ref_row = b_i[0:1]
                qp = (q_i * jnp.exp(b_i - ref_row)).astype(BF16)
                kp = (k[:n] * jnp.exp(ref_row - b[:n])).astype(BF16)
                acc = acc + _dot(_dot_nt(qp, kp).astype(BF16), vb[:n])
            blocks.append(acc)
        o = o_inter + (jnp.concatenate(blocks, axis=0) if len(blocks) > 1 else blocks[0])
        b_last = b[chunk - 1:chunk]
        kp = (k * jnp.exp(b_last - b)).astype(BF16)
        decay_col = jnp.transpose(jnp.broadcast_to(jnp.exp(b_last), (GLA_DK, GLA_DK)))[:, 0:1]
        s_scr[...] = decay_col * s_old + _dot_tn(kp, vb)
        o_ref[0, pl.ds(r0, chunk), :] = _rms(o, gn_ref[...]) * sgr_ref[0, pl.ds(r0, chunk), :]
        return carry

    lax.fori_loop(0, n_chunk, one_chunk, 0)

    @pl.when(t == pl.num_programs(2) - 1)
    def _():
        so_ref[0, 0] = s_scr[...]


def _gla(q, k, v, la, sgr, gn, s0, *, tb):
    bsz, t, _ = q.shape
    chunk = min(GLA_CHUNK, tb)
    sub = min(GLA_SUB, chunk)
    blk = lambda wd: pl.BlockSpec((1, tb, wd), lambda b, h, i: (b, i, h))
    st_spec = pl.BlockSpec((1, 1, GLA_DK, GLA_DV), lambda b, h, i: (b, h, 0, 0))
    in_specs = [blk(GLA_DK), blk(GLA_DK), blk(GLA_DV), blk(GLA_DK), blk(GLA_DV),
                pl.BlockSpec((1, GLA_DV), lambda b, h, i: (0, 0))]
    args = [q, k, v, la, sgr, gn]
    if s0 is not None:
        in_specs.append(st_spec)
        args.append(s0)
    return pl.pallas_call(
        functools.partial(_gla_kernel, chunk=chunk, sub=sub, n_chunk=tb // chunk, has_init=s0 is not None),
        out_shape=[jax.ShapeDtypeStruct((bsz, t, V_W), F32),
                   jax.ShapeDtypeStruct((bsz, GLA_HEADS, GLA_DK, GLA_DV), F32)],
        grid=(bsz, GLA_HEADS, t // tb),
        in_specs=in_specs,
        out_specs=[blk(GLA_DV), st_spec],
        scratch_shapes=[pltpu.VMEM((GLA_DK, GLA_DV), F32)],
        compiler_params=_cparams(("parallel", "parallel", "arbitrary")),
        name="gla",
    )(*args)


def _mm_kernel(x_ref, w_ref, o_ref):
    o_ref[...] = _dot(x_ref[...].astype(BF16), w_ref[...])


def _mm(x, w, *, tm):
    m, kdim = x.shape
    n = w.shape[1]
    return pl.pallas_call(
        _mm_kernel,
        out_shape=jax.ShapeDtypeStruct((m, n), F32),
        grid=(m // tm,),
        in_specs=[pl.BlockSpec((tm, kdim), lambda i: (i, 0)), _const_spec((kdim, n))],
        out_specs=pl.BlockSpec((tm, n), lambda i: (i, 0)),
        compiler_params=_cparams(("parallel",)),
        name="cmp_mm",
    )(x, w)


CMP_TAPS = 4


def _cmp_paged_kernel(*refs, pps):
    pages = refs[1:1 + pps]
    w_ref, o_ref, tok_s = refs[1 + pps:]
    heads_per_slab = LANES // DH
    n_slab = NSA_G // heads_per_slab
    for p in range(pps):
        t = jnp.transpose(pages[p][0].reshape(KV_W, PAGE))
        for sl in range(n_slab):
            tok_s[sl, p * PAGE:(p + 1) * PAGE, :] = t[:, sl * LANES:(sl + 1) * LANES]
    n_seg = pps * PAGE // CMP_STRIDE
    width = CMP_RATIO * CMP_HID
    first_half = lax.broadcasted_iota(jnp.int32, (1, LANES), 1) < DH
    for sl in range(n_slab):
        x = [tok_s[sl, pl.ds(e, n_seg, stride=CMP_STRIDE), :] for e in range(CMP_STRIDE)]
        r = [pltpu.roll(v, DH, 1) for v in x]
        for odd in range(heads_per_slab):
            g = sl * heads_per_slab + odd
            acc = None
            for j in range(CMP_STRIDE // CMP_TAPS):
                pieces = []
                for e in range(j * CMP_TAPS, (j + 1) * CMP_TAPS, 2):
                    lo, hi = (r[e], x[e + 1]) if odd else (x[e], r[e + 1])
                    pieces.append(jnp.where(first_half, lo, hi))
                term = _dot(jnp.concatenate(pieces, axis=1).astype(BF16), w_ref[j])
                acc = term if acc is None else acc + term
            o_ref[0, :, g * width:(g + 1) * width] = acc


def _cmp_paged(pool_t, page_table, w_taps, *, pages_per_step):
    bsz, n_pages = page_table.shape
    pps = pages_per_step
    n_seg = pps * PAGE // CMP_STRIDE
    width = NSA_G * CMP_RATIO * CMP_HID

    def page_spec(j):
        return pl.BlockSpec((1, NSA_G, DH, PAGE), lambda b, s, pt: (pt[b, s * pps + j], 0, 0, 0))

    grid_spec = pltpu.PrefetchScalarGridSpec(
        num_scalar_prefetch=1,
        grid=(bsz, n_pages // pps),
        in_specs=[page_spec(j) for j in range(pps)]
        + [pl.BlockSpec(w_taps.shape, lambda b, s, pt: (0, 0, 0), pipeline_mode=pl.Buffered(1))],
        out_specs=pl.BlockSpec((1, n_seg, width), lambda b, s, pt: (b, s, 0)),
        scratch_shapes=[pltpu.VMEM((NSA_G * DH // LANES, pps * PAGE, LANES), F32)],
    )
    return pl.pallas_call(
        functools.partial(_cmp_paged_kernel, pps=pps),
        out_shape=jax.ShapeDtypeStruct((bsz, n_pages * PAGE // CMP_STRIDE, width), F32),
        grid_spec=grid_spec,
        compiler_params=_cparams(("parallel", "arbitrary")),
        name="cmp_paged",
    )(page_table, *([pool_t] * pps), w_taps)


def _cmpfin_kernel(*refs, n, has_new, layout):
    if has_new:
        p_ref, pn_ref, pe_ref, w1_ref, w2_ref, o_ref = refs
    else:
        p_ref, pe_ref, w1_ref, w2_ref, o_ref = refs
    pe_term = _dot(pe_ref[...], w1_ref[...], precision=HIGHEST)[0:1]
    w2 = w2_ref[...].astype(BF16)
    rowi = lax.broadcasted_iota(jnp.int32, (n, 1), 0)
    for g in range(NSA_G):
        base = g * CMP_RATIO * CMP_HID
        first = p_ref[0, :, base:base + CMP_HID]
        second = pltpu.roll(p_ref[0, :, base + CMP_HID:base + 2 * CMP_HID], n - 1, 0)
        if has_new:
            second = jnp.where(rowi == n - 1, pn_ref[0, :, base + CMP_HID:base + 2 * CMP_HID], second)
        acc = first + second + pe_term
        hid = (acc * _sigmoid(acc)).astype(BF16)
        if layout == 'flat':
            o_ref[0, :, g * DH:(g + 1) * DH] = _dot(hid, w2)
        elif layout == 'rows':
            o_ref[0, g] = _dot(hid, w2)
        else:
            o_ref[0, g] = _dot_nt(w2, hid)


def _cmpfin(parts, parts_new, pe, w1, w2, *, layout):
    bsz, n, width = parts.shape
    in_specs = [pl.BlockSpec((1, n, width), lambda b: (b, 0, 0))]
    args = [parts]
    if parts_new is not None:
        in_specs.append(pl.BlockSpec((1, 1, width), lambda b: (b, 0, 0)))
        args.append(parts_new)
    in_specs += [_const_spec(pe.shape), _const_spec(w1.shape), _const_spec(w2.shape)]
    if layout == 'flat':
        out_shape = jax.ShapeDtypeStruct((bsz, n, KV_W), F32)
        out_spec = pl.BlockSpec((1, n, KV_W), lambda b: (b, 0, 0))
    else:
        inner = (n, DH) if layout == 'rows' else (DH, n)
        out_shape = jax.ShapeDtypeStruct((bsz, NSA_G) + inner, F32)
        out_spec = pl.BlockSpec((1, NSA_G) + inner, lambda b: (b, 0, 0, 0))
    return pl.pallas_call(
        functools.partial(_cmpfin_kernel, n=n, has_new=parts_new is not None, layout=layout),
        out_shape=out_shape,
        grid=(bsz,),
        in_specs=in_specs,
        out_specs=out_spec,
        compiler_params=_cparams(("parallel",)),
        name="cmp_finish",
    )(*args, pe, w1, w2)


def _masked_softmax(s, mask, axis):
    s = jnp.where(mask, s, NEG)
    m = jnp.max(s, axis=axis, keepdims=True)
    e = jnp.where(mask, jnp.exp(s - m), 0.0)
    return e / jnp.maximum(jnp.sum(e, axis=axis, keepdims=True), 1e-30)


CMP_WIN_ROWS = 24
CMP_WIN_PAD = 16


def _nsa_prompt_kernel(qt_ref, kc_ref, vct_ref, ks_ref, vst_ref, kw_ref, vwt_ref, gtt_ref, bt_ref, wc_ref, far_ref,
                       c2s_ref, o_ref, sel_s, bc_s, m_s, l_s, acc_s, *, n_cmp, n_sel, n_tiles):
    g = pl.program_id(1)
    i = pl.program_id(2)
    cols = NSA_HG * TQ
    q4 = jnp.concatenate([qt_ref[h] for h in range(NSA_HG)], axis=1)
    tok = i * TQ + lax.broadcasted_iota(jnp.int32, (1, TQ), 1)
    tok4 = jnp.concatenate([tok] * NSA_HG, axis=1)
    row = lax.broadcasted_iota(jnp.int32, (TQ, 1), 0)

    bc_s[...] = jnp.broadcast_to(far_ref[0], bc_s.shape)
    bc_s[pl.ds(pl.multiple_of(i * (TQ // CMP_STRIDE), TQ // CMP_STRIDE), CMP_WIN_ROWS), :] = wc_ref[0]
    s_c = _dot(kc_ref[0, 0].astype(BF16), q4) + bc_s[CMP_WIN_PAD:CMP_WIN_PAD + TQ, :]
    mask_c = (tok4 - (row * CMP_STRIDE + CMP_BLOCK - 1) >= 0) & (row < n_cmp)
    p_c = _masked_softmax(s_c, mask_c, 0)
    o_cmp = _dot(vct_ref[0, 0].astype(BF16), p_c.astype(BF16))

    p_sum = p_c[:, 0:TQ] + p_c[:, TQ:2 * TQ] + p_c[:, 2 * TQ:3 * TQ] + p_c[:, 3 * TQ:4 * TQ]
    imp = _dot(c2s_ref[...], p_sum, precision=HIGHEST)
    blk = lax.broadcasted_iota(jnp.int32, (n_sel, 1), 0)
    cur = jnp.right_shift(tok, SEL_SHIFT)
    forced = (blk == 0) | (blk == cur) | (blk == cur - 1)
    valid = blk * SEL <= tok
    score = jnp.where(forced, FORCE, jnp.where(valid, imp, -FORCE))
    cnt = jnp.zeros((n_sel, TQ), jnp.int32)
    for b in range(n_sel):
        s_b = score[b:b + 1]
        ahead = (s_b > score) | ((s_b == score) & (blk > b))
        cnt = cnt + ahead.astype(jnp.int32)
    sel_s[...] = jnp.where(cnt < min(TOPK, n_sel), 1.0, 0.0)

    def attend(k_ref, vt_ref, lo, hi, use_sel):
        m_s[...] = jnp.full_like(m_s, NEG)
        l_s[...] = jnp.zeros_like(l_s)
        acc_s[...] = jnp.zeros_like(acc_s)

        def scores(kt):
            kt_ld = jnp.minimum(kt, n_tiles - 1)
            k0 = pl.multiple_of(kt_ld * TQ, TQ)
            s = _dot(k_ref[0, 0, pl.ds(k0, TQ), :], q4) + bt_ref[jnp.clip(i - kt, 0, 2), 0]
            dist = tok - (kt * TQ + row)
            if use_sel:
                blocks_per_tile = TQ // SEL
                sel_add = None
                for j in reversed(range(blocks_per_tile)):
                    add_j = jnp.where(sel_s[pl.ds(kt_ld * blocks_per_tile + j, 1), :] > 0.5, 0.0, NEG)
                    sel_add = add_j if sel_add is None else jnp.where(row < (j + 1) * SEL, add_j, sel_add)
                madd = jnp.where(dist >= 0, sel_add, NEG)
            else:
                madd = jnp.where(lax.bitcast_convert_type(dist, jnp.uint32) < WIN, 0.0, NEG)
            return s + jnp.concatenate([madd] * NSA_HG, axis=1), k0

        def body(c, carry):
            s_a, k0_a = scores(lo + 2 * c)
            s_b, k0_b = scores(lo + 2 * c + 1)
            m_old = m_s[...]
            m_new = jnp.maximum(m_old, jnp.maximum(jnp.max(s_a, axis=0, keepdims=True),
                                                   jnp.max(s_b, axis=0, keepdims=True)))
            p_a = jnp.exp(s_a - m_new)
            p_b = jnp.exp(s_b - m_new)
            alpha = jnp.exp(m_old - m_new)
            l_s[...] = alpha * l_s[...] + jnp.sum(p_a, axis=0, keepdims=True) + jnp.sum(p_b, axis=0, keepdims=True)
            acc_s[...] = (alpha * acc_s[...] + _dot(vt_ref[0, :, pl.ds(k0_a, TQ)], p_a.astype(BF16))
                          + _dot(vt_ref[0, :, pl.ds(k0_b, TQ)], p_b.astype(BF16)))
            m_s[...] = m_new
            return carry

        lax.fori_loop(0, (hi - lo + 1) // 2, body, 0)
        return acc_s[...] / jnp.maximum(l_s[...], 1e-30)

    o_slc = attend(ks_ref, vst_ref, 0, i + 1, True)
    o_win = attend(kw_ref, vwt_ref, jnp.maximum(i - WIN // TQ, 0), i + 1, False)

    heads = []
    for h in range(NSA_HG):
        cs = slice(h * TQ, (h + 1) * TQ)
        o_h = jnp.zeros((DH, TQ), F32)
        for br, o_br in enumerate((o_cmp, o_slc, o_win)):
            gate = gtt_ref[pl.ds(br * NSA_HEADS + g * NSA_HG + h, 1), :]
            o_h = o_h + gate * o_br[:, cs]
        heads.append(o_h)
    o_ref[0] = jnp.transpose(jnp.concatenate(heads, axis=0))


def _nsa_prompt(qt, kc, vct, ksh, vst, kwh, vwt, gtt, bias_t, bias_wc, bias_far, c2s_t, *, bsz, n_cmp):
    t = qt.shape[2] // bsz
    nq = t // TQ
    n_sel = t // SEL
    cols = NSA_HG * TQ
    k_spec = pl.BlockSpec((1, 1, t, DH), lambda b, g, i: (g, b, 0, 0))
    vt_spec = pl.BlockSpec((1, DH, t), lambda b, g, i: (g, 0, b))
    return pl.pallas_call(
        functools.partial(_nsa_prompt_kernel, n_cmp=n_cmp, n_sel=n_sel, n_tiles=nq),
        out_shape=jax.ShapeDtypeStruct((bsz, t, D_MODEL), F32),
        grid=(bsz, NSA_G, nq),
        in_specs=[pl.BlockSpec((NSA_HG, DH, TQ), lambda b, g, i: (g, 0, b * nq + i)),
                  pl.BlockSpec((1, 1, kc.shape[2], DH), lambda b, g, i: (b, g, 0, 0)),
                  pl.BlockSpec((1, 1, DH, vct.shape[3]), lambda b, g, i: (b, g, 0, 0)),
                  k_spec, vt_spec, k_spec, vt_spec,
                  pl.BlockSpec((LANES, TQ), lambda b, g, i: (0, b * nq + i)),
                  pl.BlockSpec((3, 1, TQ, cols), lambda b, g, i: (0, g, 0, 0)),
                  pl.BlockSpec((1, CMP_WIN_ROWS, cols), lambda b, g, i: (g, 0, 0)),
                  pl.BlockSpec((1, 1, cols), lambda b, g, i: (g, 0, 0)),
                  pl.BlockSpec(c2s_t.shape, lambda b, g, i: (0, 0))],
        out_specs=pl.BlockSpec((1, TQ, NSA_HG * DH), lambda b, g, i: (b, i, g)),
        scratch_shapes=[pltpu.VMEM((n_sel, TQ), F32), pltpu.VMEM((CMP_WIN_PAD + TQ, cols), F32),
                        pltpu.VMEM((1, cols), F32), pltpu.VMEM((1, cols), F32), pltpu.VMEM((DH, cols), F32)],
        compiler_params=_cparams(("parallel", "parallel", "arbitrary")),
        name="nsa_prompt",
    )(qt, kc, vct, ksh, vst, kwh, vwt, gtt, bias_t, bias_wc, bias_far, c2s_t)


def _nsa_sample_kernel(*refs, pps, n_cmp, n_sel, n_sel_pad, past_len, t_new):
    pt_ref = refs[0]
    del pt_ref
    k_pages = refs[1:1 + pps]
    v_pages = refs[1 + pps:1 + 2 * pps]
    (q_ref, kc_ref, vc_ref, kn_ref, vn_ref, kwc_ref, vwc_ref, kwn_ref, vwn_ref, gt_ref, bc_ref, bs_ref, bn_ref,
     bw_ref, c2s_ref, hsum_ref, hexp_ref, e_ref, o_ref,
     sel_s, score_s, m_s, l_s, acc_s, ocmp_s, owin_s) = refs[1 + 2 * pps:]
    ch = pl.program_id(1)
    q = q_ref[0]
    col = lax.broadcasted_iota(jnp.int32, (1, LANES), 1)
    tok = col & (t_new - 1)
    qpos = past_len + tok

    def finish(p_t_list, v_list, v_is_transposed=()):
        out = None
        for n, (p_t, v) in enumerate(zip(p_t_list, v_list)):
            p = jnp.transpose(p_t).astype(BF16)
            term = _dot_nt(p, v.astype(BF16)) if n in v_is_transposed else _dot(p, v.astype(BF16))
            out = term if out is None else out + term
        return out

    def to_col(row_vec):
        return jnp.transpose(jnp.broadcast_to(row_vec, (LANES, LANES)))[:, 0:1]

    @pl.when(ch == 0)
    def _():
        crow = lax.broadcasted_iota(jnp.int32, (n_cmp, 1), 0)
        s_c = _dot(kc_ref[0].astype(BF16), q) + bc_ref[...]
        mask_c = (qpos - (crow * CMP_STRIDE + CMP_BLOCK - 1)) >= 0
        p_c = _masked_softmax(s_c, mask_c, 0)
        ocmp_s[...] = finish([p_c], [vc_ref[0]])
        p_sum = _dot(p_c, hsum_ref[...], precision=HIGHEST)
        imp = _dot(c2s_ref[...], p_sum, precision=HIGHEST)
        blk = lax.broadcasted_iota(jnp.int32, (n_sel_pad, 1), 0)
        cur = jnp.right_shift(qpos, SEL_SHIFT)
        forced = (blk == 0) | (blk == cur) | (blk == cur - 1)
        valid = blk * SEL <= qpos
        score = jnp.where(forced, FORCE, jnp.where(valid, imp, -FORCE))
        score = jnp.where(blk < n_sel, score, -3.0 * FORCE)
        score_s[...] = score

        def rank_body(b, cnt):
            s_b = score_s[pl.ds(b, 1), :]
            ahead = (s_b > score) | ((s_b == score) & (blk > b))
            return cnt + ahead.astype(jnp.int32)

        cnt = lax.fori_loop(0, n_sel, rank_body, jnp.zeros((n_sel_pad, LANES), jnp.int32))
        sel = jnp.where((cnt < min(TOPK, n_sel)) & (blk < n_sel), 1.0, 0.0).astype(BF16)
        sel_s[...] = _dot(sel, hexp_ref[...])

        jrow = lax.broadcasted_iota(jnp.int32, (WIN, 1), 0)
        nrow = lax.broadcasted_iota(jnp.int32, (PAGE, 1), 0)
        s_w = _dot(jnp.transpose(kwc_ref[0]).astype(BF16), q) + bw_ref[...]
        dist_w = qpos - (past_len - WIN + jrow)
        mask_w = (dist_w >= 0) & (dist_w < WIN)
        s_n = _dot(kwn_ref[0].astype(BF16), q) + bn_ref[...]
        mask_n = (nrow < t_new) & (tok - nrow >= 0)
        s_w = jnp.where(mask_w, s_w, NEG)
        s_n = jnp.where(mask_n, s_n, NEG)
        m = jnp.maximum(jnp.max(s_w, axis=0, keepdims=True), jnp.max(s_n, axis=0, keepdims=True))
        e_w = jnp.where(mask_w, jnp.exp(s_w - m), 0.0)
        e_n = jnp.where(mask_n, jnp.exp(s_n - m), 0.0)
        den = jnp.maximum(jnp.sum(e_w, axis=0, keepdims=True) + jnp.sum(e_n, axis=0, keepdims=True), 1e-30)
        owin_s[...] = finish([e_w / den, e_n / den], [vwc_ref[0], vwn_ref[0]], v_is_transposed=(0,))

        s_t = _dot(kn_ref[0].astype(BF16), q) + bn_ref[...]
        tail_sel = sel_s[pl.ds(past_len // SEL, 1), :] > 0.5
        mask_t = mask_n & tail_sel
        s_t = jnp.where(mask_t, s_t, NEG)
        m0 = jnp.max(s_t, axis=0, keepdims=True)
        p0 = jnp.where(mask_t, jnp.exp(s_t - m0), 0.0)
        m_s[...] = m0
        l_s[...] = jnp.sum(p0, axis=0, keepdims=True)
        acc_s[...] = finish([p0], [vn_ref[0]])

    keys = pps * PAGE
    kc_all = jnp.concatenate([jnp.transpose(r[0].reshape(KV_W, PAGE)) for r in k_pages], axis=0).astype(BF16)
    vt_all = jnp.concatenate([r[0].reshape(KV_W, PAGE) for r in v_pages], axis=1)
    s = _dot(kc_all, q) + bs_ref[0]
    blocks = keys // SEL
    sel_blk = sel_s[pl.ds(pl.multiple_of(ch * blocks, blocks), blocks), :].astype(BF16)
    mask = _dot(e_ref[...], sel_blk) > 0.5
    s = jnp.where(mask, s, NEG)
    m_old = m_s[...]
    m_new = jnp.maximum(m_old, jnp.max(s, axis=0, keepdims=True))
    p = jnp.where(mask, jnp.exp(s - m_new), 0.0)
    alpha = jnp.exp(m_old - m_new)
    l_s[...] = alpha * l_s[...] + jnp.sum(p, axis=0, keepdims=True)
    acc_s[...] = to_col(alpha) * acc_s[...] + finish([p], [vt_all], v_is_transposed=(0,))
    m_s[...] = m_new

    @pl.when(ch == pl.num_programs(1) - 1)
    def _():
        o_slc = acc_s[...] / jnp.maximum(to_col(l_s[...]), 1e-30)
        gates = gt_ref[0]
        hg_rows = NSA_HG * t_new
        for g in range(NSA_G):
            rs = slice(g * hg_rows, (g + 1) * hg_rows)
            cs = slice(g * DH, (g + 1) * DH)
            o_ref[0, rs, :] = (gates[rs, 0:1] * ocmp_s[rs, cs] + gates[rs, 1:2] * o_slc[rs, cs]
                               + gates[rs, 2:3] * owin_s[rs, cs])


def _nsa_sample(page_table, k_pool, v_pool, q_bd, kc, vc, k_new, v_new, kw_cache, vw_cache, kw_new, vw_new, gates,
                bias_c, bias_s, bias_n, bias_w, c2s_t, hsum, hexp, expand, *, pps, t_new):
    bsz, n_pages = page_table.shape
    past_len = n_pages * PAGE
    n_cmp = kc.shape[1]
    n_sel = past_len // SEL + 1
    n_sel_pad = c2s_t.shape[0]
    rows_out = NSA_HEADS * t_new

    def page_spec(j):
        return pl.BlockSpec((1, NSA_G, DH, PAGE), lambda b, s, pt: (pt[b, s * pps + j], 0, 0, 0))

    per_b = lambda shape: pl.BlockSpec((1,) + shape, lambda b, s, pt: (b,) + (0,) * len(shape))
    const = lambda arr: pl.BlockSpec(arr.shape, lambda b, s, pt: (0,) * arr.ndim)
    in_specs = ([page_spec(j) for j in range(pps)] * 2
                + [per_b((KV_W, LANES)), per_b((n_cmp, KV_W)), per_b((n_cmp, KV_W)),
                   per_b((PAGE, KV_W)), per_b((PAGE, KV_W)), per_b((KV_W, WIN)), per_b((KV_W, WIN)),
                   per_b((PAGE, KV_W)), per_b((PAGE, KV_W)), per_b((rows_out, 3)),
                   const(bias_c),
                   pl.BlockSpec((1, pps * PAGE, LANES), lambda b, s, pt: (s, 0, 0)),
                   const(bias_n), const(bias_w), const(c2s_t), const(hsum), const(hexp), const(expand)])
    grid_spec = pltpu.PrefetchScalarGridSpec(
        num_scalar_prefetch=1,
        grid=(bsz, n_pages // pps),
        in_specs=in_specs,
        out_specs=pl.BlockSpec((1, rows_out, DH), lambda b, s, pt: (b, 0, 0)),
        scratch_shapes=[pltpu.VMEM((n_sel_pad, LANES), F32), pltpu.VMEM((n_sel_pad, LANES), F32),
                        pltpu.VMEM((1, LANES), F32), pltpu.VMEM((1, LANES), F32),
                        pltpu.VMEM((LANES, KV_W), F32), pltpu.VMEM((LANES, KV_W), F32),
                        pltpu.VMEM((LANES, KV_W), F32)],
    )
    return pl.pallas_call(
        functools.partial(_nsa_sample_kernel, pps=pps, n_cmp=n_cmp, n_sel=n_sel, n_sel_pad=n_sel_pad,
                          past_len=past_len, t_new=t_new),
        out_shape=jax.ShapeDtypeStruct((bsz, rows_out, DH), F32),
        grid_spec=grid_spec,
        compiler_params=_cparams(("parallel", "arbitrary")),
        name="nsa_sample",
    )(page_table, *([k_pool] * pps), *([v_pool] * pps), q_bd, kc, vc, k_new, v_new, kw_cache, vw_cache, kw_new,
      vw_new, gates, bias_c, bias_s, bias_n, bias_w, c2s_t, hsum, hexp, expand)


def _post_kernel(x_ref, og_ref, on_ref, sma_ref, smb_ref, gate1_ref, sc2_ref, sh2_ref, gate2_ref, n2_ref, fg_ref,
                 wo_ref, wu_ref, wd_ref, y_ref):
    merged = sma_ref[...] * og_ref[...] + smb_ref[...] * on_ref[...]
    x1 = x_ref[...] + gate1_ref[0] * _dot(merged.astype(BF16), wo_ref[...])
    h2 = _rms(x1, n2_ref[...]) * (1.0 + sc2_ref[0]) + sh2_ref[0]
    up = jnp.maximum(_dot(h2.astype(BF16), wu_ref[...]), 0.0)
    x2 = x1 + gate2_ref[0] * _dot((up * up).astype(BF16), wd_ref[...])
    y_ref[...] = _rms(x2, fg_ref[...])


def _post(x, o_gla, o_nsa, sma, smb, gate1, scale2, shift2, gate2, n2, fg, wo, wu, wd, *, tm):
    m = x.shape[0]
    nb, r, _ = gate1.shape
    rows_per_mod = m // nb
    mod_spec = pl.BlockSpec((1, r, D_MODEL), lambda i: (i * tm // rows_per_mod, 0, 0))
    row = pl.BlockSpec((tm, D_MODEL), lambda i: (i, 0))
    return pl.pallas_call(
        _post_kernel,
        out_shape=jax.ShapeDtypeStruct((m, D_MODEL), F32),
        grid=(m // tm,),
        in_specs=[row] * 5 + [mod_spec] * 4 + [_const_spec((1, D_MODEL))] * 2
        + [_const_spec(wo.shape), _const_spec(wu.shape), _const_spec(wd.shape)],
        out_specs=row,
        compiler_params=_cparams(("parallel",)),
        name="post",
    )(x, o_gla, o_nsa, sma, smb, gate1, scale2, shift2, gate2, n2, fg, wo, wu, wd)


def _block_diag_w1(w1):
    w = w1.reshape(CMP_RATIO, CMP_STRIDE, DH, CMP_HID)
    eye = jnp.eye(NSA_G, dtype=w1.dtype)
    bd = jnp.einsum('redh,pg->epdgrh', w, eye)
    return bd.reshape(CMP_STRIDE * NSA_G * DH, NSA_G * CMP_RATIO * CMP_HID).astype(BF16)


def _tap_w1(w1):
    w = w1.reshape(CMP_RATIO, CMP_STRIDE // CMP_TAPS, CMP_TAPS, DH, CMP_HID)
    return jnp.transpose(w, (1, 2, 3, 0, 4)).reshape(CMP_STRIDE // CMP_TAPS, CMP_TAPS * DH,
                                                     CMP_RATIO * CMP_HID).astype(BF16)


def _cmp_to_sel_np(n_cmp, n_sel):
    cs = np.arange(n_cmp)[:, None] * CMP_STRIDE
    ss = np.arange(n_sel)[None, :] * SEL
    ov = np.minimum(cs + CMP_BLOCK, ss + SEL) - np.maximum(cs, ss)
    return np.clip(ov, 0, None).astype(np.float32) / CMP_BLOCK


def _bias_by_dist(rel_bias, n):
    return jnp.transpose(rel_bias[_bucket_np(np.arange(n))])


def _toeplitz(v, n):
    lead = v.shape[:-1]
    a = jnp.broadcast_to(v[..., None, :], lead + (n, 2 * n)).reshape(lead + (2 * n * n,))
    a = a[..., :n * (2 * n - 1)].reshape(lead + (n, 2 * n - 1))
    return a[..., n - 1:]


def _mod_split(mod, rows):
    parts = jnp.split(mod, 6, axis=-1)
    return [p.reshape(-1, rows, D_MODEL) for p in parts]


def kernel(x_prompt, x_sample, c_prompt, c_sample, state_gla, cache_k_cmp, cache_v_cmp, cache_k_slc, cache_v_slc, cache_k_win, cache_v_win, page_table, ada_w, ada_b, norm1_g, norm2_g, w_in, gla_a2, gla_a_b, gla_norm_g, cmp_pe_k, cmp_w1_k, cmp_w2_k, cmp_pe_v, cmp_w1_v, cmp_w2_v, w_o, w_up, w_down, rel_bias, final_g):
    depth = ada_w.shape[0]
    assert depth == 1, "single-layer trunk"
    bp, tp, _ = x_prompt.shape
    bs, ts, _ = x_sample.shape
    n_pages = page_table.shape[1]
    past_len = n_pages * PAGE
    n_phys = cache_k_cmp.shape[1]
    mp, ms = bp * tp, bs * ts
    tm = 256
    tm_s = min(tm, ms)

    w = w_in[0]
    w_perm = jnp.concatenate([w[:, :3072], w[:, 3088:5648], w[:, 5696:7744], w[:, 5648:5696], w[:, 3072:3088],
                              jnp.zeros((D_MODEL, LANES - 64), w.dtype)], axis=1).astype(BF16)
    a2p = jnp.zeros((LANES, QK_W), F32).at[Z_GA:Z_GA + GLA_RANK].set(gla_a2[0]).astype(BF16)
    ab = gla_a_b[0].reshape(1, QK_W)
    g1 = norm1_g[0].reshape(1, D_MODEL)
    n2 = norm2_g[0].reshape(1, D_MODEL)
    fg = final_g.reshape(1, D_MODEL)
    gn = gla_norm_g[0].reshape(1, GLA_DV)
    wo, wu, wd = w_o[0].astype(BF16), w_up[0].astype(BF16), w_down[0].astype(BF16)
    cmp_k = (_block_diag_w1(cmp_w1_k[0]), jnp.broadcast_to(cmp_pe_k[0].reshape(1, -1), (8, CMP_BLOCK * DH)),
             cmp_w1_k[0].reshape(CMP_BLOCK * DH, CMP_HID), cmp_w2_k[0])
    cmp_v = (_block_diag_w1(cmp_w1_v[0]), jnp.broadcast_to(cmp_pe_v[0].reshape(1, -1), (8, CMP_BLOCK * DH)),
             cmp_w1_v[0].reshape(CMP_BLOCK * DH, CMP_HID), cmp_w2_v[0])

    mod = _ada(jnp.concatenate([c_prompt, c_sample], axis=0), ada_w[0], ada_b[0])
    sh1_p, sc1_p, gt1_p, sh2_p, sc2_p, gt2_p = _mod_split(mod[:bp], 1)
    sh1_s, sc1_s, gt1_s, sh2_s, sc2_s, gt2_s = _mod_split(jnp.repeat(mod[bp:], ts, axis=0), tm_s)

    xp = x_prompt.reshape(mp, D_MODEL)
    (gq, gk, gv, sgr, la, kc, vc, ks, vs, kw, vw, _, sma, smb, qt, ksh, vst, kwh, vwt, gtt) = _inproj(
        xp, sc1_p, sh1_p, g1, w_perm, a2p, ab, head_major=True, tm=tm)

    r3 = lambda a: a.reshape(bp, tp, a.shape[-1])
    o_gla_p, s_gla_p = _gla(r3(gq), r3(gk), r3(gv), r3(la), r3(sgr), gn, None, tb=256)

    n_seg = tp // CMP_STRIDE
    assert n_seg == LANES, "prompt attention keeps one compressed block per lane"
    n_cmp_p = n_seg - CMP_RATIO + 1
    seg = lambda a: a.reshape(mp // CMP_STRIDE, CMP_STRIDE * KV_W)
    tm_seg = min(tm, mp // CMP_STRIDE)
    kcc = _cmpfin(_mm(seg(kc), cmp_k[0], tm=tm_seg).reshape(bp, n_seg, -1), None, *cmp_k[1:], layout='rows')
    vcc = _cmpfin(_mm(seg(vc), cmp_v[0], tm=tm_seg).reshape(bp, n_seg, -1), None, *cmp_v[1:3],
                  jnp.transpose(cmp_v[3]), layout='lanes')

    n_sel_p = tp // SEL
    f = _bias_by_dist(rel_bias, 2 * TQ)
    v_same = jnp.concatenate([jnp.repeat(f[:, :1], TQ - 1, axis=1), f[:, :TQ + 1]], axis=1)
    v_prev = jnp.concatenate([f[:, 1:], f[:, -1:]], axis=1)
    tiles = _toeplitz(jnp.stack([v_same, v_prev]), TQ).reshape(2, NSA_G, NSA_HG, TQ, TQ)
    tiles = jnp.transpose(tiles, (0, 1, 3, 2, 4)).reshape(2, NSA_G, TQ, NSA_HG * TQ)
    far_cols = jnp.broadcast_to(f[:, -1].reshape(NSA_G, NSA_HG, 1), (NSA_G, NSA_HG, TQ)).reshape(NSA_G, 1, -1)
    bias_t = jnp.concatenate([tiles, jnp.broadcast_to(far_cols[None], (1, NSA_G, TQ, NSA_HG * TQ))], axis=0)
    d_wc = (np.arange(TQ)[None, :] - CMP_STRIDE * (np.arange(CMP_WIN_ROWS)[:, None] - CMP_WIN_PAD)
            - (CMP_BLOCK - 1))
    bias_wc = f[:, np.clip(d_wc, 0, 2 * TQ - 1)].reshape(NSA_G, NSA_HG, CMP_WIN_ROWS, TQ)
    bias_wc = jnp.transpose(bias_wc, (0, 2, 1, 3)).reshape(NSA_G, CMP_WIN_ROWS, NSA_HG * TQ)
    c2s_t = np.zeros((n_sel_p, n_seg), np.float32)
    c2s_t[:, :n_cmp_p] = _cmp_to_sel_np(n_cmp_p, n_sel_p).T
    k4 = lambda a: a.reshape(NSA_G, bp, tp, DH)
    o_nsa_p = _nsa_prompt(qt, kcc, vcc, k4(ksh), vst, k4(kwh), vwt, gtt, bias_t, bias_wc, far_cols,
                          jnp.asarray(c2s_t), bsz=bp, n_cmp=n_cmp_p)

    y_p = _post(xp, o_gla_p.reshape(mp, V_W), o_nsa_p.reshape(mp, D_MODEL), sma, smb, gt1_p, sc2_p, sh2_p, gt2_p,
                n2, fg, wo, wu, wd, tm=tm)

    kv5 = lambda a, b, t: a.reshape(1, b, t, NSA_G, DH)
    wp = min(WIN, tp)
    p_out = (s_gla_p[None], kv5(kc, bp, tp), kv5(vc, bp, tp), kv5(ks, bp, tp), kv5(vs, bp, tp),
             kv5(kw, bp, tp)[:, :, tp - wp:], kv5(vw, bp, tp)[:, :, tp - wp:])

    xs = x_sample.reshape(ms, D_MODEL)
    (gq, gk, gv, sgr, la, kc, vc, ks, vs, kw, vw, gates, sma, smb, nq) = _inproj(
        xs, sc1_s, sh1_s, g1, w_perm, a2p, ab, head_major=False, tm=tm_s)

    t_pad = 8
    pad_t = lambda a: jnp.pad(a.reshape(bs, ts, a.shape[-1]), ((0, 0), (0, t_pad - ts), (0, 0)))
    o_gla_s, s_gla_s = _gla(pad_t(gq), pad_t(gk), pad_t(gv), pad_t(la), pad_t(sgr), gn, state_gla[0], tb=t_pad)
    o_gla_s = o_gla_s[:, :ts].reshape(ms, V_W)

    pool_t = lambda a: jnp.transpose(a[0], (0, 2, 3, 1))
    new_seg = lambda a: jnp.pad(a.reshape(bs, ts, KV_W), ((0, 0), (0, CMP_STRIDE - ts), (0, 0))).reshape(
        bs, CMP_STRIDE * KV_W)
    n_cmp_s = past_len // CMP_STRIDE
    pps_cmp = min(16, n_pages)
    kcc = _cmpfin(_cmp_paged(pool_t(cache_k_cmp), page_table, _tap_w1(cmp_w1_k[0]), pages_per_step=pps_cmp),
                  _mm(new_seg(kc), cmp_k[0], tm=bs).reshape(bs, 1, -1), *cmp_k[1:], layout='flat')
    vcc = _cmpfin(_cmp_paged(pool_t(cache_v_cmp), page_table, _tap_w1(cmp_w1_v[0]), pages_per_step=pps_cmp),
                  _mm(new_seg(vc), cmp_v[0], tm=bs).reshape(bs, 1, -1), *cmp_v[1:], layout='flat')

    q5 = nq.reshape(bs, ts, NSA_G, NSA_HG, DH)
    q_bd = jnp.einsum('btghd,pg->bpdght', q5, jnp.eye(NSA_G, dtype=F32)).reshape(bs, KV_W, NSA_HEADS * ts)
    q_bd = jnp.pad(q_bd, ((0, 0), (0, 0), (0, LANES - NSA_HEADS * ts))).astype(BF16)

    ncol = NSA_HEADS * ts
    col_head = np.arange(ncol) // ts
    col_tok = np.arange(ncol) % ts

    def bias_cols(key_pos):
        dist = past_len + col_tok[None, :] - key_pos[:, None]
        near = np.nonzero(dist.min(axis=1) < REL_MAX)[0]
        lo = int(near.min()) if near.size else len(key_pos)
        far_part = jnp.broadcast_to(f[col_head, -1][None, :], (lo, ncol))
        near_part = f[col_head[None, :], np.clip(dist[lo:], 0, 2 * TQ - 1)]
        return jnp.pad(jnp.concatenate([far_part, near_part], axis=0), ((0, 0), (0, LANES - ncol)))

    pps = min(16, n_pages)
    bias_c_s = bias_cols(np.arange(n_cmp_s) * CMP_STRIDE + CMP_BLOCK - 1)
    bias_s_s = bias_cols(np.arange(past_len)).reshape(n_pages // pps, pps * PAGE, LANES)
    bias_n_s = bias_cols(past_len + np.arange(PAGE))
    bias_w_s = bias_cols(past_len - WIN + np.arange(WIN))
    n_sel_s = past_len // SEL + 1
    n_sel_pad = -(-n_sel_s // 8) * 8
    c2s_t = np.zeros((n_sel_pad, n_cmp_s), np.float32)
    c2s_t[:n_sel_s] = _cmp_to_sel_np(n_cmp_s, n_sel_s).T
    gt_col = np.arange(LANES) // (NSA_HG * ts) * ts + np.arange(LANES) % ts
    hsum = ((gt_col[:, None] == np.arange(LANES)[None, :]) & (np.arange(LANES)[:, None] < ncol)).astype(np.float32)
    expand_s = (np.arange(pps * PAGE)[:, None] // SEL == np.arange(pps * PAGE // SEL)[None, :]).astype(np.float32)
    gates_s = jnp.transpose(gates[:, :3 * NSA_HEADS].reshape(bs, ts, 3, NSA_HEADS), (0, 3, 1, 2)).reshape(bs, ncol, 3)
    new_rows = lambda a: jnp.pad(a.reshape(bs, ts, KV_W), ((0, 0), (0, PAGE - ts), (0, 0)))
    wb = cache_k_win.shape[2]
    win_t = lambda a: pool_t(a).reshape(bs, KV_W, wb)
    o_s = _nsa_sample(page_table, pool_t(cache_k_slc), pool_t(cache_v_slc), q_bd, kcc, vcc, new_rows(ks),
                      new_rows(vs), win_t(cache_k_win), win_t(cache_v_win), new_rows(kw),
                      new_rows(vw), gates_s, bias_c_s, bias_s_s, bias_n_s, bias_w_s, jnp.asarray(c2s_t),
                      jnp.asarray(hsum), jnp.asarray(hsum.T, dtype=BF16), jnp.asarray(expand_s, dtype=BF16),
                      pps=pps, t_new=ts)
    o_nsa_s = jnp.transpose(o_s.reshape(bs, NSA_HEADS, ts, DH), (0, 2, 1, 3)).reshape(ms, D_MODEL)

    y_s = _post(xs, o_gla_s, o_nsa_s, sma, smb, gt1_s, sc2_s, sh2_s, gt2_s, n2, fg, wo, wu, wd, tm=tm_s)

    kw_new = jnp.concatenate([cache_k_win[0], kv5(kw, bs, ts)[0]], axis=1)[:, ts:][None]
    vw_new = jnp.concatenate([cache_v_win[0], kv5(vw, bs, ts)[0]], axis=1)[:, ts:][None]
    s_out = (s_gla_s[None], kv5(kc, bs, ts), kv5(vc, bs, ts), kv5(ks, bs, ts), kv5(vs, bs, ts), kw_new, vw_new)

    return (y_p.reshape(bp, tp, D_MODEL), y_s.reshape(bs, ts, D_MODEL)) + p_out + s_out
```

```python
import functools
import math

import numpy as np
import jax
import jax.numpy as jnp
from jax import lax
from jax.experimental import pallas as pl
from jax.experimental.pallas import tpu as pltpu

F32 = jnp.float32
BF16 = jnp.bfloat16
HIGHEST = lax.Precision.HIGHEST

D_MODEL = 1024
GLA_HEADS = 4
GLA_DK = 128
GLA_DV = 256
GLA_RANK = 16
GLA_TAU = 16.0
GLA_CHUNK = 64
GLA_SUB = 16
NSA_HEADS = 16
DH = 64
NSA_G = 4
NSA_HG = 4
CMP_BLOCK = 32
CMP_STRIDE = 16
CMP_RATIO = 2
CMP_HID = 128
SEL = 64
SEL_SHIFT = 6
TOPK = 16
WIN = 512
REL_BUCKETS = 32
REL_MAX = 128
D_FF = 4096
EPS = 1e-6
NEG = -1e30
FORCE = 1e4
PAGE = 128

QK_W = GLA_HEADS * GLA_DK
V_W = GLA_HEADS * GLA_DV
KV_W = NSA_G * DH
LANES = 128
TQ = 128
VMEM_LIMIT = 56 * 1024 * 1024

C_GQ, C_GK, C_GV, C_GR, C_NQ = 0, 512, 1024, 2048, 3072
C_KV = 4096
C_MA, C_MB, C_Z = 5632, 6656, 7680
W_COLS = 7808
Z_GA = 48


def _cparams(sem):
    return pltpu.CompilerParams(dimension_semantics=sem, vmem_limit_bytes=VMEM_LIMIT)


def _const_spec(shape):
    return pl.BlockSpec(shape, lambda *a: (0,) * len(shape), pipeline_mode=pl.Buffered(1))


def _sigmoid(x):
    return 1.0 / (1.0 + jnp.exp(-x))


def _rms(x, g):
    return x * lax.rsqrt(jnp.mean(x * x, axis=-1, keepdims=True) + EPS) * g


def _dot(a, b, **kw):
    return jnp.dot(a, b, preferred_element_type=F32, **kw)


def _dot_nt(a, b):
    return lax.dot_general(a, b, (((1,), (1,)), ((), ())), preferred_element_type=F32)


def _dot_tn(a, b):
    return lax.dot_general(a, b, (((0,), (0,)), ((), ())), preferred_element_type=F32)


def _bucket_np(dist):
    n = np.maximum(dist, 0)
    nf = np.maximum(n, 1).astype(np.float64)
    large = 16 + (np.log(nf / 16.0) / math.log(REL_MAX / 16.0) * 16.0).astype(np.int64)
    large = np.minimum(large, REL_BUCKETS - 1)
    return np.where(n < 16, n, large).astype(np.int32)


def _ada_kernel(c_ref, w_ref, b_ref, o_ref):
    c = c_ref[...]
    o_ref[...] = _dot(c * _sigmoid(c), w_ref[...], precision=HIGHEST) + b_ref[...]


def _ada(c, w, b):
    r, d = c.shape
    n = w.shape[1]
    return pl.pallas_call(
        _ada_kernel,
        out_shape=jax.ShapeDtypeStruct((r, n), F32),
        grid=(n // d,),
        in_specs=[pl.BlockSpec((r, d), lambda j: (0, 0)),
                  pl.BlockSpec((d, d), lambda j: (0, j)),
                  pl.BlockSpec((1, d), lambda j: (0, j))],
        out_specs=pl.BlockSpec((r, d), lambda j: (0, j)),
        compiler_params=_cparams(("arbitrary",)),
        name="ada",
    )(c, w, b.reshape(1, n))


def _inproj_kernel(x_ref, sc_ref, sh_ref, g_ref, w_ref, a2_ref, ab_ref, *outs, head_major):
    (gq_o, gk_o, gv_o, sgr_o, la_o, kc_o, vc_o, ks_o, vs_o, kw_o, vw_o, gt_o, sma_o, smb_o, *rest) = outs
    h = _rms(x_ref[...], g_ref[...]) * (1.0 + sc_ref[0]) + sh_ref[0]
    hb = h.astype(BF16)

    def proj(lo, hi):
        return _dot(hb, w_ref[:, lo:hi])

    gq_o[...] = proj(C_GQ, C_GK) * (GLA_DK ** -0.5)
    gk_o[...] = proj(C_GK, C_GV)
    gv_o[...] = proj(C_GV, C_GR)
    gr = proj(C_GR, C_NQ)
    sgr_o[...] = gr * _sigmoid(gr)
    nq = proj(C_NQ, C_KV) * (DH ** -0.5)
    kv = proj(C_KV, C_MA)
    for n, o in enumerate((kc_o, vc_o, ks_o, vs_o, kw_o, vw_o)):
        o[...] = kv[:, n * KV_W:(n + 1) * KV_W]
    sma_o[...] = _sigmoid(proj(C_MA, C_MB))
    smb_o[...] = _sigmoid(proj(C_MB, C_Z))
    z = proj(C_Z, W_COLS)
    gt_o[...] = _sigmoid(z)
    pre = _dot(z.astype(BF16), a2_ref[...]) + ab_ref[...]
    la_o[...] = (jnp.minimum(pre, 0.0) - jnp.log(1.0 + jnp.exp(-jnp.abs(pre)))) * (1.0 / GLA_TAU)
    if head_major:
        qt_o, ksh_o, vst_o, kwh_o, vwt_o, gtt_o = rest

        def put_transposed(o, x):
            for c in range(x.shape[1] // LANES):
                t = jnp.transpose(x[:, c * LANES:(c + 1) * LANES]).astype(BF16)
                o[2 * c] = t[:DH]
                o[2 * c + 1] = t[DH:]

        put_transposed(qt_o, nq)
        put_transposed(vst_o, kv[:, 3 * KV_W:4 * KV_W])
        put_transposed(vwt_o, kv[:, 5 * KV_W:6 * KV_W])
        for n, o in ((2, ksh_o), (4, kwh_o)):
            for gg in range(NSA_G):
                lo = n * KV_W + gg * DH
                o[gg] = kv[:, lo:lo + DH].astype(BF16)
        gtt_o[...] = jnp.transpose(_sigmoid(z))
    else:
        (nq_o,) = rest
        nq_o[...] = nq


def _inproj(x, scale, shift, g1, w, a2p, ab, *, head_major, tm):
    m = x.shape[0]
    nb, r, _ = scale.shape
    rows_per_mod = m // nb
    mod_spec = pl.BlockSpec((1, r, D_MODEL), lambda i: (i * tm // rows_per_mod, 0, 0))
    row = lambda wdt: pl.BlockSpec((tm, wdt), lambda i: (i, 0))
    widths = [QK_W, QK_W, V_W, V_W, QK_W] + [KV_W] * 6 + [LANES, D_MODEL, D_MODEL]
    out_shape = [jax.ShapeDtypeStruct((m, wd), F32) for wd in widths]
    out_specs = [row(wd) for wd in widths]
    if head_major:
        def rows_major(n):
            out_shape.append(jax.ShapeDtypeStruct((n, m, DH), BF16))
            out_specs.append(pl.BlockSpec((n, tm, DH), lambda i: (0, i, 0)))

        def lanes_major(n):
            out_shape.append(jax.ShapeDtypeStruct((n, DH, m), BF16))
            out_specs.append(pl.BlockSpec((n, DH, tm), lambda i: (0, 0, i)))

        lanes_major(NSA_HEADS)
        rows_major(NSA_G)
        lanes_major(NSA_G)
        rows_major(NSA_G)
        lanes_major(NSA_G)
        out_shape.append(jax.ShapeDtypeStruct((LANES, m), F32))
        out_specs.append(pl.BlockSpec((LANES, tm), lambda i: (0, i)))
    else:
        out_shape.append(jax.ShapeDtypeStruct((m, D_MODEL), F32))
        out_specs.append(row(D_MODEL))
    return pl.pallas_call(
        functools.partial(_inproj_kernel, head_major=head_major),
        out_shape=out_shape,
        grid=(m // tm,),
        in_specs=[row(D_MODEL), mod_spec, mod_spec, _const_spec((1, D_MODEL)), _const_spec((D_MODEL, W_COLS)),
                  _const_spec((LANES, QK_W)), _const_spec((1, QK_W))],
        out_specs=out_specs,
        compiler_params=_cparams(("parallel",)),
        name="inproj",
    )(x, scale, shift, g1, w, a2p, ab)


def _gla_kernel(*refs, chunk, sub, n_chunk, has_init):
    if has_init:
        q_ref, k_ref, v_ref, la_ref, sgr_ref, gn_ref, s0_ref, o_ref, so_ref, s_scr = refs
    else:
        q_ref, k_ref, v_ref, la_ref, sgr_ref, gn_ref, o_ref, so_ref, s_scr = refs
    t = pl.program_id(1)

    @pl.when(t == 0)
    def _():
        s_scr[...] = s0_ref[0] if has_init else jnp.zeros_like(s_scr)

    tril = (lax.broadcasted_iota(jnp.int32, (chunk, chunk), 0)
            >= lax.broadcasted_iota(jnp.int32, (chunk, chunk), 1)).astype(F32)
    sub_row = lax.broadcasted_iota(jnp.int32, (sub, 1), 0)

    def one_chunk(c, carry):
        r0 = pl.multiple_of(c * chunk, chunk)
        rows = pl.ds(r0, chunk)
        b_all = _dot(tril, la_ref[0, rows, :], precision=HIGHEST)
        for h in range(GLA_HEADS):
            one_head(h, rows, b_all[:, h * GLA_DK:(h + 1) * GLA_DK])
        return carry

    def one_head(h, rows, b):
        ks, vs = slice(h * GLA_DK, (h + 1) * GLA_DK), slice(h * GLA_DV, (h + 1) * GLA_DV)
        q = q_ref[0, rows, ks]
        k = k_ref[0, rows, ks]
        v = v_ref[0, rows, vs]
        s_old = s_scr[h]
        vb = v.astype(BF16)
        o_inter = _dot((q * jnp.exp(b)).astype(BF16), s_old.astype(BF16))
        blocks = []
        for blk in range(chunk // sub):
            sl = slice(blk * sub, (blk + 1) * sub)
            q_i, k_i, v_i, b_i = q[sl], k[sl], v[sl], b[sl]
            acc = jnp.zeros((sub, GLA_DV), F32)
            for j in range(sub):
                w = q_i * k_i[j:j + 1] * jnp.exp(jnp.minimum(b_i - b_i[j:j + 1], 0.0))
                a = jnp.sum(w, axis=-1, keepdims=True)
                acc = acc + jnp.where(sub_row >= j, a, 0.0) * v_i[j:j + 1]
            if blk > 0:
                n = blk * sub
                ref_row = b_i[0:1]
                qp = (q_i * jnp.exp(b_i - ref_row)).astype(BF16)
                kp = (k[:n] * jnp.exp(ref_row - b[:n])).astype(BF16)
                acc = acc + _dot(_dot_nt(qp, kp).astype(BF16), vb[:n])
            blocks.append(acc)
        o = o_inter + (jnp.concatenate(blocks, axis=0) if len(blocks) > 1 else blocks[0])
        b_last = b[chunk - 1:chunk]
        kp = (k * jnp.exp(b_last - b)).astype(BF16)
        decay_col = jnp.transpose(jnp.broadcast_to(jnp.exp(b_last), (GLA_DK, GLA_DK)))[:, 0:1]
        s_scr[h] = decay_col * s_old + _dot_tn(kp, vb)
        o_ref[0, rows, vs] = _rms(o, gn_ref[...]) * sgr_ref[0, rows, vs]

    lax.fori_loop(0, n_chunk, one_chunk, 0)

    @pl.when(t == pl.num_programs(1) - 1)
    def _():
        so_ref[0] = s_scr[...]


def _gla(q, k, v, la, sgr, gn, s0, *, tb):
    bsz, t, _ = q.shape
    chunk = min(GLA_CHUNK, tb)
    sub = min(GLA_SUB, chunk)
    blk = lambda wd: pl.BlockSpec((1, tb, wd), lambda b, i: (b, i, 0))
    st_spec = pl.BlockSpec((1, GLA_HEADS, GLA_DK, GLA_DV), lambda b, i: (b, 0, 0, 0))
    in_specs = [blk(QK_W), blk(QK_W), blk(V_W), blk(QK_W), blk(V_W),
                pl.BlockSpec((1, GLA_DV), lambda b, i: (0, 0))]
    args = [q, k, v, la, sgr, gn]
    if s0 is not None:
        in_specs.append(st_spec)
        args.append(s0)
    return pl.pallas_call(
        functools.partial(_gla_kernel, chunk=chunk, sub=sub, n_chunk=tb // chunk, has_init=s0 is not None),
        out_shape=[jax.ShapeDtypeStruct((bsz, t, V_W), F32),
                   jax.ShapeDtypeStruct((bsz, GLA_HEADS, GLA_DK, GLA_DV), F32)],
        grid=(bsz, t // tb),
        in_specs=in_specs,
        out_specs=[blk(V_W), st_spec],
        scratch_shapes=[pltpu.VMEM((GLA_HEADS, GLA_DK, GLA_DV), F32)],
        compiler_params=_cparams(("parallel", "arbitrary")),
        name="gla",
    )(*args)


def _mm_kernel(x_ref, w_ref, o_ref):
    o_ref[...] = _dot(x_ref[...].astype(BF16), w_ref[...])


def _mm(x, w, *, tm):
    m, kdim = x.shape
    n = w.shape[1]
    return pl.pallas_call(
        _mm_kernel,
        out_shape=jax.ShapeDtypeStruct((m, n), F32),
        grid=(m // tm,),
        in_specs=[pl.BlockSpec((tm, kdim), lambda i: (i, 0)), _const_spec((kdim, n))],
        out_specs=pl.BlockSpec((tm, n), lambda i: (i, 0)),
        compiler_params=_cparams(("parallel",)),
        name="cmp_mm",
    )(x, w)


CMP_TAPS = 4


def _cmp_paged_kernel(*refs, pps):
    pages = refs[1:1 + pps]
    w_ref, o_ref, tok_s = refs[1 + pps:]
    heads_per_slab = LANES // DH
    n_slab = NSA_G // heads_per_slab
    for p in range(pps):
        t = jnp.transpose(pages[p][0].reshape(KV_W, PAGE))
        for sl in range(n_slab):
            tok_s[sl, p * PAGE:(p + 1) * PAGE, :] = t[:, sl * LANES:(sl + 1) * LANES]
    n_seg = pps * PAGE // CMP_STRIDE
    width = CMP_RATIO * CMP_HID
    first_half = lax.broadcasted_iota(jnp.int32, (1, LANES), 1) < DH
    for sl in range(n_slab):
        x = [tok_s[sl, pl.ds(e, n_seg, stride=CMP_STRIDE), :] for e in range(CMP_STRIDE)]
        r = [pltpu.roll(v, DH, 1) for v in x]
        for odd in range(heads_per_slab):
            g = sl * heads_per_slab + odd
            acc = None
            for j in range(CMP_STRIDE // CMP_TAPS):
                pieces = []
                for e in range(j * CMP_TAPS, (j + 1) * CMP_TAPS, 2):
                    lo, hi = (r[e], x[e + 1]) if odd else (x[e], r[e + 1])
                    pieces.append(jnp.where(first_half, lo, hi))
                term = _dot(jnp.concatenate(pieces, axis=1).astype(BF16), w_ref[j])
                acc = term if acc is None else acc + term
            o_ref[0, :, g * width:(g + 1) * width] = acc


def _cmp_paged(pool_t, page_table, w_taps, *, pages_per_step):
    bsz, n_pages = page_table.shape
    pps = pages_per_step
    n_seg = pps * PAGE // CMP_STRIDE
    width = NSA_G * CMP_RATIO * CMP_HID

    def page_spec(j):
        return pl.BlockSpec((1, NSA_G, DH, PAGE), lambda b, s, pt: (pt[b, s * pps + j], 0, 0, 0))

    grid_spec = pltpu.PrefetchScalarGridSpec(
        num_scalar_prefetch=1,
        grid=(bsz, n_pages // pps),
        in_specs=[page_spec(j) for j in range(pps)]
        + [pl.BlockSpec(w_taps.shape, lambda b, s, pt: (0, 0, 0), pipeline_mode=pl.Buffered(1))],
        out_specs=pl.BlockSpec((1, n_seg, width), lambda b, s, pt: (b, s, 0)),
        scratch_shapes=[pltpu.VMEM((NSA_G * DH // LANES, pps * PAGE, LANES), F32)],
    )
    return pl.pallas_call(
        functools.partial(_cmp_paged_kernel, pps=pps),
        out_shape=jax.ShapeDtypeStruct((bsz, n_pages * PAGE // CMP_STRIDE, width), F32),
        grid_spec=grid_spec,
        compiler_params=_cparams(("parallel", "arbitrary")),
        name="cmp_paged",
    )(page_table, *([pool_t] * pps), w_taps)


def _cmpfin_kernel(*refs, n, has_new, layout):
    if has_new:
        p_ref, pn_ref, pe_ref, w1_ref, w2_ref, o_ref = refs
    else:
        p_ref, pe_ref, w1_ref, w2_ref, o_ref = refs
    pe_term = _dot(pe_ref[...], w1_ref[...], precision=HIGHEST)[0:1]
    w2 = w2_ref[...].astype(BF16)
    rowi = lax.broadcasted_iota(jnp.int32, (n, 1), 0)
    for g in range(NSA_G):
        base = g * CMP_RATIO * CMP_HID
        first = p_ref[0, :, base:base + CMP_HID]
        second = pltpu.roll(p_ref[0, :, base + CMP_HID:base + 2 * CMP_HID], n - 1, 0)
        if has_new:
            second = jnp.where(rowi == n - 1, pn_ref[0, :, base + CMP_HID:base + 2 * CMP_HID], second)
        acc = first + second + pe_term
        hid = (acc * _sigmoid(acc)).astype(BF16)
        if layout == 'flat':
            o_ref[0, :, g * DH:(g + 1) * DH] = _dot(hid, w2)
        elif layout == 'rows':
            o_ref[0, g] = _dot(hid, w2)
        else:
            o_ref[0, g] = _dot_nt(w2, hid)


def _cmpfin(parts, parts_new, pe, w1, w2, *, layout):
    bsz, n, width = parts.shape
    in_specs = [pl.BlockSpec((1, n, width), lambda b: (b, 0, 0))]
    args = [parts]
    if parts_new is not None:
        in_specs.append(pl.BlockSpec((1, 1, width), lambda b: (b, 0, 0)))
        args.append(parts_new)
    in_specs += [_const_spec(pe.shape), _const_spec(w1.shape), _const_spec(w2.shape)]
    if layout == 'flat':
        out_shape = jax.ShapeDtypeStruct((bsz, n, KV_W), F32)
        out_spec = pl.BlockSpec((1, n, KV_W), lambda b: (b, 0, 0))
    else:
        inner = (n, DH) if layout == 'rows' else (DH, n)
        out_shape = jax.ShapeDtypeStruct((bsz, NSA_G) + inner, F32)
        out_spec = pl.BlockSpec((1, NSA_G) + inner, lambda b: (b, 0, 0, 0))
    return pl.pallas_call(
        functools.partial(_cmpfin_kernel, n=n, has_new=parts_new is not None, layout=layout),
        out_shape=out_shape,
        grid=(bsz,),
        in_specs=in_specs,
        out_specs=out_spec,
        compiler_params=_cparams(("parallel",)),
        name="cmp_finish",
    )(*args, pe, w1, w2)


def _masked_softmax(s, mask, axis):
    s = jnp.where(mask, s, NEG)
    m = jnp.max(s, axis=axis, keepdims=True)
    e = jnp.where(mask, jnp.exp(s - m), 0.0)
    return e / jnp.maximum(jnp.sum(e, axis=axis, keepdims=True), 1e-30)


CMP_WIN_ROWS = 24
CMP_WIN_PAD = 16


def _nsa_prompt_kernel(qt_ref, kc_ref, vct_ref, ks_ref, vst_ref, kw_ref, vwt_ref, gtt_ref, bt_ref, wc_ref, far_ref,
                       c2s_ref, o_ref, sel_s, bc_s, m_s, l_s, acc_s, *, n_cmp, n_sel, n_tiles):
    g = pl.program_id(1)
    i = pl.program_id(2)
    q4 = jnp.concatenate([qt_ref[h] for h in range(NSA_HG)], axis=1)
    tok = i * TQ + lax.broadcasted_iota(jnp.int32, (1, TQ), 1)
    tok4 = jnp.concatenate([tok] * NSA_HG, axis=1)
    row = lax.broadcasted_iota(jnp.int32, (TQ, 1), 0)

    bc_s[...] = jnp.broadcast_to(far_ref[0], bc_s.shape)
    bc_s[pl.ds(pl.multiple_of(i * (TQ // CMP_STRIDE), TQ // CMP_STRIDE), CMP_WIN_ROWS), :] = wc_ref[0]
    s_c = _dot(kc_ref[0, 0].astype(BF16), q4) + bc_s[CMP_WIN_PAD:CMP_WIN_PAD + TQ, :]
    mask_c = (tok4 - (row * CMP_STRIDE + CMP_BLOCK - 1) >= 0) & (row < n_cmp)
    p_c = _masked_softmax(s_c, mask_c, 0)
    o_cmp = _dot(vct_ref[0, 0].astype(BF16), p_c.astype(BF16))

    p_sum = p_c[:, 0:TQ] + p_c[:, TQ:2 * TQ] + p_c[:, 2 * TQ:3 * TQ] + p_c[:, 3 * TQ:4 * TQ]
    imp = _dot(c2s_ref[...], p_sum, precision=HIGHEST)
    blk = lax.broadcasted_iota(jnp.int32, (n_sel, 1), 0)
    cur = jnp.right_shift(tok, SEL_SHIFT)
    forced = (blk == 0) | (blk == cur) | (blk == cur - 1)
    valid = blk * SEL <= tok
    score = jnp.where(forced, FORCE, jnp.where(valid, imp, -FORCE))
    cnt = jnp.zeros((n_sel, TQ), jnp.int32)
    for b in range(n_sel):
        s_b = score[b:b + 1]
        ahead = (s_b > score) | ((s_b == score) & (blk > b))
        cnt = cnt + ahead.astype(jnp.int32)
    sel_s[...] = jnp.where(cnt < min(TOPK, n_sel), 1.0, 0.0)

    def attend(k_ref, vt_ref, lo, hi, use_sel):
        m_s[...] = jnp.full_like(m_s, NEG)
        l_s[...] = jnp.zeros_like(l_s)
        acc_s[...] = jnp.zeros_like(acc_s)

        def scores(kt):
            kt_ld = jnp.minimum(kt, n_tiles - 1)
            k0 = pl.multiple_of(kt_ld * TQ, TQ)
            s = _dot(k_ref[0, 0, pl.ds(k0, TQ), :], q4) + bt_ref[jnp.clip(i - kt, 0, 2), 0]
            dist = tok - (kt * TQ + row)
            if use_sel:
                blocks_per_tile = TQ // SEL
                sel_add = None
                for j in reversed(range(blocks_per_tile)):
                    add_j = jnp.where(sel_s[pl.ds(kt_ld * blocks_per_tile + j, 1), :] > 0.5, 0.0, NEG)
                    sel_add = add_j if sel_add is None else jnp.where(row < (j + 1) * SEL, add_j, sel_add)
                madd = jnp.where(dist >= 0, sel_add, NEG)
            else:
                madd = jnp.where(lax.bitcast_convert_type(dist, jnp.uint32) < WIN, 0.0, NEG)
            return s + jnp.concatenate([madd] * NSA_HG, axis=1), k0

        def body(c, carry):
            pairs = [(scores(lo + 4 * c + 2 * u), scores(lo + 4 * c + 2 * u + 1)) for u in range(2)]
            m_run, l_run, acc = m_s[...], l_s[...], acc_s[...]
            for (s_a, k0_a), (s_b, k0_b) in pairs:
                m_new = jnp.maximum(m_run, jnp.maximum(jnp.max(s_a, axis=0, keepdims=True),
                                                       jnp.max(s_b, axis=0, keepdims=True)))
                p_a = jnp.exp(s_a - m_new)
                p_b = jnp.exp(s_b - m_new)
                alpha = jnp.exp(m_run - m_new)
                l_run = alpha * l_run + jnp.sum(p_a, axis=0, keepdims=True) + jnp.sum(p_b, axis=0, keepdims=True)
                acc = (alpha * acc + _dot(vt_ref[0, :, pl.ds(k0_a, TQ)], p_a.astype(BF16))
                       + _dot(vt_ref[0, :, pl.ds(k0_b, TQ)], p_b.astype(BF16)))
                m_run = m_new
            m_s[...], l_s[...], acc_s[...] = m_run, l_run, acc
            return carry

        lax.fori_loop(0, (hi - lo + 3) // 4, body, 0)
        return acc_s[...] / jnp.maximum(l_s[...], 1e-30)

    o_slc = attend(ks_ref, vst_ref, 0, i + 1, True)
    o_win = attend(kw_ref, vwt_ref, jnp.maximum(i - WIN // TQ, 0), i + 1, False)

    heads = []
    for h in range(NSA_HG):
        cs = slice(h * TQ, (h + 1) * TQ)
        o_h = jnp.zeros((DH, TQ), F32)
        for br, o_br in enumerate((o_cmp, o_slc, o_win)):
            gate = gtt_ref[pl.ds(br * NSA_HEADS + g * NSA_HG + h, 1), :]
            o_h = o_h + gate * o_br[:, cs]
        heads.append(o_h)
    o_ref[0] = jnp.transpose(jnp.concatenate(heads, axis=0))


def _nsa_prompt(qt, kc, vct, ksh, vst, kwh, vwt, gtt, bias_t, bias_wc, bias_far, c2s_t, *, bsz, n_cmp):
    t = qt.shape[2] // bsz
    nq = t // TQ
    n_sel = t // SEL
    cols = NSA_HG * TQ
    k_spec = pl.BlockSpec((1, 1, t, DH), lambda b, g, i: (g, b, 0, 0))
    vt_spec = pl.BlockSpec((1, DH, t), lambda b, g, i: (g, 0, b))
    return pl.pallas_call(
        functools.partial(_nsa_prompt_kernel, n_cmp=n_cmp, n_sel=n_sel, n_tiles=nq),
        out_shape=jax.ShapeDtypeStruct((bsz, t, D_MODEL), F32),
        grid=(bsz, NSA_G, nq),
        in_specs=[pl.BlockSpec((NSA_HG, DH, TQ), lambda b, g, i: (g, 0, b * nq + i)),
                  pl.BlockSpec((1, 1, kc.shape[2], DH), lambda b, g, i: (b, g, 0, 0)),
                  pl.BlockSpec((1, 1, DH, vct.shape[3]), lambda b, g, i: (b, g, 0, 0)),
                  k_spec, vt_spec, k_spec, vt_spec,
                  pl.BlockSpec((LANES, TQ), lambda b, g, i: (0, b * nq + i)),
                  pl.BlockSpec((3, 1, TQ, cols), lambda b, g, i: (0, g, 0, 0)),
                  pl.BlockSpec((1, CMP_WIN_ROWS, cols), lambda b, g, i: (g, 0, 0)),
                  pl.BlockSpec((1, 1, cols), lambda b, g, i: (g, 0, 0)),
                  pl.BlockSpec(c2s_t.shape, lambda b, g, i: (0, 0))],
        out_specs=pl.BlockSpec((1, TQ, NSA_HG * DH), lambda b, g, i: (b, i, g)),
        scratch_shapes=[pltpu.VMEM((n_sel, TQ), F32), pltpu.VMEM((CMP_WIN_PAD + TQ, cols), F32),
                        pltpu.VMEM((1, cols), F32), pltpu.VMEM((1, cols), F32), pltpu.VMEM((DH, cols), F32)],
        compiler_params=_cparams(("parallel", "parallel", "arbitrary")),
        name="nsa_prompt",
    )(qt, kc, vct, ksh, vst, kwh, vwt, gtt, bias_t, bias_wc, bias_far, c2s_t)


def _nsa_sample_kernel(*refs, pps, n_cmp, n_sel, n_sel_pad, past_len, t_new):
    k_pages = refs[1:1 + pps]
    v_pages = refs[1 + pps:1 + 2 * pps]
    (q_ref, kct_ref, vc_ref, kn_ref, vn_ref, kwt_ref, vwt_ref, kwn_ref, vwn_ref, gt_ref, bc_ref, bs_ref, bn_ref,
     bw_ref, c2s_ref, hsum_ref, hexp_ref, e_ref, o_ref,
     sel_s, score_s, m_s, l_s, acc_s, ocmp_s, owin_s) = refs[1 + 2 * pps:]
    ch = pl.program_id(1)
    nrow = NSA_HEADS * t_new
    q = q_ref[0]
    rtok = lax.broadcasted_iota(jnp.int32, (nrow, 1), 0) & (t_new - 1)
    qpos = past_len + rtok
    lane = lax.broadcasted_iota(jnp.int32, (1, LANES), 1)

    def to_col(row_vec):
        return jnp.transpose(jnp.broadcast_to(row_vec, (LANES, LANES)))[:nrow, 0:1]

    @pl.when(ch == 0)
    def _():
        c_end = lax.broadcasted_iota(jnp.int32, (1, n_cmp), 1) * CMP_STRIDE + CMP_BLOCK - 1
        s_c = _dot(q, kct_ref[0].astype(BF16)) + bc_ref[...]
        p_c = _masked_softmax(s_c, qpos - c_end >= 0, -1)
        ocmp_s[...] = _dot(p_c.astype(BF16), vc_ref[0].astype(BF16))

        p_ct = jnp.transpose(jnp.concatenate([p_c, jnp.zeros((LANES - nrow, n_cmp), F32)], axis=0))
        p_sum = _dot(p_ct, hsum_ref[...], precision=HIGHEST)
        imp = _dot(c2s_ref[...], p_sum, precision=HIGHEST)
        blk = lax.broadcasted_iota(jnp.int32, (n_sel_pad, 1), 0)
        cpos = past_len + (lane & (t_new - 1))
        cur = jnp.right_shift(cpos, SEL_SHIFT)
        forced = (blk == 0) | (blk == cur) | (blk == cur - 1)
        valid = blk * SEL <= cpos
        score = jnp.where(forced, FORCE, jnp.where(valid, imp, -FORCE))
        score = jnp.where(blk < n_sel, score, -3.0 * FORCE)
        score_s[...] = score

        def rank_body(b, cnt):
            s_b = score_s[pl.ds(b, 1), :]
            ahead = (s_b > score) | ((s_b == score) & (blk > b))
            return cnt + ahead.astype(jnp.int32)

        cnt = lax.fori_loop(0, n_sel, rank_body, jnp.zeros((n_sel_pad, LANES), jnp.int32))
        sel = jnp.where((cnt < min(TOPK, n_sel)) & (blk < n_sel), 1.0, 0.0).astype(BF16)
        sel_s[...] = _dot(sel, hexp_ref[...])

        key_w = past_len - WIN + lax.broadcasted_iota(jnp.int32, (1, WIN), 1)
        s_w = _dot(q, kwt_ref[0].astype(BF16)) + bw_ref[...]
        mask_w = lax.bitcast_convert_type(qpos - key_w, jnp.uint32) < WIN
        s_n = _dot_nt(q, kwn_ref[0].astype(BF16)) + bn_ref[...]
        mask_n = (lane < t_new) & (rtok - lane >= 0)
        s_w = jnp.where(mask_w, s_w, NEG)
        s_n = jnp.where(mask_n, s_n, NEG)
        m = jnp.maximum(jnp.max(s_w, axis=-1, keepdims=True), jnp.max(s_n, axis=-1, keepdims=True))
        e_w = jnp.where(mask_w, jnp.exp(s_w - m), 0.0)
        e_n = jnp.where(mask_n, jnp.exp(s_n - m), 0.0)
        den = jnp.maximum(jnp.sum(e_w, axis=-1, keepdims=True) + jnp.sum(e_n, axis=-1, keepdims=True), 1e-30)
        owin_s[...] = (_dot_nt((e_w / den).astype(BF16), vwt_ref[0].astype(BF16))
                       + _dot((e_n / den).astype(BF16), vwn_ref[0].astype(BF16)))

        s_t = _dot_nt(q, kn_ref[0].astype(BF16)) + bn_ref[...]
        mask_t = mask_n & (to_col(sel_s[pl.ds(past_len // SEL, 1), :]) > 0.5)
        s_t = jnp.where(mask_t, s_t, NEG)
        m0 = jnp.max(s_t, axis=-1, keepdims=True)
        p0 = jnp.where(mask_t, jnp.exp(s_t - m0), 0.0)
        m_s[...] = m0
        l_s[...] = jnp.sum(p0, axis=-1, keepdims=True)
        acc_s[...] = _dot(p0.astype(BF16), vn_ref[0].astype(BF16))

    kt_all = jnp.concatenate([r[0].reshape(KV_W, PAGE) for r in k_pages], axis=1).astype(BF16)
    vt_all = jnp.concatenate([r[0].reshape(KV_W, PAGE) for r in v_pages], axis=1).astype(BF16)
    s = _dot(q, kt_all) + bs_ref[0]
    blocks = pps * PAGE // SEL
    sel_blk = sel_s[pl.ds(pl.multiple_of(ch * blocks, blocks), blocks), :][:, :nrow].astype(BF16)
    mask = _dot_tn(sel_blk, e_ref[...]) > 0.5
    s = jnp.where(mask, s, NEG)
    m_old = m_s[...]
    m_new = jnp.maximum(m_old, jnp.max(s, axis=-1, keepdims=True))
    p = jnp.where(mask, jnp.exp(s - m_new), 0.0)
    alpha = jnp.exp(m_old - m_new)
    l_s[...] = alpha * l_s[...] + jnp.sum(p, axis=-1, keepdims=True)
    acc_s[...] = alpha * acc_s[...] + _dot_nt(p.astype(BF16), vt_all)
    m_s[...] = m_new

    @pl.when(ch == pl.num_programs(1) - 1)
    def _():
        o_slc = acc_s[...] / jnp.maximum(l_s[...], 1e-30)
        gates = gt_ref[0]
        hg_rows = NSA_HG * t_new
        for g in range(NSA_G):
            rs = slice(g * hg_rows, (g + 1) * hg_rows)
            cs = slice(g * DH, (g + 1) * DH)
            o_ref[0, rs, :] = (gates[rs, 0:1] * ocmp_s[rs, cs] + gates[rs, 1:2] * o_slc[rs, cs]
                               + gates[rs, 2:3] * owin_s[rs, cs])


def _nsa_sample(page_table, k_pool, v_pool, q_bd, kct, vc, k_new, v_new, kw_cache, vw_cache, kw_new, vw_new, gates,
                bias_c, bias_s, bias_n, bias_w, c2s_t, hsum, hexp, expand, *, pps, t_new):
    bsz, n_pages = page_table.shape
    past_len = n_pages * PAGE
    n_cmp = vc.shape[1]
    n_sel = past_len // SEL + 1
    n_sel_pad = c2s_t.shape[0]
    nrow = NSA_HEADS * t_new

    def page_spec(j):
        return pl.BlockSpec((1, NSA_G, DH, PAGE), lambda b, s, pt: (pt[b, s * pps + j], 0, 0, 0))

    per_b = lambda shape: pl.BlockSpec((1,) + shape, lambda b, s, pt: (b,) + (0,) * len(shape))
    const = lambda arr: pl.BlockSpec(arr.shape, lambda b, s, pt: (0,) * arr.ndim)
    in_specs = ([page_spec(j) for j in range(pps)] * 2
                + [per_b((nrow, KV_W)), per_b((KV_W, n_cmp)), per_b((n_cmp, KV_W)),
                   per_b((PAGE, KV_W)), per_b((PAGE, KV_W)), per_b((KV_W, WIN)), per_b((KV_W, WIN)),
                   per_b((PAGE, KV_W)), per_b((PAGE, KV_W)), per_b((nrow, 3)),
                   const(bias_c),
                   pl.BlockSpec((1, nrow, pps * PAGE), lambda b, s, pt: (s, 0, 0)),
                   const(bias_n), const(bias_w), const(c2s_t), const(hsum), const(hexp), const(expand)])
    grid_spec = pltpu.PrefetchScalarGridSpec(
        num_scalar_prefetch=1,
        grid=(bsz, n_pages // pps),
        in_specs=in_specs,
        out_specs=pl.BlockSpec((1, nrow, DH), lambda b, s, pt: (b, 0, 0)),
        scratch_shapes=[pltpu.VMEM((n_sel_pad, LANES), F32), pltpu.VMEM((n_sel_pad, LANES), F32),
                        pltpu.VMEM((nrow, 1), F32), pltpu.VMEM((nrow, 1), F32),
                        pltpu.VMEM((nrow, KV_W), F32), pltpu.VMEM((nrow, KV_W), F32),
                        pltpu.VMEM((nrow, KV_W), F32)],
    )
    return pl.pallas_call(
        functools.partial(_nsa_sample_kernel, pps=pps, n_cmp=n_cmp, n_sel=n_sel, n_sel_pad=n_sel_pad,
                          past_len=past_len, t_new=t_new),
        out_shape=jax.ShapeDtypeStruct((bsz, nrow, DH), F32),
        grid_spec=grid_spec,
        compiler_params=_cparams(("parallel", "arbitrary")),
        name="nsa_sample",
    )(page_table, *([k_pool] * pps), *([v_pool] * pps), q_bd, kct, vc, k_new, v_new, kw_cache, vw_cache, kw_new,
      vw_new, gates, bias_c, bias_s, bias_n, bias_w, c2s_t, hsum, hexp, expand)


def _post_kernel(x_ref, og_ref, on_ref, sma_ref, smb_ref, gate1_ref, sc2_ref, sh2_ref, gate2_ref, n2_ref, fg_ref,
                 wo_ref, wu_ref, wd_ref, y_ref):
    merged = sma_ref[...] * og_ref[...] + smb_ref[...] * on_ref[...]
    x1 = x_ref[...] + gate1_ref[0] * _dot(merged.astype(BF16), wo_ref[...])
    h2 = _rms(x1, n2_ref[...]) * (1.0 + sc2_ref[0]) + sh2_ref[0]
    up = jnp.maximum(_dot(h2.astype(BF16), wu_ref[...]), 0.0)
    x2 = x1 + gate2_ref[0] * _dot((up * up).astype(BF16), wd_ref[...])
    y_ref[...] = _rms(x2, fg_ref[...])


def _post(x, o_gla, o_nsa, sma, smb, gate1, scale2, shift2, gate2, n2, fg, wo, wu, wd, *, tm):
    m = x.shape[0]
    nb, r, _ = gate1.shape
    rows_per_mod = m // nb
    mod_spec = pl.BlockSpec((1, r, D_MODEL), lambda i: (i * tm // rows_per_mod, 0, 0))
    row = pl.BlockSpec((tm, D_MODEL), lambda i: (i, 0))
    return pl.pallas_call(
        _post_kernel,
        out_shape=jax.ShapeDtypeStruct((m, D_MODEL), F32),
        grid=(m // tm,),
        in_specs=[row] * 5 + [mod_spec] * 4 + [_const_spec((1, D_MODEL))] * 2
        + [_const_spec(wo.shape), _const_spec(wu.shape), _const_spec(wd.shape)],
        out_specs=row,
        compiler_params=_cparams(("parallel",)),
        name="post",
    )(x, o_gla, o_nsa, sma, smb, gate1, scale2, shift2, gate2, n2, fg, wo, wu, wd)


def _block_diag_w1(w1):
    w = w1.reshape(CMP_RATIO, CMP_STRIDE, DH, CMP_HID)
    eye = jnp.eye(NSA_G, dtype=w1.dtype)
    bd = jnp.einsum('redh,pg->epdgrh', w, eye)
    return bd.reshape(CMP_STRIDE * NSA_G * DH, NSA_G * CMP_RATIO * CMP_HID).astype(BF16)


def _tap_w1(w1):
    w = w1.reshape(CMP_RATIO, CMP_STRIDE // CMP_TAPS, CMP_TAPS, DH, CMP_HID)
    return jnp.transpose(w, (1, 2, 3, 0, 4)).reshape(CMP_STRIDE // CMP_TAPS, CMP_TAPS * DH,
                                                     CMP_RATIO * CMP_HID).astype(BF16)


def _cmp_to_sel_np(n_cmp, n_sel):
    cs = np.arange(n_cmp)[:, None] * CMP_STRIDE
    ss = np.arange(n_sel)[None, :] * SEL
    ov = np.minimum(cs + CMP_BLOCK, ss + SEL) - np.maximum(cs, ss)
    return np.clip(ov, 0, None).astype(np.float32) / CMP_BLOCK


def _bias_by_dist(rel_bias, n):
    return jnp.transpose(rel_bias[_bucket_np(np.arange(n))])


def _toeplitz(v, n):
    lead = v.shape[:-1]
    a = jnp.broadcast_to(v[..., None, :], lead + (n, 2 * n)).reshape(lead + (2 * n * n,))
    a = a[..., :n * (2 * n - 1)].reshape(lead + (n, 2 * n - 1))
    return a[..., n - 1:]


def _mod_split(mod, rows):
    parts = jnp.split(mod, 6, axis=-1)
    return [p.reshape(-1, rows, D_MODEL) for p in parts]


def kernel(x_prompt, x_sample, c_prompt, c_sample, state_gla, cache_k_cmp, cache_v_cmp, cache_k_slc, cache_v_slc, cache_k_win, cache_v_win, page_table, ada_w, ada_b, norm1_g, norm2_g, w_in, gla_a2, gla_a_b, gla_norm_g, cmp_pe_k, cmp_w1_k, cmp_w2_k, cmp_pe_v, cmp_w1_v, cmp_w2_v, w_o, w_up, w_down, rel_bias, final_g):
    depth = ada_w.shape[0]
    assert depth == 1, "single-layer trunk"
    bp, tp, _ = x_prompt.shape
    bs, ts, _ = x_sample.shape
    n_pages = page_table.shape[1]
    past_len = n_pages * PAGE
    mp, ms = bp * tp, bs * ts
    tm = 256
    tm_s = min(tm, ms)

    w = w_in[0]
    w_perm = jnp.concatenate([w[:, :3072], w[:, 3088:5648], w[:, 5696:7744], w[:, 5648:5696], w[:, 3072:3088],
                              jnp.zeros((D_MODEL, LANES - 64), w.dtype)], axis=1).astype(BF16)
    a2p = jnp.zeros((LANES, QK_W), F32).at[Z_GA:Z_GA + GLA_RANK].set(gla_a2[0]).astype(BF16)
    ab = gla_a_b[0].reshape(1, QK_W)
    g1 = norm1_g[0].reshape(1, D_MODEL)
    n2 = norm2_g[0].reshape(1, D_MODEL)
    fg = final_g.reshape(1, D_MODEL)
    gn = gla_norm_g[0].reshape(1, GLA_DV)
    wo, wu, wd = w_o[0].astype(BF16), w_up[0].astype(BF16), w_down[0].astype(BF16)
    cmp_k = (_block_diag_w1(cmp_w1_k[0]), jnp.broadcast_to(cmp_pe_k[0].reshape(1, -1), (8, CMP_BLOCK * DH)),
             cmp_w1_k[0].reshape(CMP_BLOCK * DH, CMP_HID), cmp_w2_k[0])
    cmp_v = (_block_diag_w1(cmp_w1_v[0]), jnp.broadcast_to(cmp_pe_v[0].reshape(1, -1), (8, CMP_BLOCK * DH)),
             cmp_w1_v[0].reshape(CMP_BLOCK * DH, CMP_HID), cmp_w2_v[0])

    mod = _ada(jnp.concatenate([c_prompt, c_sample], axis=0), ada_w[0], ada_b[0])
    sh1_p, sc1_p, gt1_p, sh2_p, sc2_p, gt2_p = _mod_split(mod[:bp], 1)
    sh1_s, sc1_s, gt1_s, sh2_s, sc2_s, gt2_s = _mod_split(jnp.repeat(mod[bp:], ts, axis=0), tm_s)

    xp = x_prompt.reshape(mp, D_MODEL)
    (gq, gk, gv, sgr, la, kc, vc, ks, vs, kw, vw, _, sma, smb, qt, ksh, vst, kwh, vwt, gtt) = _inproj(
        xp, sc1_p, sh1_p, g1, w_perm, a2p, ab, head_major=True, tm=tm)

    r3 = lambda a: a.reshape(bp, tp, a.shape[-1])
    o_gla_p, s_gla_p = _gla(r3(gq), r3(gk), r3(gv), r3(la), r3(sgr), gn, None, tb=256)

    n_seg = tp // CMP_STRIDE
    assert n_seg == LANES, "prompt attention keeps one compressed block per lane"
    n_cmp_p = n_seg - CMP_RATIO + 1
    seg = lambda a: a.reshape(mp // CMP_STRIDE, CMP_STRIDE * KV_W)
    tm_seg = min(tm, mp // CMP_STRIDE)
    kcc = _cmpfin(_mm(seg(kc), cmp_k[0], tm=tm_seg).reshape(bp, n_seg, -1), None, *cmp_k[1:], layout='rows')
    vcc = _cmpfin(_mm(seg(vc), cmp_v[0], tm=tm_seg).reshape(bp, n_seg, -1), None, *cmp_v[1:3],
                  jnp.transpose(cmp_v[3]), layout='lanes')

    n_sel_p = tp // SEL
    f = _bias_by_dist(rel_bias, 2 * TQ)
    v_same = jnp.concatenate([jnp.repeat(f[:, :1], TQ - 1, axis=1), f[:, :TQ + 1]], axis=1)
    v_prev = jnp.concatenate([f[:, 1:], f[:, -1:]], axis=1)
    tiles = _toeplitz(jnp.stack([v_same, v_prev]), TQ).reshape(2, NSA_G, NSA_HG, TQ, TQ)
    tiles = jnp.transpose(tiles, (0, 1, 3, 2, 4)).reshape(2, NSA_G, TQ, NSA_HG * TQ)
    far_cols = jnp.broadcast_to(f[:, -1].reshape(NSA_G, NSA_HG, 1), (NSA_G, NSA_HG, TQ)).reshape(NSA_G, 1, -1)
    bias_t = jnp.concatenate([tiles, jnp.broadcast_to(far_cols[None], (1, NSA_G, TQ, NSA_HG * TQ))], axis=0)
    d_wc = (np.arange(TQ)[None, :] - CMP_STRIDE * (np.arange(CMP_WIN_ROWS)[:, None] - CMP_WIN_PAD)
            - (CMP_BLOCK - 1))
    bias_wc = f[:, np.clip(d_wc, 0, 2 * TQ - 1)].reshape(NSA_G, NSA_HG, CMP_WIN_ROWS, TQ)
    bias_wc = jnp.transpose(bias_wc, (0, 2, 1, 3)).reshape(NSA_G, CMP_WIN_ROWS, NSA_HG * TQ)
    c2s_t = np.zeros((n_sel_p, n_seg), np.float32)
    c2s_t[:, :n_cmp_p] = _cmp_to_sel_np(n_cmp_p, n_sel_p).T
    k4 = lambda a: a.reshape(NSA_G, bp, tp, DH)
    o_nsa_p = _nsa_prompt(qt, kcc, vcc, k4(ksh), vst, k4(kwh), vwt, gtt, bias_t, bias_wc, far_cols,
                          jnp.asarray(c2s_t), bsz=bp, n_cmp=n_cmp_p)

    y_p = _post(xp, o_gla_p.reshape(mp, V_W), o_nsa_p.reshape(mp, D_MODEL), sma, smb, gt1_p, sc2_p, sh2_p, gt2_p,
                n2, fg, wo, wu, wd, tm=tm)

    kv5 = lambda a, b, t: a.reshape(1, b, t, NSA_G, DH)
    wp = min(WIN, tp)
    p_out = (s_gla_p[None], kv5(kc, bp, tp), kv5(vc, bp, tp), kv5(ks, bp, tp), kv5(vs, bp, tp),
             kv5(kw, bp, tp)[:, :, tp - wp:], kv5(vw, bp, tp)[:, :, tp - wp:])

    xs = x_sample.reshape(ms, D_MODEL)
    (gq, gk, gv, sgr, la, kc, vc, ks, vs, kw, vw, gates, sma, smb, nq) = _inproj(
        xs, sc1_s, sh1_s, g1, w_perm, a2p, ab, head_major=False, tm=tm_s)

    t_pad = 8
    pad_t = lambda a: jnp.pad(a.reshape(bs, ts, a.shape[-1]), ((0, 0), (0, t_pad - ts), (0, 0)))
    o_gla_s, s_gla_s = _gla(pad_t(gq), pad_t(gk), pad_t(gv), pad_t(la), pad_t(sgr), gn, state_gla[0], tb=t_pad)
    o_gla_s = o_gla_s[:, :ts].reshape(ms, V_W)

    pool_t = lambda a: jnp.transpose(a[0], (0, 2, 3, 1))
    new_seg = lambda a: jnp.pad(a.reshape(bs, ts, KV_W), ((0, 0), (0, CMP_STRIDE - ts), (0, 0))).reshape(
        bs, CMP_STRIDE * KV_W)
    n_cmp_s = past_len // CMP_STRIDE
    pps_cmp = min(16, n_pages)
    kcc = _cmpfin(_cmp_paged(pool_t(cache_k_cmp), page_table, _tap_w1(cmp_w1_k[0]), pages_per_step=pps_cmp),
                  _mm(new_seg(kc), cmp_k[0], tm=bs).reshape(bs, 1, -1), *cmp_k[1:3], jnp.transpose(cmp_k[3]),
                  layout='lanes').reshape(bs, KV_W, n_cmp_s)
    vcc = _cmpfin(_cmp_paged(pool_t(cache_v_cmp), page_table, _tap_w1(cmp_w1_v[0]), pages_per_step=pps_cmp),
                  _mm(new_seg(vc), cmp_v[0], tm=bs).reshape(bs, 1, -1), *cmp_v[1:], layout='flat')

    q5 = nq.reshape(bs, ts, NSA_G, NSA_HG, DH)
    q_bd = jnp.einsum('btghd,pg->bghtpd', q5, jnp.eye(NSA_G, dtype=F32)).reshape(bs, NSA_HEADS * ts, KV_W)
    q_bd = q_bd.astype(BF16)

    ncol = NSA_HEADS * ts
    col_head = np.arange(ncol) // ts
    col_tok = np.arange(ncol) % ts

    def bias_rows(key_pos):
        dist = past_len + col_tok[:, None] - key_pos[None, :]
        near = np.nonzero(dist.min(axis=0) < REL_MAX)[0]
        lo = int(near.min()) if near.size else len(key_pos)
        far_part = jnp.broadcast_to(f[col_head, -1][:, None], (ncol, lo))
        near_part = f[col_head[:, None], np.clip(dist[:, lo:], 0, 2 * TQ - 1)]
        return jnp.concatenate([far_part, near_part], axis=1)

    pps = min(16, n_pages)
    bias_c_s = bias_rows(np.arange(n_cmp_s) * CMP_STRIDE + CMP_BLOCK - 1)
    bias_s_s = jnp.transpose(bias_rows(np.arange(past_len)).reshape(ncol, n_pages // pps, pps * PAGE), (1, 0, 2))
    bias_n_s = bias_rows(past_len + np.arange(PAGE))
    bias_w_s = bias_rows(past_len - WIN + np.arange(WIN))
    n_sel_s = past_len // SEL + 1
    n_sel_pad = -(-n_sel_s // 8) * 8
    c2s_t = np.zeros((n_sel_pad, n_cmp_s), np.float32)
    c2s_t[:n_sel_s] = _cmp_to_sel_np(n_cmp_s, n_sel_s).T
    gt_col = np.arange(LANES) // (NSA_HG * ts) * ts + np.arange(LANES) % ts
    hsum = ((gt_col[:, None] == np.arange(LANES)[None, :]) & (np.arange(LANES)[:, None] < ncol)).astype(np.float32)
    expand_s = (np.arange(pps * PAGE)[None, :] // SEL == np.arange(pps * PAGE // SEL)[:, None]).astype(np.float32)
    gates_s = jnp.transpose(gates[:, :3 * NSA_HEADS].reshape(bs, ts, 3, NSA_HEADS), (0, 3, 1, 2)).reshape(bs, ncol, 3)
    new_rows = lambda a: jnp.pad(a.reshape(bs, ts, KV_W), ((0, 0), (0, PAGE - ts), (0, 0)))
    wb = cache_k_win.shape[2]
    win_t = lambda a: pool_t(a).reshape(bs, KV_W, wb)
    o_s = _nsa_sample(page_table, pool_t(cache_k_slc), pool_t(cache_v_slc), q_bd, kcc, vcc, new_rows(ks),
                      new_rows(vs), win_t(cache_k_win), win_t(cache_v_win), new_rows(kw),
                      new_rows(vw), gates_s, bias_c_s, bias_s_s, bias_n_s, bias_w_s, jnp.asarray(c2s_t),
                      jnp.asarray(hsum), jnp.asarray(hsum.T, dtype=BF16), jnp.asarray(expand_s, dtype=BF16),
                      pps=pps, t_new=ts)
    o_nsa_s = jnp.transpose(o_s.reshape(bs, NSA_HEADS, ts, DH), (0, 2, 1, 3)).reshape(ms, D_MODEL)

    y_s = _post(xs, o_gla_s, o_nsa_s, sma, smb, gt1_s, sc2_s, sh2_s, gt2_s, n2, fg, wo, wu, wd, tm=tm_s)

    kw_new = jnp.concatenate([cache_k_win[0], kv5(kw, bs, ts)[0]], axis=1)[:, ts:][None]
    vw_new = jnp.concatenate([cache_v_win[0], kv5(vw, bs, ts)[0]], axis=1)[:, ts:][None]
    s_out = (s_gla_s[None], kv5(kc, bs, ts), kv5(vc, bs, ts), kv5(ks, bs, ts), kv5(vs, bs, ts), kw_new, vw_new)

    return (y_p.reshape(bp, tp, D_MODEL), y_s.reshape(bs, ts, D_MODEL)) + p_out + s_out
```

```python
import functools
import math

import numpy as np
import jax
import jax.numpy as jnp
from jax import lax
from jax.experimental import pallas as pl
from jax.experimental.pallas import tpu as pltpu

F32 = jnp.float32
BF16 = jnp.bfloat16
HIGHEST = lax.Precision.HIGHEST

D_MODEL = 1024
GLA_HEADS = 4
GLA_DK = 128
GLA_DV = 256
GLA_RANK = 16
GLA_TAU = 16.0
GLA_CHUNK = 64
GLA_SUB = 16
NSA_HEADS = 16
DH = 64
NSA_G = 4
NSA_HG = 4
CMP_BLOCK = 32
CMP_STRIDE = 16
CMP_RATIO = 2
CMP_HID = 128
SEL = 64
SEL_SHIFT = 6
TOPK = 16
WIN = 512
REL_BUCKETS = 32
REL_MAX = 128
D_FF = 4096
EPS = 1e-6
NEG = -1e30
LOG2E = math.log2(math.e)
FORCE = 1e4
PAGE = 128

QK_W = GLA_HEADS * GLA_DK
V_W = GLA_HEADS * GLA_DV
KV_W = NSA_G * DH
LANES = 128
TQ = 128
VMEM_LIMIT = 56 * 1024 * 1024

C_GQ, C_GK, C_GV, C_GR, C_NQ = 0, 512, 1024, 2048, 3072
C_KV = 4096
C_MA, C_MB, C_Z = 5632, 6656, 7680
W_COLS = 7808
Z_GA = 48


def _cparams(sem):
    return pltpu.CompilerParams(dimension_semantics=sem, vmem_limit_bytes=VMEM_LIMIT)


def _const_spec(shape):
    return pl.BlockSpec(shape, lambda *a: (0,) * len(shape), pipeline_mode=pl.Buffered(1))


def _sigmoid(x):
    return 1.0 / (1.0 + jnp.exp(-x))


def _rms(x, g):
    return x * lax.rsqrt(jnp.mean(x * x, axis=-1, keepdims=True) + EPS) * g


def _dot(a, b, **kw):
    return jnp.dot(a, b, preferred_element_type=F32, **kw)


def _dot_nt(a, b):
    return lax.dot_general(a, b, (((1,), (1,)), ((), ())), preferred_element_type=F32)


def _dot_tn(a, b):
    return lax.dot_general(a, b, (((0,), (0,)), ((), ())), preferred_element_type=F32)


def _bucket_np(dist):
    n = np.maximum(dist, 0)
    nf = np.maximum(n, 1).astype(np.float64)
    large = 16 + (np.log(nf / 16.0) / math.log(REL_MAX / 16.0) * 16.0).astype(np.int64)
    large = np.minimum(large, REL_BUCKETS - 1)
    return np.where(n < 16, n, large).astype(np.int32)


def _ada_kernel(c_ref, w_ref, b_ref, o_ref):
    c = c_ref[...]
    o_ref[...] = _dot(c * _sigmoid(c), w_ref[...], precision=HIGHEST) + b_ref[...]


def _ada(c, w, b):
    r, d = c.shape
    n = w.shape[1]
    return pl.pallas_call(
        _ada_kernel,
        out_shape=jax.ShapeDtypeStruct((r, n), F32),
        grid=(n // d,),
        in_specs=[pl.BlockSpec((r, d), lambda j: (0, 0)),
                  pl.BlockSpec((d, d), lambda j: (0, j)),
                  pl.BlockSpec((1, d), lambda j: (0, j))],
        out_specs=pl.BlockSpec((r, d), lambda j: (0, j)),
        compiler_params=_cparams(("arbitrary",)),
        name="ada",
    )(c, w, b.reshape(1, n))


def _inproj_kernel(x_ref, sc_ref, sh_ref, g_ref, w_ref, a2_ref, ab_ref, *outs, head_major):
    (gq_o, gk_o, gv_o, sgr_o, la_o, kc_o, vc_o, ks_o, vs_o, kw_o, vw_o, gt_o, sma_o, smb_o, *rest) = outs
    h = _rms(x_ref[...], g_ref[...]) * (1.0 + sc_ref[0]) + sh_ref[0]
    hb = h.astype(BF16)

    def proj(lo, hi):
        return _dot(hb, w_ref[:, lo:hi])

    gq_o[...] = proj(C_GQ, C_GK) * (GLA_DK ** -0.5)
    gk_o[...] = proj(C_GK, C_GV)
    gv_o[...] = proj(C_GV, C_GR)
    gr = proj(C_GR, C_NQ)
    sgr_o[...] = gr * _sigmoid(gr)
    nq = proj(C_NQ, C_KV) * (DH ** -0.5)
    kv = proj(C_KV, C_MA)
    for n, o in enumerate((kc_o, vc_o, ks_o, vs_o, kw_o, vw_o)):
        o[...] = kv[:, n * KV_W:(n + 1) * KV_W]
    sma_o[...] = _sigmoid(proj(C_MA, C_MB))
    smb_o[...] = _sigmoid(proj(C_MB, C_Z))
    z = proj(C_Z, W_COLS)
    gt_o[...] = _sigmoid(z)
    pre = _dot(z.astype(BF16), a2_ref[...]) + ab_ref[...]
    la_o[...] = (jnp.minimum(pre, 0.0) - jnp.log(1.0 + jnp.exp(-jnp.abs(pre)))) * (1.0 / GLA_TAU)
    if head_major:
        qt_o, ksh_o, vst_o, kwh_o, vwt_o, gtt_o = rest

        def put_transposed(o, x):
            for c in range(x.shape[1] // LANES):
                t = jnp.transpose(x[:, c * LANES:(c + 1) * LANES]).astype(BF16)
                o[2 * c] = t[:DH]
                o[2 * c + 1] = t[DH:]

        put_transposed(qt_o, nq * LOG2E)
        put_transposed(vst_o, kv[:, 3 * KV_W:4 * KV_W])
        put_transposed(vwt_o, kv[:, 5 * KV_W:6 * KV_W])
        for n, o in ((2, ksh_o), (4, kwh_o)):
            for gg in range(NSA_G):
                lo = n * KV_W + gg * DH
                o[gg] = kv[:, lo:lo + DH].astype(BF16)
        gtt_o[...] = jnp.transpose(_sigmoid(z))
    else:
        (nq_o,) = rest
        nq_o[...] = nq


def _inproj(x, scale, shift, g1, w, a2p, ab, *, head_major, tm):
    m = x.shape[0]
    nb, r, _ = scale.shape
    rows_per_mod = m // nb
    mod_spec = pl.BlockSpec((1, r, D_MODEL), lambda i: (i * tm // rows_per_mod, 0, 0))
    row = lambda wdt: pl.BlockSpec((tm, wdt), lambda i: (i, 0))
    widths = [QK_W, QK_W, V_W, V_W, QK_W] + [KV_W] * 6 + [LANES, D_MODEL, D_MODEL]
    out_shape = [jax.ShapeDtypeStruct((m, wd), F32) for wd in widths]
    out_specs = [row(wd) for wd in widths]
    if head_major:
        def rows_major(n):
            out_shape.append(jax.ShapeDtypeStruct((n, m, DH), BF16))
            out_specs.append(pl.BlockSpec((n, tm, DH), lambda i: (0, i, 0)))

        def lanes_major(n):
            out_shape.append(jax.ShapeDtypeStruct((n, DH, m), BF16))
            out_specs.append(pl.BlockSpec((n, DH, tm), lambda i: (0, 0, i)))

        lanes_major(NSA_HEADS)
        rows_major(NSA_G)
        lanes_major(NSA_G)
        rows_major(NSA_G)
        lanes_major(NSA_G)
        out_shape.append(jax.ShapeDtypeStruct((LANES, m), F32))
        out_specs.append(pl.BlockSpec((LANES, tm), lambda i: (0, i)))
    else:
        out_shape.append(jax.ShapeDtypeStruct((m, D_MODEL), F32))
        out_specs.append(row(D_MODEL))
    return pl.pallas_call(
        functools.partial(_inproj_kernel, head_major=head_major),
        out_shape=out_shape,
        grid=(m // tm,),
        in_specs=[row(D_MODEL), mod_spec, mod_spec, _const_spec((1, D_MODEL)), _const_spec((D_MODEL, W_COLS)),
                  _const_spec((LANES, QK_W)), _const_spec((1, QK_W))],
        out_specs=out_specs,
        compiler_params=_cparams(("parallel",)),
        name="inproj",
    )(x, scale, shift, g1, w, a2p, ab)


def _gla_kernel(*refs, chunk, sub, n_chunk, has_init):
    if has_init:
        q_ref, k_ref, v_ref, la_ref, sgr_ref, gn_ref, s0_ref, o_ref, so_ref, s_scr = refs
    else:
        q_ref, k_ref, v_ref, la_ref, sgr_ref, gn_ref, o_ref, so_ref, s_scr = refs
    t = pl.program_id(1)

    @pl.when(t == 0)
    def _():
        s_scr[...] = s0_ref[0] if has_init else jnp.zeros_like(s_scr)

    tril = (lax.broadcasted_iota(jnp.int32, (chunk, chunk), 0)
            >= lax.broadcasted_iota(jnp.int32, (chunk, chunk), 1)).astype(F32)
    sub_row = lax.broadcasted_iota(jnp.int32, (sub, 1), 0)

    def one_chunk(c, carry):
        r0 = pl.multiple_of(c * chunk, chunk)
        rows = pl.ds(r0, chunk)
        b_all = _dot(tril, la_ref[0, rows, :], precision=HIGHEST)
        for h in range(GLA_HEADS):
            one_head(h, rows, b_all[:, h * GLA_DK:(h + 1) * GLA_DK])
        return carry

    def one_head(h, rows, b):
        ks, vs = slice(h * GLA_DK, (h + 1) * GLA_DK), slice(h * GLA_DV, (h + 1) * GLA_DV)
        q = q_ref[0, rows, ks]
        k = k_ref[0, rows, ks]
        v = v_ref[0, rows, vs]
        s_old = s_scr[h]
        vb = v.astype(BF16)
        o_inter = _dot((q * jnp.exp(b)).astype(BF16), s_old.astype(BF16))
        blocks = []
        for blk in range(chunk // sub):
            sl = slice(blk * sub, (blk + 1) * sub)
            q_i, k_i, v_i, b_i = q[sl], k[sl], v[sl], b[sl]
            acc = jnp.zeros((sub, GLA_DV), F32)
            for j in range(sub):
                w = q_i * k_i[j:j + 1] * jnp.exp(jnp.minimum(b_i - b_i[j:j + 1], 0.0))
                a = jnp.sum(w, axis=-1, keepdims=True)
                acc = acc + jnp.where(sub_row >= j, a, 0.0) * v_i[j:j + 1]
            if blk > 0:
                n = blk * sub
                ref_row = b_i[0:1]
                qp = (q_i * jnp.exp(b_i - ref_row)).astype(BF16)
                kp = (k[:n] * jnp.exp(ref_row - b[:n])).astype(BF16)
                acc = acc + _dot(_dot_nt(qp, kp).astype(BF16), vb[:n])
            blocks.append(acc)
        o = o_inter + (jnp.concatenate(blocks, axis=0) if len(blocks) > 1 else blocks[0])
        b_last = b[chunk - 1:chunk]
        kp = (k * jnp.exp(b_last - b)).astype(BF16)
        decay_col = jnp.transpose(jnp.broadcast_to(jnp.exp(b_last), (GLA_DK, GLA_DK)))[:, 0:1]
        s_scr[h] = decay_col * s_old + _dot_tn(kp, vb)
        o_ref[0, rows, vs] = _rms(o, gn_ref[...]) * sgr_ref[0, rows, vs]

    lax.fori_loop(0, n_chunk, one_chunk, 0)

    @pl.when(t == pl.num_programs(1) - 1)
    def _():
        so_ref[0] = s_scr[...]


def _gla(q, k, v, la, sgr, gn, s0, *, tb):
    bsz, t, _ = q.shape
    chunk = min(GLA_CHUNK, tb)
    sub = min(GLA_SUB, chunk)
    blk = lambda wd: pl.BlockSpec((1, tb, wd), lambda b, i: (b, i, 0))
    st_spec = pl.BlockSpec((1, GLA_HEADS, GLA_DK, GLA_DV), lambda b, i: (b, 0, 0, 0))
    in_specs = [blk(QK_W), blk(QK_W), blk(V_W), blk(QK_W), blk(V_W),
                pl.BlockSpec((1, GLA_DV), lambda b, i: (0, 0))]
    args = [q, k, v, la, sgr, gn]
    if s0 is not None:
        in_specs.append(st_spec)
        args.append(s0)
    return pl.pallas_call(
        functools.partial(_gla_kernel, chunk=chunk, sub=sub, n_chunk=tb // chunk, has_init=s0 is not None),
        out_shape=[jax.ShapeDtypeStruct((bsz, t, V_W), F32),
                   jax.ShapeDtypeStruct((bsz, GLA_HEADS, GLA_DK, GLA_DV), F32)],
        grid=(bsz, t // tb),
        in_specs=in_specs,
        out_specs=[blk(V_W), st_spec],
        scratch_shapes=[pltpu.VMEM((GLA_HEADS, GLA_DK, GLA_DV), F32)],
        compiler_params=_cparams(("parallel", "arbitrary")),
        name="gla",
    )(*args)


def _mm_kernel(x_ref, w_ref, o_ref):
    o_ref[...] = _dot(x_ref[...].astype(BF16), w_ref[...])


def _mm(x, w, *, tm):
    m, kdim = x.shape
    n = w.shape[1]
    return pl.pallas_call(
        _mm_kernel,
        out_shape=jax.ShapeDtypeStruct((m, n), F32),
        grid=(m // tm,),
        in_specs=[pl.BlockSpec((tm, kdim), lambda i: (i, 0)), _const_spec((kdim, n))],
        out_specs=pl.BlockSpec((tm, n), lambda i: (i, 0)),
        compiler_params=_cparams(("parallel",)),
        name="cmp_mm",
    )(x, w)


CMP_TAPS = 4


def _cmp_paged_kernel(*refs, pps):
    pages = refs[1:1 + pps]
    w_ref, o_ref, tok_s = refs[1 + pps:]
    heads_per_slab = LANES // DH
    n_slab = NSA_G // heads_per_slab
    for p in range(pps):
        t = jnp.transpose(pages[p][0].reshape(KV_W, PAGE))
        for sl in range(n_slab):
            tok_s[sl, p * PAGE:(p + 1) * PAGE, :] = t[:, sl * LANES:(sl + 1) * LANES]
    n_seg = pps * PAGE // CMP_STRIDE
    width = CMP_RATIO * CMP_HID
    first_half = lax.broadcasted_iota(jnp.int32, (1, LANES), 1) < DH
    for sl in range(n_slab):
        x = [tok_s[sl, pl.ds(e, n_seg, stride=CMP_STRIDE), :] for e in range(CMP_STRIDE)]
        r = [pltpu.roll(v, DH, 1) for v in x]
        for odd in range(heads_per_slab):
            g = sl * heads_per_slab + odd
            acc = None
            for j in range(CMP_STRIDE // CMP_TAPS):
                pieces = []
                for e in range(j * CMP_TAPS, (j + 1) * CMP_TAPS, 2):
                    lo, hi = (r[e], x[e + 1]) if odd else (x[e], r[e + 1])
                    pieces.append(jnp.where(first_half, lo, hi))
                term = _dot(jnp.concatenate(pieces, axis=1).astype(BF16), w_ref[j])
                acc = term if acc is None else acc + term
            o_ref[0, :, g * width:(g + 1) * width] = acc


def _cmp_paged(pool_t, page_table, w_taps, *, pages_per_step):
    bsz, n_pages = page_table.shape
    pps = pages_per_step
    n_seg = pps * PAGE // CMP_STRIDE
    width = NSA_G * CMP_RATIO * CMP_HID

    def page_spec(j):
        return pl.BlockSpec((1, NSA_G, DH, PAGE), lambda b, s, pt: (pt[b, s * pps + j], 0, 0, 0))

    grid_spec = pltpu.PrefetchScalarGridSpec(
        num_scalar_prefetch=1,
        grid=(bsz, n_pages // pps),
        in_specs=[page_spec(j) for j in range(pps)]
        + [pl.BlockSpec(w_taps.shape, lambda b, s, pt: (0, 0, 0), pipeline_mode=pl.Buffered(1))],
        out_specs=pl.BlockSpec((1, n_seg, width), lambda b, s, pt: (b, s, 0)),
        scratch_shapes=[pltpu.VMEM((NSA_G * DH // LANES, pps * PAGE, LANES), F32)],
    )
    return pl.pallas_call(
        functools.partial(_cmp_paged_kernel, pps=pps),
        out_shape=jax.ShapeDtypeStruct((bsz, n_pages * PAGE // CMP_STRIDE, width), F32),
        grid_spec=grid_spec,
        compiler_params=_cparams(("parallel", "arbitrary")),
        name="cmp_paged",
    )(page_table, *([pool_t] * pps), w_taps)


def _cmpfin_kernel(*refs, n, has_new, layout):
    if has_new:
        p_ref, pn_ref, pe_ref, w1_ref, w2_ref, o_ref = refs
    else:
        p_ref, pe_ref, w1_ref, w2_ref, o_ref = refs
    pe_term = _dot(pe_ref[...], w1_ref[...], precision=HIGHEST)[0:1]
    w2 = w2_ref[...].astype(BF16)
    rowi = lax.broadcasted_iota(jnp.int32, (n, 1), 0)
    for g in range(NSA_G):
        base = g * CMP_RATIO * CMP_HID
        first = p_ref[0, :, base:base + CMP_HID]
        second = pltpu.roll(p_ref[0, :, base + CMP_HID:base + 2 * CMP_HID], n - 1, 0)
        if has_new:
            second = jnp.where(rowi == n - 1, pn_ref[0, :, base + CMP_HID:base + 2 * CMP_HID], second)
        acc = first + second + pe_term
        hid = (acc * _sigmoid(acc)).astype(BF16)
        if layout == 'flat':
            o_ref[0, :, g * DH:(g + 1) * DH] = _dot(hid, w2)
        elif layout == 'rows':
            o_ref[0, g] = _dot(hid, w2)
        else:
            o_ref[0, g] = _dot_nt(w2, hid)


def _cmpfin(parts, parts_new, pe, w1, w2, *, layout):
    bsz, n, width = parts.shape
    in_specs = [pl.BlockSpec((1, n, width), lambda b: (b, 0, 0))]
    args = [parts]
    if parts_new is not None:
        in_specs.append(pl.BlockSpec((1, 1, width), lambda b: (b, 0, 0)))
        args.append(parts_new)
    in_specs += [_const_spec(pe.shape), _const_spec(w1.shape), _const_spec(w2.shape)]
    if layout == 'flat':
        out_shape = jax.ShapeDtypeStruct((bsz, n, KV_W), F32)
        out_spec = pl.BlockSpec((1, n, KV_W), lambda b: (b, 0, 0))
    else:
        inner = (n, DH) if layout == 'rows' else (DH, n)
        out_shape = jax.ShapeDtypeStruct((bsz, NSA_G) + inner, F32)
        out_spec = pl.BlockSpec((1, NSA_G) + inner, lambda b: (b, 0, 0, 0))
    return pl.pallas_call(
        functools.partial(_cmpfin_kernel, n=n, has_new=parts_new is not None, layout=layout),
        out_shape=out_shape,
        grid=(bsz,),
        in_specs=in_specs,
        out_specs=out_spec,
        compiler_params=_cparams(("parallel",)),
        name="cmp_finish",
    )(*args, pe, w1, w2)


def _masked_softmax(s, mask, axis, exp=jnp.exp):
    s = jnp.where(mask, s, NEG)
    m = jnp.max(s, axis=axis, keepdims=True)
    e = jnp.where(mask, exp(s - m), 0.0)
    return e / jnp.maximum(jnp.sum(e, axis=axis, keepdims=True), 1e-30)


CMP_WIN_ROWS = 24
CMP_WIN_PAD = 16


def _nsa_prompt_kernel(qt_ref, kc_ref, vct_ref, ks_ref, vst_ref, kw_ref, vwt_ref, gtt_ref, bt_ref, wc_ref,
                       c2s_ref, o_ref, sel_s, bc_s, m_s, l_s, acc_s, *, n_cmp, n_sel, n_tiles):
    i = pl.program_id(1)
    tok = i * TQ + lax.broadcasted_iota(jnp.int32, (1, TQ), 1)
    tok4 = jnp.concatenate([tok] * NSA_HG, axis=1)
    row = lax.broadcasted_iota(jnp.int32, (TQ, 1), 0)
    groups = range(NSA_G)
    q4 = [jnp.concatenate([qt_ref[g * NSA_HG + h] for h in range(NSA_HG)], axis=1) for g in groups]

    def tile4(x):
        return jnp.concatenate([x] * NSA_HG, axis=1)

    o_cmp = []
    for g in groups:
        bc_s[g] = jnp.zeros(bc_s.shape[1:], F32)
        bc_s[g, pl.ds(pl.multiple_of(i * (TQ // CMP_STRIDE), TQ // CMP_STRIDE), CMP_WIN_ROWS), :] = wc_ref[g]
        s_c = _dot(kc_ref[0, g].astype(BF16), q4[g]) + bc_s[g, CMP_WIN_PAD:CMP_WIN_PAD + TQ, :]
        mask_c = (tok4 - (row * CMP_STRIDE + CMP_BLOCK - 1) >= 0) & (row < n_cmp)
        p_c = _masked_softmax(s_c, mask_c, 0, exp=jnp.exp2)
        o_cmp.append(_dot(vct_ref[0, g].astype(BF16), p_c.astype(BF16)))
        p_sum = p_c[:, 0:TQ] + p_c[:, TQ:2 * TQ] + p_c[:, 2 * TQ:3 * TQ] + p_c[:, 3 * TQ:4 * TQ]
        imp = _dot(c2s_ref[...], p_sum, precision=HIGHEST)
        blk = lax.broadcasted_iota(jnp.int32, (n_sel, 1), 0)
        cur = jnp.right_shift(tok, SEL_SHIFT)
        forced = (blk == 0) | (blk == cur) | (blk == cur - 1)
        valid = blk * SEL <= tok
        score = jnp.where(forced, FORCE, jnp.where(valid, imp, -FORCE))
        cnt = jnp.zeros((n_sel, TQ), jnp.int32)
        for b in range(n_sel):
            s_b = score[b:b + 1]
            ahead = (s_b > score) | ((s_b == score) & (blk > b))
            cnt = cnt + ahead.astype(jnp.int32)
        sel_s[g] = jnp.where(cnt < min(TOPK, n_sel), 1.0, 0.0)

    def tile_scores(k_ref, g, kt):
        k0 = pl.multiple_of(jnp.clip(kt, 0, n_tiles - 1) * TQ, TQ)
        return _dot(k_ref[g, 0, pl.ds(k0, TQ), :], q4[g]), k0

    def sel_mask(g, kt, valid):
        blocks_per_tile = TQ // SEL
        first = jnp.clip(kt, 0, n_tiles - 1) * blocks_per_tile
        out = None
        for j in reversed(range(blocks_per_tile)):
            add_j = jnp.where((sel_s[g, pl.ds(first + j, 1), :] > 0.5) & valid, 0.0, NEG)
            out = add_j if out is None else jnp.where(row < (j + 1) * SEL, add_j, out)
        return out

    def update(state, tiles):
        m_run, l_run, acc = state
        m_new = m_run
        for s, _ in tiles:
            m_new = jnp.maximum(m_new, jnp.max(s, axis=0, keepdims=True))
        alpha = jnp.exp2(m_run - m_new)
        l_new = alpha * l_run
        acc = alpha * acc
        for s, vt in tiles:
            p = jnp.exp2(s - m_new)
            l_new = l_new + jnp.sum(p, axis=0, keepdims=True)
            acc = acc + _dot(vt, p.astype(BF16))
        return m_new, l_new, acc

    def near_tiles(k_ref, vt_ref, g, mask_fn):
        tiles = []
        for typ, kt in ((1, i - 1), (0, i)):
            s, k0 = tile_scores(k_ref, g, kt)
            madd = mask_fn(kt, tok - (kt * TQ + row))
            tiles.append((s + bt_ref[typ, g] + tile4(madd), vt_ref[g, :, pl.ds(k0, TQ)]))
        return tiles

    cols = NSA_HG * TQ
    empty = (jnp.full((1, cols), NEG, F32), jnp.zeros((1, cols), F32), jnp.zeros((DH, cols), F32))

    def normalised(state):
        return state[2] / jnp.maximum(state[1], 1e-30)

    for g in groups:
        m_s[g], l_s[g], acc_s[g] = empty
    n_far = jnp.maximum(i - 1, 0)

    def far_body(c, carry):
        for g in groups:
            tiles = []
            for u in range(4):
                kt = 4 * c + u
                s, k0 = tile_scores(ks_ref, g, kt)
                tiles.append((s + tile4(sel_mask(g, kt, kt < n_far)), vst_ref[g, :, pl.ds(k0, TQ)]))
            m_s[g], l_s[g], acc_s[g] = update((m_s[g], l_s[g], acc_s[g]), tiles)
        return carry

    lax.fori_loop(0, (n_far + 3) // 4, far_body, 0)

    heads = []
    for g in groups:
        slc_near = near_tiles(ks_ref, vst_ref, g,
                              lambda kt, dist: jnp.where(dist >= 0, sel_mask(g, kt, kt >= 0), NEG))
        o_slc = normalised(update((m_s[g], l_s[g], acc_s[g]), slc_near))

        tiles = []
        for back in range(WIN // TQ, 1, -1):
            kt = i - back
            s, k0 = tile_scores(kw_ref, g, kt)
            if back == WIN // TQ:
                dist = tok - (kt * TQ + row)
                s = s + tile4(jnp.where((dist < WIN) & (kt >= 0), 0.0, NEG))
            else:
                s = s + jnp.where(kt >= 0, 0.0, NEG)
            tiles.append((s, vwt_ref[g, :, pl.ds(k0, TQ)]))
        win_near = near_tiles(kw_ref, vwt_ref, g, lambda kt, dist: jnp.where((dist >= 0) & (kt >= 0), 0.0, NEG))
        o_win = normalised(update(update(empty, tiles), win_near))

        for h in range(NSA_HG):
            cs = slice(h * TQ, (h + 1) * TQ)
            o_h = jnp.zeros((DH, TQ), F32)
            for br, o_br in enumerate((o_cmp[g], o_slc, o_win)):
                gate_row = br * NSA_HEADS + g * NSA_HG + h
                o_h = o_h + gtt_ref[gate_row:gate_row + 1, :] * o_br[:, cs]
            heads.append(o_h)
    o_ref[0] = jnp.transpose(jnp.concatenate(heads, axis=0))


def _nsa_prompt(qt, kc, vct, ksh, vst, kwh, vwt, gtt, bias_t, bias_wc, c2s_t, *, bsz, n_cmp):
    t = qt.shape[2] // bsz
    nq = t // TQ
    n_sel = t // SEL
    cols = NSA_HG * TQ
    k_spec = pl.BlockSpec((NSA_G, 1, t, DH), lambda b, i: (0, b, 0, 0))
    vt_spec = pl.BlockSpec((NSA_G, DH, t), lambda b, i: (0, 0, b))
    const = lambda arr: pl.BlockSpec(arr.shape, lambda b, i: (0,) * arr.ndim)
    return pl.pallas_call(
        functools.partial(_nsa_prompt_kernel, n_cmp=n_cmp, n_sel=n_sel, n_tiles=nq),
        out_shape=jax.ShapeDtypeStruct((bsz, t, D_MODEL), F32),
        grid=(bsz, nq),
        in_specs=[pl.BlockSpec((NSA_HEADS, DH, TQ), lambda b, i: (0, 0, b * nq + i)),
                  pl.BlockSpec((1, NSA_G, kc.shape[2], DH), lambda b, i: (b, 0, 0, 0)),
                  pl.BlockSpec((1, NSA_G, DH, vct.shape[3]), lambda b, i: (b, 0, 0, 0)),
                  k_spec, vt_spec, k_spec, vt_spec,
                  pl.BlockSpec((LANES, TQ), lambda b, i: (0, b * nq + i)),
                  const(bias_t), const(bias_wc), const(c2s_t)],
        out_specs=pl.BlockSpec((1, TQ, D_MODEL), lambda b, i: (b, i, 0)),
        scratch_shapes=[pltpu.VMEM((NSA_G, n_sel, TQ), F32), pltpu.VMEM((NSA_G, CMP_WIN_PAD + TQ, cols), F32),
                        pltpu.VMEM((NSA_G, 1, cols), F32), pltpu.VMEM((NSA_G, 1, cols), F32),
                        pltpu.VMEM((NSA_G, DH, cols), F32)],
        compiler_params=_cparams(("parallel", "arbitrary")),
        name="nsa_prompt",
    )(qt, kc, vct, ksh, vst, kwh, vwt, gtt, bias_t, bias_wc, c2s_t)


def _nsa_sample_kernel(*refs, pps, n_cmp, n_sel, n_sel_pad, past_len, t_new):
    k_pages = refs[1:1 + pps]
    v_pages = refs[1 + pps:1 + 2 * pps]
    (q_ref, kct_ref, vc_ref, kn_ref, vn_ref, kwt_ref, vwt_ref, kwn_ref, vwn_ref, gt_ref, bc_ref, bs_ref, bn_ref,
     bw_ref, c2s_ref, hsum_ref, hexp_ref, e_ref, o_ref,
     sel_s, score_s, m_s, l_s, acc_s, ocmp_s, owin_s) = refs[1 + 2 * pps:]
    ch = pl.program_id(1)
    nrow = NSA_HEADS * t_new
    q = q_ref[0]
    rtok = lax.broadcasted_iota(jnp.int32, (nrow, 1), 0) & (t_new - 1)
    qpos = past_len + rtok
    lane = lax.broadcasted_iota(jnp.int32, (1, LANES), 1)

    def to_col(row_vec):
        return jnp.transpose(jnp.broadcast_to(row_vec, (LANES, LANES)))[:nrow, 0:1]

    @pl.when(ch == 0)
    def _():
        c_end = lax.broadcasted_iota(jnp.int32, (1, n_cmp), 1) * CMP_STRIDE + CMP_BLOCK - 1
        s_c = _dot(q, kct_ref[0].astype(BF16)) + bc_ref[...]
        p_c = _masked_softmax(s_c, qpos - c_end >= 0, -1)
        ocmp_s[...] = _dot(p_c.astype(BF16), vc_ref[0].astype(BF16))

        p_ct = jnp.transpose(jnp.concatenate([p_c, jnp.zeros((LANES - nrow, n_cmp), F32)], axis=0))
        p_sum = _dot(p_ct, hsum_ref[...], precision=HIGHEST)
        imp = _dot(c2s_ref[...], p_sum, precision=HIGHEST)
        blk = lax.broadcasted_iota(jnp.int32, (n_sel_pad, 1), 0)
        cpos = past_len + (lane & (t_new - 1))
        cur = jnp.right_shift(cpos, SEL_SHIFT)
        forced = (blk == 0) | (blk == cur) | (blk == cur - 1)
        valid = blk * SEL <= cpos
        score = jnp.where(forced, FORCE, jnp.where(valid, imp, -FORCE))
        score = jnp.where(blk < n_sel, score, -3.0 * FORCE)
        score_s[...] = score

        def rank_body(b, cnt):
            s_b = score_s[pl.ds(b, 1), :]
            ahead = (s_b > score) | ((s_b == score) & (blk > b))
            return cnt + ahead.astype(jnp.int32)

        cnt = lax.fori_loop(0, n_sel, rank_body, jnp.zeros((n_sel_pad, LANES), jnp.int32),
                            unroll=3 if n_sel % 3 == 0 else 1)
        sel = jnp.where((cnt < min(TOPK, n_sel)) & (blk < n_sel), 1.0, 0.0).astype(BF16)
        sel_s[...] = _dot(sel, hexp_ref[...])

        key_w = past_len - WIN + lax.broadcasted_iota(jnp.int32, (1, WIN), 1)
        s_w = _dot(q, kwt_ref[0].astype(BF16)) + bw_ref[...]
        mask_w = lax.bitcast_convert_type(qpos - key_w, jnp.uint32) < WIN
        s_n = _dot_nt(q, kwn_ref[0].astype(BF16)) + bn_ref[...]
        mask_n = (lane < t_new) & (rtok - lane >= 0)
        s_w = jnp.where(mask_w, s_w, NEG)
        s_n = jnp.where(mask_n, s_n, NEG)
        m = jnp.maximum(jnp.max(s_w, axis=-1, keepdims=True), jnp.max(s_n, axis=-1, keepdims=True))
        e_w = jnp.where(mask_w, jnp.exp(s_w - m), 0.0)
        e_n = jnp.where(mask_n, jnp.exp(s_n - m), 0.0)
        den = jnp.maximum(jnp.sum(e_w, axis=-1, keepdims=True) + jnp.sum(e_n, axis=-1, keepdims=True), 1e-30)
        owin_s[...] = (_dot_nt((e_w / den).astype(BF16), vwt_ref[0].astype(BF16))
                       + _dot((e_n / den).astype(BF16), vwn_ref[0].astype(BF16)))

        s_t = _dot_nt(q, kn_ref[0].astype(BF16)) + bn_ref[...]
        mask_t = mask_n & (to_col(sel_s[pl.ds(past_len // SEL, 1), :]) > 0.5)
        s_t = jnp.where(mask_t, s_t, NEG)
        m0 = jnp.max(s_t, axis=-1, keepdims=True)
        p0 = jnp.where(mask_t, jnp.exp(s_t - m0), 0.0)
        m_s[...] = m0
        l_s[...] = jnp.sum(p0, axis=-1, keepdims=True)
        acc_s[...] = _dot(p0.astype(BF16), vn_ref[0].astype(BF16))

    kt_all = jnp.concatenate([r[0].reshape(KV_W, PAGE) for r in k_pages], axis=1).astype(BF16)
    vt_all = jnp.concatenate([r[0].reshape(KV_W, PAGE) for r in v_pages], axis=1).astype(BF16)
    s = _dot(q, kt_all) + bs_ref[0]
    blocks = pps * PAGE // SEL
    sel_blk = sel_s[pl.ds(pl.multiple_of(ch * blocks, blocks), blocks), :][:, :nrow].astype(BF16)
    mask = _dot_tn(sel_blk, e_ref[...]) > 0.5
    s = jnp.where(mask, s, NEG)
    m_old = m_s[...]
    m_new = jnp.maximum(m_old, jnp.max(s, axis=-1, keepdims=True))
    p = jnp.where(mask, jnp.exp(s - m_new), 0.0)
    alpha = jnp.exp(m_old - m_new)
    l_s[...] = alpha * l_s[...] + jnp.sum(p, axis=-1, keepdims=True)
    acc_s[...] = alpha * acc_s[...] + _dot_nt(p.astype(BF16), vt_all)
    m_s[...] = m_new

    @pl.when(ch == pl.num_programs(1) - 1)
    def _():
        o_slc = acc_s[...] / jnp.maximum(l_s[...], 1e-30)
        gates = gt_ref[0]
        hg_rows = NSA_HG * t_new
        for g in range(NSA_G):
            rs = slice(g * hg_rows, (g + 1) * hg_rows)
            cs = slice(g * DH, (g + 1) * DH)
            o_ref[0, rs, :] = (gates[rs, 0:1] * ocmp_s[rs, cs] + gates[rs, 1:2] * o_slc[rs, cs]
                               + gates[rs, 2:3] * owin_s[rs, cs])


def _nsa_sample(page_table, k_pool, v_pool, q_bd, kct, vc, k_new, v_new, kw_cache, vw_cache, kw_new, vw_new, gates,
                bias_c, bias_s, bias_n, bias_w, c2s_t, hsum, hexp, expand, *, pps, t_new):
    bsz, n_pages = page_table.shape
    past_len = n_pages * PAGE
    n_cmp = vc.shape[1]
    n_sel = past_len // SEL + 1
    n_sel_pad = c2s_t.shape[0]
    nrow = NSA_HEADS * t_new

    def page_spec(j):
        return pl.BlockSpec((1, NSA_G, DH, PAGE), lambda b, s, pt: (pt[b, s * pps + j], 0, 0, 0))

    per_b = lambda shape: pl.BlockSpec((1,) + shape, lambda b, s, pt: (b,) + (0,) * len(shape))
    const = lambda arr: pl.BlockSpec(arr.shape, lambda b, s, pt: (0,) * arr.ndim)
    in_specs = ([page_spec(j) for j in range(pps)] * 2
                + [per_b((nrow, KV_W)), per_b((KV_W, n_cmp)), per_b((n_cmp, KV_W)),
                   per_b((PAGE, KV_W)), per_b((PAGE, KV_W)), per_b((KV_W, WIN)), per_b((KV_W, WIN)),
                   per_b((PAGE, KV_W)), per_b((PAGE, KV_W)), per_b((nrow, 3)),
                   const(bias_c),
                   pl.BlockSpec((1, nrow, pps * PAGE), lambda b, s, pt: (s, 0, 0)),
                   const(bias_n), const(bias_w), const(c2s_t), const(hsum), const(hexp), const(expand)])
    grid_spec = pltpu.PrefetchScalarGridSpec(
        num_scalar_prefetch=1,
        grid=(bsz, n_pages // pps),
        in_specs=in_specs,
        out_specs=pl.BlockSpec((1, nrow, DH), lambda b, s, pt: (b, 0, 0)),
        scratch_shapes=[pltpu.VMEM((n_sel_pad, LANES), F32), pltpu.VMEM((n_sel_pad, LANES), F32),
                        pltpu.VMEM((nrow, 1), F32), pltpu.VMEM((nrow, 1), F32),
                        pltpu.VMEM((nrow, KV_W), F32), pltpu.VMEM((nrow, KV_W), F32),
                        pltpu.VMEM((nrow, KV_W), F32)],
    )
    return pl.pallas_call(
        functools.partial(_nsa_sample_kernel, pps=pps, n_cmp=n_cmp, n_sel=n_sel, n_sel_pad=n_sel_pad,
                          past_len=past_len, t_new=t_new),
        out_shape=jax.ShapeDtypeStruct((bsz, nrow, DH), F32),
        grid_spec=grid_spec,
        compiler_params=_cparams(("parallel", "arbitrary")),
        name="nsa_sample",
    )(page_table, *([k_pool] * pps), *([v_pool] * pps), q_bd, kct, vc, k_new, v_new, kw_cache, vw_cache, kw_new,
      vw_new, gates, bias_c, bias_s, bias_n, bias_w, c2s_t, hsum, hexp, expand)


def _post_kernel(x_ref, og_ref, on_ref, sma_ref, smb_ref, gate1_ref, sc2_ref, sh2_ref, gate2_ref, n2_ref, fg_ref,
                 wo_ref, wu_ref, wd_ref, y_ref):
    merged = sma_ref[...] * og_ref[...] + smb_ref[...] * on_ref[...]
    x1 = x_ref[...] + gate1_ref[0] * _dot(merged.astype(BF16), wo_ref[...])
    h2 = _rms(x1, n2_ref[...]) * (1.0 + sc2_ref[0]) + sh2_ref[0]
    up = jnp.maximum(_dot(h2.astype(BF16), wu_ref[...]), 0.0)
    x2 = x1 + gate2_ref[0] * _dot((up * up).astype(BF16), wd_ref[...])
    y_ref[...] = _rms(x2, fg_ref[...])


def _post(x, o_gla, o_nsa, sma, smb, gate1, scale2, shift2, gate2, n2, fg, wo, wu, wd, *, tm):
    m = x.shape[0]
    nb, r, _ = gate1.shape
    rows_per_mod = m // nb
    mod_spec = pl.BlockSpec((1, r, D_MODEL), lambda i: (i * tm // rows_per_mod, 0, 0))
    row = pl.BlockSpec((tm, D_MODEL), lambda i: (i, 0))
    return pl.pallas_call(
        _post_kernel,
        out_shape=jax.ShapeDtypeStruct((m, D_MODEL), F32),
        grid=(m // tm,),
        in_specs=[row] * 5 + [mod_spec] * 4 + [_const_spec((1, D_MODEL))] * 2
        + [_const_spec(wo.shape), _const_spec(wu.shape), _const_spec(wd.shape)],
        out_specs=row,
        compiler_params=_cparams(("parallel",)),
        name="post",
    )(x, o_gla, o_nsa, sma, smb, gate1, scale2, shift2, gate2, n2, fg, wo, wu, wd)


def _block_diag_w1(w1):
    w = w1.reshape(CMP_RATIO, CMP_STRIDE, DH, CMP_HID)
    eye = jnp.eye(NSA_G, dtype=w1.dtype)
    bd = jnp.einsum('redh,pg->epdgrh', w, eye)
    return bd.reshape(CMP_STRIDE * NSA_G * DH, NSA_G * CMP_RATIO * CMP_HID).astype(BF16)


def _tap_w1(w1):
    w = w1.reshape(CMP_RATIO, CMP_STRIDE // CMP_TAPS, CMP_TAPS, DH, CMP_HID)
    return jnp.transpose(w, (1, 2, 3, 0, 4)).reshape(CMP_STRIDE // CMP_TAPS, CMP_TAPS * DH,
                                                     CMP_RATIO * CMP_HID).astype(BF16)


def _cmp_to_sel_np(n_cmp, n_sel):
    cs = np.arange(n_cmp)[:, None] * CMP_STRIDE
    ss = np.arange(n_sel)[None, :] * SEL
    ov = np.minimum(cs + CMP_BLOCK, ss + SEL) - np.maximum(cs, ss)
    return np.clip(ov, 0, None).astype(np.float32) / CMP_BLOCK


def _bias_by_dist(rel_bias, n):
    return jnp.transpose(rel_bias[_bucket_np(np.arange(n))])


def _toeplitz(v, n):
    lead = v.shape[:-1]
    a = jnp.broadcast_to(v[..., None, :], lead + (n, 2 * n)).reshape(lead + (2 * n * n,))
    a = a[..., :n * (2 * n - 1)].reshape(lead + (n, 2 * n - 1))
    return a[..., n - 1:]


def _mod_split(mod, rows):
    parts = jnp.split(mod, 6, axis=-1)
    return [p.reshape(-1, rows, D_MODEL) for p in parts]


def kernel(x_prompt, x_sample, c_prompt, c_sample, state_gla, cache_k_cmp, cache_v_cmp, cache_k_slc, cache_v_slc, cache_k_win, cache_v_win, page_table, ada_w, ada_b, norm1_g, norm2_g, w_in, gla_a2, gla_a_b, gla_norm_g, cmp_pe_k, cmp_w1_k, cmp_w2_k, cmp_pe_v, cmp_w1_v, cmp_w2_v, w_o, w_up, w_down, rel_bias, final_g):
    depth = ada_w.shape[0]
    assert depth == 1, "single-layer trunk"
    bp, tp, _ = x_prompt.shape
    bs, ts, _ = x_sample.shape
    n_pages = page_table.shape[1]
    past_len = n_pages * PAGE
    mp, ms = bp * tp, bs * ts
    tm = 256
    tm_s = min(tm, ms)

    w = w_in[0]
    w_perm = jnp.concatenate([w[:, :3072], w[:, 3088:5648], w[:, 5696:7744], w[:, 5648:5696], w[:, 3072:3088],
                              jnp.zeros((D_MODEL, LANES - 64), w.dtype)], axis=1).astype(BF16)
    a2p = jnp.zeros((LANES, QK_W), F32).at[Z_GA:Z_GA + GLA_RANK].set(gla_a2[0]).astype(BF16)
    ab = gla_a_b[0].reshape(1, QK_W)
    g1 = norm1_g[0].reshape(1, D_MODEL)
    n2 = norm2_g[0].reshape(1, D_MODEL)
    fg = final_g.reshape(1, D_MODEL)
    gn = gla_norm_g[0].reshape(1, GLA_DV)
    wo, wu, wd = w_o[0].astype(BF16), w_up[0].astype(BF16), w_down[0].astype(BF16)
    cmp_k = (_block_diag_w1(cmp_w1_k[0]), jnp.broadcast_to(cmp_pe_k[0].reshape(1, -1), (8, CMP_BLOCK * DH)),
             cmp_w1_k[0].reshape(CMP_BLOCK * DH, CMP_HID), cmp_w2_k[0])
    cmp_v = (_block_diag_w1(cmp_w1_v[0]), jnp.broadcast_to(cmp_pe_v[0].reshape(1, -1), (8, CMP_BLOCK * DH)),
             cmp_w1_v[0].reshape(CMP_BLOCK * DH, CMP_HID), cmp_w2_v[0])

    mod = _ada(jnp.concatenate([c_prompt, c_sample], axis=0), ada_w[0], ada_b[0])
    sh1_p, sc1_p, gt1_p, sh2_p, sc2_p, gt2_p = _mod_split(mod[:bp], 1)
    sh1_s, sc1_s, gt1_s, sh2_s, sc2_s, gt2_s = _mod_split(jnp.repeat(mod[bp:], ts, axis=0), tm_s)

    xp = x_prompt.reshape(mp, D_MODEL)
    (gq, gk, gv, sgr, la, kc, vc, ks, vs, kw, vw, _, sma, smb, qt, ksh, vst, kwh, vwt, gtt) = _inproj(
        xp, sc1_p, sh1_p, g1, w_perm, a2p, ab, head_major=True, tm=tm)

    r3 = lambda a: a.reshape(bp, tp, a.shape[-1])
    o_gla_p, s_gla_p = _gla(r3(gq), r3(gk), r3(gv), r3(la), r3(sgr), gn, None, tb=256)

    n_seg = tp // CMP_STRIDE
    assert n_seg == LANES, "prompt attention keeps one compressed block per lane"
    n_cmp_p = n_seg - CMP_RATIO + 1
    seg = lambda a: a.reshape(mp // CMP_STRIDE, CMP_STRIDE * KV_W)
    tm_seg = min(tm, mp // CMP_STRIDE)
    kcc = _cmpfin(_mm(seg(kc), cmp_k[0], tm=tm_seg).reshape(bp, n_seg, -1), None, *cmp_k[1:], layout='rows')
    vcc = _cmpfin(_mm(seg(vc), cmp_v[0], tm=tm_seg).reshape(bp, n_seg, -1), None, *cmp_v[1:3],
                  jnp.transpose(cmp_v[3]), layout='lanes')

    n_sel_p = tp // SEL
    f = _bias_by_dist(rel_bias, 2 * TQ)
    v_same = jnp.concatenate([jnp.repeat(f[:, :1], TQ - 1, axis=1), f[:, :TQ + 1]], axis=1)
    v_prev = jnp.concatenate([f[:, 1:], f[:, -1:]], axis=1)
    tiles = _toeplitz(jnp.stack([v_same, v_prev]), TQ).reshape(2, NSA_G, NSA_HG, TQ, TQ)
    tiles = jnp.transpose(tiles, (0, 1, 3, 2, 4)).reshape(2, NSA_G, TQ, NSA_HG * TQ)
    far_cols = jnp.broadcast_to(f[:, -1].reshape(NSA_G, NSA_HG, 1), (NSA_G, NSA_HG, TQ)).reshape(NSA_G, 1, -1)
    bias_t = (tiles - far_cols[None]) * LOG2E
    d_wc = (np.arange(TQ)[None, :] - CMP_STRIDE * (np.arange(CMP_WIN_ROWS)[:, None] - CMP_WIN_PAD)
            - (CMP_BLOCK - 1))
    bias_wc = f[:, np.clip(d_wc, 0, 2 * TQ - 1)].reshape(NSA_G, NSA_HG, CMP_WIN_ROWS, TQ)
    bias_wc = jnp.transpose(bias_wc, (0, 2, 1, 3)).reshape(NSA_G, CMP_WIN_ROWS, NSA_HG * TQ)
    bias_wc = (bias_wc - far_cols) * LOG2E
    c2s_t = np.zeros((n_sel_p, n_seg), np.float32)
    c2s_t[:, :n_cmp_p] = _cmp_to_sel_np(n_cmp_p, n_sel_p).T
    k4 = lambda a: a.reshape(NSA_G, bp, tp, DH)
    o_nsa_p = _nsa_prompt(qt, kcc, vcc, k4(ksh), vst, k4(kwh), vwt, gtt, bias_t, bias_wc,
                          jnp.asarray(c2s_t), bsz=bp, n_cmp=n_cmp_p)

    y_p = _post(xp, o_gla_p.reshape(mp, V_W), o_nsa_p.reshape(mp, D_MODEL), sma, smb, gt1_p, sc2_p, sh2_p, gt2_p,
                n2, fg, wo, wu, wd, tm=tm)

    kv5 = lambda a, b, t: a.reshape(1, b, t, NSA_G, DH)
    wp = min(WIN, tp)
    p_out = (s_gla_p[None], kv5(kc, bp, tp), kv5(vc, bp, tp), kv5(ks, bp, tp), kv5(vs, bp, tp),
             kv5(kw, bp, tp)[:, :, tp - wp:], kv5(vw, bp, tp)[:, :, tp - wp:])

    xs = x_sample.reshape(ms, D_MODEL)
    (gq, gk, gv, sgr, la, kc, vc, ks, vs, kw, vw, gates, sma, smb, nq) = _inproj(
        xs, sc1_s, sh1_s, g1, w_perm, a2p, ab, head_major=False, tm=tm_s)

    t_pad = 8
    pad_t = lambda a: jnp.pad(a.reshape(bs, ts, a.shape[-1]), ((0, 0), (0, t_pad - ts), (0, 0)))
    o_gla_s, s_gla_s = _gla(pad_t(gq), pad_t(gk), pad_t(gv), pad_t(la), pad_t(sgr), gn, state_gla[0], tb=t_pad)
    o_gla_s = o_gla_s[:, :ts].reshape(ms, V_W)

    pool_t = lambda a: jnp.transpose(a[0], (0, 2, 3, 1))
    new_seg = lambda a: jnp.pad(a.reshape(bs, ts, KV_W), ((0, 0), (0, CMP_STRIDE - ts), (0, 0))).reshape(
        bs, CMP_STRIDE * KV_W)
    n_cmp_s = past_len // CMP_STRIDE
    pps_cmp = min(16, n_pages)
    kcc = _cmpfin(_cmp_paged(pool_t(cache_k_cmp), page_table, _tap_w1(cmp_w1_k[0]), pages_per_step=pps_cmp),
                  _mm(new_seg(kc), cmp_k[0], tm=bs).reshape(bs, 1, -1), *cmp_k[1:3], jnp.transpose(cmp_k[3]),
                  layout='lanes').reshape(bs, KV_W, n_cmp_s)
    vcc = _cmpfin(_cmp_paged(pool_t(cache_v_cmp), page_table, _tap_w1(cmp_w1_v[0]), pages_per_step=pps_cmp),
                  _mm(new_seg(vc), cmp_v[0], tm=bs).reshape(bs, 1, -1), *cmp_v[1:], layout='flat')

    q5 = nq.reshape(bs, ts, NSA_G, NSA_HG, DH)
    q_bd = jnp.einsum('btghd,pg->bghtpd', q5, jnp.eye(NSA_G, dtype=F32)).reshape(bs, NSA_HEADS * ts, KV_W)
    q_bd = q_bd.astype(BF16)

    ncol = NSA_HEADS * ts
    col_head = np.arange(ncol) // ts
    col_tok = np.arange(ncol) % ts

    def bias_rows(key_pos):
        dist = past_len + col_tok[:, None] - key_pos[None, :]
        near = np.nonzero(dist.min(axis=0) < REL_MAX)[0]
        lo = int(near.min()) if near.size else len(key_pos)
        far_part = jnp.broadcast_to(f[col_head, -1][:, None], (ncol, lo))
        near_part = f[col_head[:, None], np.clip(dist[:, lo:], 0, 2 * TQ - 1)]
        return jnp.concatenate([far_part, near_part], axis=1)

    pps = min(16, n_pages)
    bias_c_s = bias_rows(np.arange(n_cmp_s) * CMP_STRIDE + CMP_BLOCK - 1)
    bias_s_s = jnp.transpose(bias_rows(np.arange(past_len)).reshape(ncol, n_pages // pps, pps * PAGE), (1, 0, 2))
    bias_n_s = bias_rows(past_len + np.arange(PAGE))
    bias_w_s = bias_rows(past_len - WIN + np.arange(WIN))
    n_sel_s = past_len // SEL + 1
    n_sel_pad = -(-n_sel_s // 8) * 8
    c2s_t = np.zeros((n_sel_pad, n_cmp_s), np.float32)
    c2s_t[:n_sel_s] = _cmp_to_sel_np(n_cmp_s, n_sel_s).T
    gt_col = np.arange(LANES) // (NSA_HG * ts) * ts + np.arange(LANES) % ts
    hsum = ((gt_col[:, None] == np.arange(LANES)[None, :]) & (np.arange(LANES)[:, None] < ncol)).astype(np.float32)
    expand_s = (np.arange(pps * PAGE)[None, :] // SEL == np.arange(pps * PAGE // SEL)[:, None]).astype(np.float32)
    gates_s = jnp.transpose(gates[:, :3 * NSA_HEADS].reshape(bs, ts, 3, NSA_HEADS), (0, 3, 1, 2)).reshape(bs, ncol, 3)
    new_rows = lambda a: jnp.pad(a.reshape(bs, ts, KV_W), ((0, 0), (0, PAGE - ts), (0, 0)))
    wb = cache_k_win.shape[2]
    win_t = lambda a: pool_t(a).reshape(bs, KV_W, wb)
    o_s = _nsa_sample(page_table, pool_t(cache_k_slc), pool_t(cache_v_slc), q_bd, kcc, vcc, new_rows(ks),
                      new_rows(vs), win_t(cache_k_win), win_t(cache_v_win), new_rows(kw),
                      new_rows(vw), gates_s, bias_c_s, bias_s_s, bias_n_s, bias_w_s, jnp.asarray(c2s_t),
                      jnp.asarray(hsum), jnp.asarray(hsum.T, dtype=BF16), jnp.asarray(expand_s, dtype=BF16),
                      pps=pps, t_new=ts)
    o_nsa_s = jnp.transpose(o_s.reshape(bs, NSA_HEADS, ts, DH), (0, 2, 1, 3)).reshape(ms, D_MODEL)

    y_s = _post(xs, o_gla_s, o_nsa_s, sma, smb, gt1_s, sc2_s, sh2_s, gt2_s, n2, fg, wo, wu, wd, tm=tm_s)

    kw_new = jnp.concatenate([cache_k_win[0], kv5(kw, bs, ts)[0]], axis=1)[:, ts:][None]
    vw_new = jnp.concatenate([cache_v_win[0], kv5(vw, bs, ts)[0]], axis=1)[:, ts:][None]
    s_out = (s_gla_s[None], kv5(kc, bs, ts), kv5(vc, bs, ts), kv5(ks, bs, ts), kv5(vs, bs, ts), kw_new, vw_new)

    return (y_p.reshape(bp, tp, D_MODEL), y_s.reshape(bs, ts, D_MODEL)) + p_out + s_out
```

```python
import functools
import math

import numpy as np
import jax
import jax.numpy as jnp
from jax import lax
from jax.experimental import pallas as pl
from jax.experimental.pallas import tpu as pltpu

F32 = jnp.float32
BF16 = jnp.bfloat16
HIGHEST = lax.Precision.HIGHEST

D_MODEL = 1024
GLA_HEADS = 4
GLA_DK = 128
GLA_DV = 256
GLA_RANK = 16
GLA_TAU = 16.0
GLA_CHUNK = 64
GLA_SUB = 16
NSA_HEADS = 16
DH = 64
NSA_G = 4
NSA_HG = 4
CMP_BLOCK = 32
CMP_STRIDE = 16
CMP_RATIO = 2
CMP_HID = 128
SEL = 64
SEL_SHIFT = 6
TOPK = 16
WIN = 512
REL_BUCKETS = 32
REL_MAX = 128
D_FF = 4096
EPS = 1e-6
NEG = -1e30
LOG2E = math.log2(math.e)
FORCE = 1e4
PAGE = 128

QK_W = GLA_HEADS * GLA_DK
V_W = GLA_HEADS * GLA_DV
KV_W = NSA_G * DH
LANES = 128
TQ = 128
VMEM_LIMIT = 56 * 1024 * 1024

C_GQ, C_GK, C_GV, C_GR, C_NQ = 0, 512, 1024, 2048, 3072
C_KV = 4096
C_MA, C_MB, C_Z = 5632, 6656, 7680
W_COLS = 7808
Z_GA = 48


def _cparams(sem):
    return pltpu.CompilerParams(dimension_semantics=sem, vmem_limit_bytes=VMEM_LIMIT)


def _const_spec(shape):
    return pl.BlockSpec(shape, lambda *a: (0,) * len(shape), pipeline_mode=pl.Buffered(1))


def _sigmoid(x):
    return 1.0 / (1.0 + jnp.exp(-x))


def _rms(x, g):
    return x * lax.rsqrt(jnp.mean(x * x, axis=-1, keepdims=True) + EPS) * g


def _dot(a, b, **kw):
    return jnp.dot(a, b, preferred_element_type=F32, **kw)


def _dot_nt(a, b):
    return lax.dot_general(a, b, (((1,), (1,)), ((), ())), preferred_element_type=F32)


def _dot_tn(a, b):
    return lax.dot_general(a, b, (((0,), (0,)), ((), ())), preferred_element_type=F32)


def _bucket_np(dist):
    n = np.maximum(dist, 0)
    nf = np.maximum(n, 1).astype(np.float64)
    large = 16 + (np.log(nf / 16.0) / math.log(REL_MAX / 16.0) * 16.0).astype(np.int64)
    large = np.minimum(large, REL_BUCKETS - 1)
    return np.where(n < 16, n, large).astype(np.int32)


def _ada_kernel(c_ref, w_ref, b_ref, o_ref):
    c = c_ref[...]
    o_ref[...] = _dot(c * _sigmoid(c), w_ref[...], precision=HIGHEST) + b_ref[...]


def _ada(c, w, b):
    r, d = c.shape
    n = w.shape[1]
    return pl.pallas_call(
        _ada_kernel,
        out_shape=jax.ShapeDtypeStruct((r, n), F32),
        grid=(n // d,),
        in_specs=[pl.BlockSpec((r, d), lambda j: (0, 0)),
                  pl.BlockSpec((d, d), lambda j: (0, j)),
                  pl.BlockSpec((1, d), lambda j: (0, j))],
        out_specs=pl.BlockSpec((r, d), lambda j: (0, j)),
        compiler_params=_cparams(("arbitrary",)),
        name="ada",
    )(c, w, b.reshape(1, n))


def _inproj_kernel(x_ref, sc_ref, sh_ref, g_ref, w_ref, a2_ref, ab_ref, *outs, head_major):
    (gq_o, gk_o, gv_o, sgr_o, la_o, kc_o, vc_o, ks_o, vs_o, kw_o, vw_o, gt_o, sma_o, smb_o, *rest) = outs
    h = _rms(x_ref[...], g_ref[...]) * (1.0 + sc_ref[0]) + sh_ref[0]
    hb = h.astype(BF16)

    def proj(lo, hi):
        return _dot(hb, w_ref[:, lo:hi])

    gq_o[...] = proj(C_GQ, C_GK) * (GLA_DK ** -0.5)
    gk_o[...] = proj(C_GK, C_GV)
    gv_o[...] = proj(C_GV, C_GR)
    gr = proj(C_GR, C_NQ)
    sgr_o[...] = gr * _sigmoid(gr)
    nq = proj(C_NQ, C_KV) * (DH ** -0.5)
    kv = proj(C_KV, C_MA)
    for n, o in enumerate((kc_o, vc_o, ks_o, vs_o, kw_o, vw_o)):
        o[...] = kv[:, n * KV_W:(n + 1) * KV_W]
    sma_o[...] = _sigmoid(proj(C_MA, C_MB))
    smb_o[...] = _sigmoid(proj(C_MB, C_Z))
    z = proj(C_Z, W_COLS)
    gt_o[...] = _sigmoid(z)
    pre = _dot(z.astype(BF16), a2_ref[...]) + ab_ref[...]
    la_o[...] = (jnp.minimum(pre, 0.0) - jnp.log(1.0 + jnp.exp(-jnp.abs(pre)))) * (1.0 / GLA_TAU)
    if head_major:
        qt_o, ksh_o, vst_o, kwh_o, vwt_o, gtt_o, kct_o, vct_o, kst_o, vstf_o = rest

        def put_transposed(x, *outs):
            for c in range(x.shape[1] // LANES):
                t = jnp.transpose(x[:, c * LANES:(c + 1) * LANES])
                for o, lead in outs:
                    o[lead + (2 * c,)] = t[:DH].astype(o.dtype)
                    o[lead + (2 * c + 1,)] = t[DH:].astype(o.dtype)

        put_transposed(nq * LOG2E, (qt_o, ()))
        put_transposed(kv[:, 0 * KV_W:1 * KV_W], (kct_o, (0,)))
        put_transposed(kv[:, 1 * KV_W:2 * KV_W], (vct_o, (0,)))
        put_transposed(kv[:, 2 * KV_W:3 * KV_W], (kst_o, (0,)))
        put_transposed(kv[:, 3 * KV_W:4 * KV_W], (vst_o, ()), (vstf_o, (0,)))
        put_transposed(kv[:, 5 * KV_W:6 * KV_W], (vwt_o, ()))
        for n, o in ((2, ksh_o), (4, kwh_o)):
            for gg in range(NSA_G):
                lo = n * KV_W + gg * DH
                o[gg] = kv[:, lo:lo + DH].astype(BF16)
        gtt_o[...] = jnp.transpose(_sigmoid(z))
    else:
        (nq_o,) = rest
        nq_o[...] = nq


def _inproj(x, scale, shift, g1, w, a2p, ab, *, head_major, tm, seq=None):
    m = x.shape[0]
    nb, r, _ = scale.shape
    rows_per_mod = m // nb
    mod_spec = pl.BlockSpec((1, r, D_MODEL), lambda i: (i * tm // rows_per_mod, 0, 0))
    row = lambda wdt: pl.BlockSpec((tm, wdt), lambda i: (i, 0))
    widths = [QK_W, QK_W, V_W, V_W, QK_W] + [KV_W] * 6 + [LANES, D_MODEL, D_MODEL]
    out_shape = [jax.ShapeDtypeStruct((m, wd), F32) for wd in widths]
    out_specs = [row(wd) for wd in widths]
    if head_major:
        def rows_major(n):
            out_shape.append(jax.ShapeDtypeStruct((n, m, DH), BF16))
            out_specs.append(pl.BlockSpec((n, tm, DH), lambda i: (0, i, 0)))

        def lanes_major(n):
            out_shape.append(jax.ShapeDtypeStruct((n, DH, m), BF16))
            out_specs.append(pl.BlockSpec((n, DH, tm), lambda i: (0, 0, i)))

        lanes_major(NSA_HEADS)
        rows_major(NSA_G)
        lanes_major(NSA_G)
        rows_major(NSA_G)
        lanes_major(NSA_G)
        out_shape.append(jax.ShapeDtypeStruct((LANES, m), F32))
        out_specs.append(pl.BlockSpec((LANES, tm), lambda i: (0, i)))
        tiles_per_seq = seq // tm
        for _ in range(4):
            out_shape.append(jax.ShapeDtypeStruct((m // seq, NSA_G, DH, seq), F32))
            out_specs.append(pl.BlockSpec((1, NSA_G, DH, tm),
                                          lambda i: (i // tiles_per_seq, 0, 0, i % tiles_per_seq)))
    else:
        out_shape.append(jax.ShapeDtypeStruct((m, D_MODEL), F32))
        out_specs.append(row(D_MODEL))
    return pl.pallas_call(
        functools.partial(_inproj_kernel, head_major=head_major),
        out_shape=out_shape,
        grid=(m // tm,),
        in_specs=[row(D_MODEL), mod_spec, mod_spec, _const_spec((1, D_MODEL)), _const_spec((D_MODEL, W_COLS)),
                  _const_spec((LANES, QK_W)), _const_spec((1, QK_W))],
        out_specs=out_specs,
        compiler_params=_cparams(("parallel",)),
        name="inproj",
    )(x, scale, shift, g1, w, a2p, ab)


def _gla_kernel(*refs, chunk, sub, n_chunk, has_init):
    if has_init:
        q_ref, k_ref, v_ref, la_ref, sgr_ref, gn_ref, s0_ref, o_ref, so_ref, s_scr = refs
    else:
        q_ref, k_ref, v_ref, la_ref, sgr_ref, gn_ref, o_ref, so_ref, s_scr = refs
    t = pl.program_id(1)

    @pl.when(t == 0)
    def _():
        s_scr[...] = s0_ref[0] if has_init else jnp.zeros_like(s_scr)

    tril = (lax.broadcasted_iota(jnp.int32, (chunk, chunk), 0)
            >= lax.broadcasted_iota(jnp.int32, (chunk, chunk), 1)).astype(F32)
    sub_row = lax.broadcasted_iota(jnp.int32, (sub, 1), 0)

    def one_chunk(c, carry):
        r0 = pl.multiple_of(c * chunk, chunk)
        rows = pl.ds(r0, chunk)
        b_all = _dot(tril, la_ref[0, rows, :], precision=HIGHEST)
        for h in range(GLA_HEADS):
            one_head(h, rows, b_all[:, h * GLA_DK:(h + 1) * GLA_DK])
        return carry

    def one_head(h, rows, b):
        ks, vs = slice(h * GLA_DK, (h + 1) * GLA_DK), slice(h * GLA_DV, (h + 1) * GLA_DV)
        q = q_ref[0, rows, ks]
        k = k_ref[0, rows, ks]
        v = v_ref[0, rows, vs]
        s_old = s_scr[h]
        vb = v.astype(BF16)
        o_inter = _dot((q * jnp.exp(b)).astype(BF16), s_old.astype(BF16))
        blocks = []
        for blk in range(chunk // sub):
            sl = slice(blk * sub, (blk + 1) * sub)
            q_i, k_i, v_i, b_i = q[sl], k[sl], v[sl], b[sl]
            acc = jnp.zeros((sub, GLA_DV), F32)
            for j in range(sub):
                w = q_i * k_i[j:j + 1] * jnp.exp(jnp.minimum(b_i - b_i[j:j + 1], 0.0))
                a = jnp.sum(w, axis=-1, keepdims=True)
                acc = acc + jnp.where(sub_row >= j, a, 0.0) * v_i[j:j + 1]
            if blk > 0:
                n = blk * sub
                ref_row = b_i[0:1]
                qp = (q_i * jnp.exp(b_i - ref_row)).astype(BF16)
                kp = (k[:n] * jnp.exp(ref_row - b[:n])).astype(BF16)
                acc = acc + _dot(_dot_nt(qp, kp).astype(BF16), vb[:n])
            blocks.append(acc)
        o = o_inter + (jnp.concatenate(blocks, axis=0) if len(blocks) > 1 else blocks[0])
        b_last = b[chunk - 1:chunk]
        kp = (k * jnp.exp(b_last - b)).astype(BF16)
        decay_col = jnp.transpose(jnp.broadcast_to(jnp.exp(b_last), (GLA_DK, GLA_DK)))[:, 0:1]
        s_scr[h] = decay_col * s_old + _dot_tn(kp, vb)
        o_ref[0, rows, vs] = _rms(o, gn_ref[...]) * sgr_ref[0, rows, vs]

    lax.fori_loop(0, n_chunk, one_chunk, 0)

    @pl.when(t == pl.num_programs(1) - 1)
    def _():
        so_ref[0] = s_scr[...]


def _gla(q, k, v, la, sgr, gn, s0, *, tb):
    bsz, t, _ = q.shape
    chunk = min(GLA_CHUNK, tb)
    sub = min(GLA_SUB, chunk)
    blk = lambda wd: pl.BlockSpec((1, tb, wd), lambda b, i: (b, i, 0))
    st_spec = pl.BlockSpec((1, GLA_HEADS, GLA_DK, GLA_DV), lambda b, i: (b, 0, 0, 0))
    in_specs = [blk(QK_W), blk(QK_W), blk(V_W), blk(QK_W), blk(V_W),
                pl.BlockSpec((1, GLA_DV), lambda b, i: (0, 0))]
    args = [q, k, v, la, sgr, gn]
    if s0 is not None:
        in_specs.append(st_spec)
        args.append(s0)
    return pl.pallas_call(
        functools.partial(_gla_kernel, chunk=chunk, sub=sub, n_chunk=tb // chunk, has_init=s0 is not None),
        out_shape=[jax.ShapeDtypeStruct((bsz, t, V_W), F32),
                   jax.ShapeDtypeStruct((bsz, GLA_HEADS, GLA_DK, GLA_DV), F32)],
        grid=(bsz, t // tb),
        in_specs=in_specs,
        out_specs=[blk(V_W), st_spec],
        scratch_shapes=[pltpu.VMEM((GLA_HEADS, GLA_DK, GLA_DV), F32)],
        compiler_params=_cparams(("parallel", "arbitrary")),
        name="gla",
    )(*args)


def _mm_kernel(x_ref, w_ref, o_ref):
    o_ref[...] = _dot(x_ref[...].astype(BF16), w_ref[...])


def _mm(x, w, *, tm):
    m, kdim = x.shape
    n = w.shape[1]
    return pl.pallas_call(
        _mm_kernel,
        out_shape=jax.ShapeDtypeStruct((m, n), F32),
        grid=(m // tm,),
        in_specs=[pl.BlockSpec((tm, kdim), lambda i: (i, 0)), _const_spec((kdim, n))],
        out_specs=pl.BlockSpec((tm, n), lambda i: (i, 0)),
        compiler_params=_cparams(("parallel",)),
        name="cmp_mm",
    )(x, w)


CMP_TAPS = 4


def _cmp_paged_kernel(*refs, pps):
    pages = refs[1:1 + pps]
    w_ref, o_ref, tok_s = refs[1 + pps:]
    heads_per_slab = LANES // DH
    n_slab = NSA_G // heads_per_slab
    for p in range(pps):
        t = jnp.transpose(pages[p][0].reshape(KV_W, PAGE))
        for sl in range(n_slab):
            tok_s[sl, p * PAGE:(p + 1) * PAGE, :] = t[:, sl * LANES:(sl + 1) * LANES]
    n_seg = pps * PAGE // CMP_STRIDE
    width = CMP_RATIO * CMP_HID
    first_half = lax.broadcasted_iota(jnp.int32, (1, LANES), 1) < DH
    for sl in range(n_slab):
        x = [tok_s[sl, pl.ds(e, n_seg, stride=CMP_STRIDE), :] for e in range(CMP_STRIDE)]
        r = [pltpu.roll(v, DH, 1) for v in x]
        for odd in range(heads_per_slab):
            g = sl * heads_per_slab + odd
            acc = None
            for j in range(CMP_STRIDE // CMP_TAPS):
                pieces = []
                for e in range(j * CMP_TAPS, (j + 1) * CMP_TAPS, 2):
                    lo, hi = (r[e], x[e + 1]) if odd else (x[e], r[e + 1])
                    pieces.append(jnp.where(first_half, lo, hi))
                term = _dot(jnp.concatenate(pieces, axis=1).astype(BF16), w_ref[j])
                acc = term if acc is None else acc + term
            o_ref[0, :, g * width:(g + 1) * width] = acc


def _cmp_paged(pool_t, page_table, w_taps, *, pages_per_step):
    bsz, n_pages = page_table.shape
    pps = pages_per_step
    n_seg = pps * PAGE // CMP_STRIDE
    width = NSA_G * CMP_RATIO * CMP_HID

    def page_spec(j):
        return pl.BlockSpec((1, NSA_G, DH, PAGE), lambda b, s, pt: (pt[b, s * pps + j], 0, 0, 0))

    grid_spec = pltpu.PrefetchScalarGridSpec(
        num_scalar_prefetch=1,
        grid=(bsz, n_pages // pps),
        in_specs=[page_spec(j) for j in range(pps)]
        + [pl.BlockSpec(w_taps.shape, lambda b, s, pt: (0, 0, 0), pipeline_mode=pl.Buffered(1))],
        out_specs=pl.BlockSpec((1, n_seg, width), lambda b, s, pt: (b, s, 0)),
        scratch_shapes=[pltpu.VMEM((NSA_G * DH // LANES, pps * PAGE, LANES), F32)],
    )
    return pl.pallas_call(
        functools.partial(_cmp_paged_kernel, pps=pps),
        out_shape=jax.ShapeDtypeStruct((bsz, n_pages * PAGE // CMP_STRIDE, width), F32),
        grid_spec=grid_spec,
        compiler_params=_cparams(("parallel", "arbitrary")),
        name="cmp_paged",
    )(page_table, *([pool_t] * pps), w_taps)


def _cmpfin_kernel(*refs, n, has_new, layout):
    if has_new:
        p_ref, pn_ref, pe_ref, w1_ref, w2_ref, o_ref = refs
    else:
        p_ref, pe_ref, w1_ref, w2_ref, o_ref = refs
    pe_term = _dot(pe_ref[...], w1_ref[...], precision=HIGHEST)[0:1]
    w2 = w2_ref[...].astype(BF16)
    rowi = lax.broadcasted_iota(jnp.int32, (n, 1), 0)
    for g in range(NSA_G):
        base = g * CMP_RATIO * CMP_HID
        first = p_ref[0, :, base:base + CMP_HID]
        second = pltpu.roll(p_ref[0, :, base + CMP_HID:base + 2 * CMP_HID], n - 1, 0)
        if has_new:
            second = jnp.where(rowi == n - 1, pn_ref[0, :, base + CMP_HID:base + 2 * CMP_HID], second)
        acc = first + second + pe_term
        hid = (acc * _sigmoid(acc)).astype(BF16)
        if layout == 'flat':
            o_ref[0, :, g * DH:(g + 1) * DH] = _dot(hid, w2)
        elif layout == 'rows':
            o_ref[0, g] = _dot(hid, w2)
        else:
            o_ref[0, g] = _dot_nt(w2, hid)


def _cmpfin(parts, parts_new, pe, w1, w2, *, layout):
    bsz, n, width = parts.shape
    in_specs = [pl.BlockSpec((1, n, width), lambda b: (b, 0, 0))]
    args = [parts]
    if parts_new is not None:
        in_specs.append(pl.BlockSpec((1, 1, width), lambda b: (b, 0, 0)))
        args.append(parts_new)
    in_specs += [_const_spec(pe.shape), _const_spec(w1.shape), _const_spec(w2.shape)]
    if layout == 'flat':
        out_shape = jax.ShapeDtypeStruct((bsz, n, KV_W), F32)
        out_spec = pl.BlockSpec((1, n, KV_W), lambda b: (b, 0, 0))
    else:
        inner = (n, DH) if layout == 'rows' else (DH, n)
        out_shape = jax.ShapeDtypeStruct((bsz, NSA_G) + inner, F32)
        out_spec = pl.BlockSpec((1, NSA_G) + inner, lambda b: (b, 0, 0, 0))
    return pl.pallas_call(
        functools.partial(_cmpfin_kernel, n=n, has_new=parts_new is not None, layout=layout),
        out_shape=out_shape,
        grid=(bsz,),
        in_specs=in_specs,
        out_specs=out_spec,
        compiler_params=_cparams(("parallel",)),
        name="cmp_finish",
    )(*args, pe, w1, w2)


def _masked_softmax(s, mask, axis, exp=jnp.exp):
    s = jnp.where(mask, s, NEG)
    m = jnp.max(s, axis=axis, keepdims=True)
    e = jnp.where(mask, exp(s - m), 0.0)
    return e / jnp.maximum(jnp.sum(e, axis=axis, keepdims=True), 1e-30)


CMP_WIN_ROWS = 24
CMP_WIN_PAD = 16


def _nsa_prompt_kernel(qt_ref, kc_ref, vct_ref, ks_ref, vst_ref, kw_ref, vwt_ref, gtt_ref, bt_ref, wc_ref,
                       c2s_ref, o_ref, sel_s, bc_s, m_s, l_s, acc_s, *, n_cmp, n_sel, n_tiles):
    i = pl.program_id(1)
    tok = i * TQ + lax.broadcasted_iota(jnp.int32, (1, TQ), 1)
    tok4 = jnp.concatenate([tok] * NSA_HG, axis=1)
    row = lax.broadcasted_iota(jnp.int32, (TQ, 1), 0)
    groups = range(NSA_G)
    q4 = [jnp.concatenate([qt_ref[g * NSA_HG + h] for h in range(NSA_HG)], axis=1) for g in groups]

    def tile4(x):
        return jnp.concatenate([x] * NSA_HG, axis=1)

    o_cmp = []
    for g in groups:
        bc_s[g] = jnp.zeros(bc_s.shape[1:], F32)
        bc_s[g, pl.ds(pl.multiple_of(i * (TQ // CMP_STRIDE), TQ // CMP_STRIDE), CMP_WIN_ROWS), :] = wc_ref[g]
        s_c = _dot(kc_ref[0, g].astype(BF16), q4[g]) + bc_s[g, CMP_WIN_PAD:CMP_WIN_PAD + TQ, :]
        mask_c = (tok4 - (row * CMP_STRIDE + CMP_BLOCK - 1) >= 0) & (row < n_cmp)
        p_c = _masked_softmax(s_c, mask_c, 0, exp=jnp.exp2)
        o_cmp.append(_dot(vct_ref[0, g].astype(BF16), p_c.astype(BF16)))
        p_sum = p_c[:, 0:TQ] + p_c[:, TQ:2 * TQ] + p_c[:, 2 * TQ:3 * TQ] + p_c[:, 3 * TQ:4 * TQ]
        imp = _dot(c2s_ref[...], p_sum, precision=HIGHEST)
        blk = lax.broadcasted_iota(jnp.int32, (n_sel, 1), 0)
        cur = jnp.right_shift(tok, SEL_SHIFT)
        forced = (blk == 0) | (blk == cur) | (blk == cur - 1)
        valid = blk * SEL <= tok
        score = jnp.where(forced, FORCE, jnp.where(valid, imp, -FORCE))
        cnt = jnp.zeros((n_sel, TQ), jnp.int32)
        for b in range(n_sel):
            s_b = score[b:b + 1]
            ahead = (s_b > score) | ((s_b == score) & (blk > b))
            cnt = cnt + ahead.astype(jnp.int32)
        sel_s[g] = jnp.where(cnt < min(TOPK, n_sel), 1.0, 0.0)

    def tile_scores(k_ref, g, kt):
        k0 = pl.multiple_of(jnp.clip(kt, 0, n_tiles - 1) * TQ, TQ)
        return _dot(k_ref[g, 0, pl.ds(k0, TQ), :], q4[g]), k0

    def sel_mask(g, kt, valid):
        blocks_per_tile = TQ // SEL
        first = jnp.clip(kt, 0, n_tiles - 1) * blocks_per_tile
        out = None
        for j in reversed(range(blocks_per_tile)):
            add_j = jnp.where((sel_s[g, pl.ds(first + j, 1), :] > 0.5) & valid, 0.0, NEG)
            out = add_j if out is None else jnp.where(row < (j + 1) * SEL, add_j, out)
        return out

    def update(state, tiles):
        m_run, l_run, acc = state
        m_new = m_run
        for s, _ in tiles:
            m_new = jnp.maximum(m_new, jnp.max(s, axis=0, keepdims=True))
        alpha = jnp.exp2(m_run - m_new)
        l_new = alpha * l_run
        acc = alpha * acc
        for s, vt in tiles:
            p = jnp.exp2(s - m_new)
            l_new = l_new + jnp.sum(p, axis=0, keepdims=True)
            acc = acc + _dot(vt, p.astype(BF16))
        return m_new, l_new, acc

    def near_tiles(k_ref, vt_ref, g, mask_fn):
        tiles = []
        for typ, kt in ((1, i - 1), (0, i)):
            s, k0 = tile_scores(k_ref, g, kt)
            madd = mask_fn(kt, tok - (kt * TQ + row))
            tiles.append((s + bt_ref[typ, g] + tile4(madd), vt_ref[g, :, pl.ds(k0, TQ)]))
        return tiles

    cols = NSA_HG * TQ
    empty = (jnp.full((1, cols), NEG, F32), jnp.zeros((1, cols), F32), jnp.zeros((DH, cols), F32))

    def normalised(state):
        return state[2] / jnp.maximum(state[1], 1e-30)

    for g in groups:
        m_s[g], l_s[g], acc_s[g] = empty
    n_far = jnp.maximum(i - 1, 0)

    def far_body(c, carry):
        for g in groups:
            tiles = []
            for u in range(4):
                kt = 4 * c + u
                s, k0 = tile_scores(ks_ref, g, kt)
                tiles.append((s + tile4(sel_mask(g, kt, kt < n_far)), vst_ref[g, :, pl.ds(k0, TQ)]))
            m_s[g], l_s[g], acc_s[g] = update((m_s[g], l_s[g], acc_s[g]), tiles)
        return carry

    lax.fori_loop(0, (n_far + 3) // 4, far_body, 0)

    heads = []
    for g in groups:
        slc_near = near_tiles(ks_ref, vst_ref, g,
                              lambda kt, dist: jnp.where(dist >= 0, sel_mask(g, kt, kt >= 0), NEG))
        o_slc = normalised(update((m_s[g], l_s[g], acc_s[g]), slc_near))

        tiles = []
        for back in range(WIN // TQ, 1, -1):
            kt = i - back
            s, k0 = tile_scores(kw_ref, g, kt)
            if back == WIN // TQ:
                dist = tok - (kt * TQ + row)
                s = s + tile4(jnp.where((dist < WIN) & (kt >= 0), 0.0, NEG))
            else:
                s = s + jnp.where(kt >= 0, 0.0, NEG)
            tiles.append((s, vwt_ref[g, :, pl.ds(k0, TQ)]))
        win_near = near_tiles(kw_ref, vwt_ref, g, lambda kt, dist: jnp.where((dist >= 0) & (kt >= 0), 0.0, NEG))
        o_win = normalised(update(update(empty, tiles), win_near))

        for h in range(NSA_HG):
            cs = slice(h * TQ, (h + 1) * TQ)
            o_h = jnp.zeros((DH, TQ), F32)
            for br, o_br in enumerate((o_cmp[g], o_slc, o_win)):
                gate_row = br * NSA_HEADS + g * NSA_HG + h
                o_h = o_h + gtt_ref[gate_row:gate_row + 1, :] * o_br[:, cs]
            heads.append(o_h)
    o_ref[0] = jnp.transpose(jnp.concatenate(heads, axis=0))


def _nsa_prompt(qt, kc, vct, ksh, vst, kwh, vwt, gtt, bias_t, bias_wc, c2s_t, *, bsz, n_cmp):
    t = qt.shape[2] // bsz
    nq = t // TQ
    n_sel = t // SEL
    cols = NSA_HG * TQ
    k_spec = pl.BlockSpec((NSA_G, 1, t, DH), lambda b, i: (0, b, 0, 0))
    vt_spec = pl.BlockSpec((NSA_G, DH, t), lambda b, i: (0, 0, b))
    const = lambda arr: pl.BlockSpec(arr.shape, lambda b, i: (0,) * arr.ndim)
    return pl.pallas_call(
        functools.partial(_nsa_prompt_kernel, n_cmp=n_cmp, n_sel=n_sel, n_tiles=nq),
        out_shape=jax.ShapeDtypeStruct((bsz, t, D_MODEL), F32),
        grid=(bsz, nq),
        in_specs=[pl.BlockSpec((NSA_HEADS, DH, TQ), lambda b, i: (0, 0, b * nq + i)),
                  pl.BlockSpec((1, NSA_G, kc.shape[2], DH), lambda b, i: (b, 0, 0, 0)),
                  pl.BlockSpec((1, NSA_G, DH, vct.shape[3]), lambda b, i: (b, 0, 0, 0)),
                  k_spec, vt_spec, k_spec, vt_spec,
                  pl.BlockSpec((LANES, TQ), lambda b, i: (0, b * nq + i)),
                  const(bias_t), const(bias_wc), const(c2s_t)],
        out_specs=pl.BlockSpec((1, TQ, D_MODEL), lambda b, i: (b, i, 0)),
        scratch_shapes=[pltpu.VMEM((NSA_G, n_sel, TQ), F32), pltpu.VMEM((NSA_G, CMP_WIN_PAD + TQ, cols), F32),
                        pltpu.VMEM((NSA_G, 1, cols), F32), pltpu.VMEM((NSA_G, 1, cols), F32),
                        pltpu.VMEM((NSA_G, DH, cols), F32)],
        compiler_params=_cparams(("parallel", "arbitrary")),
        name="nsa_prompt",
    )(qt, kc, vct, ksh, vst, kwh, vwt, gtt, bias_t, bias_wc, c2s_t)


def _nsa_sample_kernel(*refs, pps, n_cmp, n_sel, n_sel_pad, past_len, t_new):
    k_pages = refs[1:1 + pps]
    v_pages = refs[1 + pps:1 + 2 * pps]
    (q_ref, kct_ref, vc_ref, kn_ref, vn_ref, kwt_ref, vwt_ref, kwn_ref, vwn_ref, gt_ref, bc_ref, bs_ref, bn_ref,
     bw_ref, c2s_ref, hsum_ref, hexp_ref, e_ref, o_ref,
     sel_s, m_s, l_s, acc_s, ocmp_s, owin_s) = refs[1 + 2 * pps:]
    ch = pl.program_id(1)
    nrow = NSA_HEADS * t_new
    q = q_ref[0]
    rtok = lax.broadcasted_iota(jnp.int32, (nrow, 1), 0) & (t_new - 1)
    qpos = past_len + rtok
    lane = lax.broadcasted_iota(jnp.int32, (1, LANES), 1)

    def to_col(row_vec):
        return jnp.transpose(jnp.broadcast_to(row_vec, (LANES, LANES)))[:nrow, 0:1]

    @pl.when(ch == 0)
    def _():
        c_end = lax.broadcasted_iota(jnp.int32, (1, n_cmp), 1) * CMP_STRIDE + CMP_BLOCK - 1
        s_c = _dot(q, kct_ref[0].astype(BF16)) + bc_ref[...]
        p_c = _masked_softmax(s_c, qpos - c_end >= 0, -1)
        ocmp_s[...] = _dot(p_c.astype(BF16), vc_ref[0].astype(BF16))

        n_gt = NSA_G * t_new
        p_sum = _dot(hsum_ref[...], jnp.concatenate([p_c, jnp.zeros((LANES - nrow, n_cmp), F32)], axis=0),
                     precision=HIGHEST)[:n_gt]
        imp = _dot(p_sum, c2s_ref[...], precision=HIGHEST)
        n_blk_lanes = imp.shape[1]
        blk = lax.broadcasted_iota(jnp.int32, (1, n_blk_lanes), 1)
        gpos = past_len + (lax.broadcasted_iota(jnp.int32, (n_gt, 1), 0) & (t_new - 1))
        cur = jnp.right_shift(gpos, SEL_SHIFT)
        forced = (blk == 0) | (blk == cur) | (blk == cur - 1)
        valid = blk * SEL <= gpos
        score = jnp.where(forced, FORCE, jnp.where(valid, imp, -FORCE))
        score = jnp.where(blk < n_sel, score, -3.0 * FORCE)
        cnt = jnp.zeros((n_gt, n_blk_lanes), jnp.int32)
        for b in range(n_sel):
            s_b = score[:, b:b + 1]
            ahead = (s_b > score) | ((s_b == score) & (blk > b))
            cnt = cnt + ahead.astype(jnp.int32)
        sel = jnp.where((cnt < min(TOPK, n_sel)) & (blk < n_sel), 1.0, 0.0)
        sel_t = jnp.transpose(jnp.concatenate([sel, jnp.zeros((LANES - n_gt, n_blk_lanes), F32)], axis=0))
        sel_s[...] = _dot(sel_t[:n_sel_pad].astype(BF16), hexp_ref[...])

        key_w = past_len - WIN + lax.broadcasted_iota(jnp.int32, (1, WIN), 1)
        s_w = _dot(q, kwt_ref[0].astype(BF16)) + bw_ref[...]
        mask_w = lax.bitcast_convert_type(qpos - key_w, jnp.uint32) < WIN
        s_n = _dot_nt(q, kwn_ref[0].astype(BF16)) + bn_ref[...]
        mask_n = (lane < t_new) & (rtok - lane >= 0)
        s_w = jnp.where(mask_w, s_w, NEG)
        s_n = jnp.where(mask_n, s_n, NEG)
        m = jnp.maximum(jnp.max(s_w, axis=-1, keepdims=True), jnp.max(s_n, axis=-1, keepdims=True))
        e_w = jnp.where(mask_w, jnp.exp(s_w - m), 0.0)
        e_n = jnp.where(mask_n, jnp.exp(s_n - m), 0.0)
        den = jnp.maximum(jnp.sum(e_w, axis=-1, keepdims=True) + jnp.sum(e_n, axis=-1, keepdims=True), 1e-30)
        owin_s[...] = (_dot_nt((e_w / den).astype(BF16), vwt_ref[0].astype(BF16))
                       + _dot((e_n / den).astype(BF16), vwn_ref[0].astype(BF16)))

        s_t = _dot_nt(q, kn_ref[0].astype(BF16)) + bn_ref[...]
        mask_t = mask_n & (to_col(sel_s[pl.ds(past_len // SEL, 1), :]) > 0.5)
        s_t = jnp.where(mask_t, s_t, NEG)
        m0 = jnp.max(s_t, axis=-1, keepdims=True)
        p0 = jnp.where(mask_t, jnp.exp(s_t - m0), 0.0)
        m_s[...] = m0
        l_s[...] = jnp.sum(p0, axis=-1, keepdims=True)
        acc_s[...] = _dot(p0.astype(BF16), vn_ref[0].astype(BF16))

    kt_all = jnp.concatenate([r[0].reshape(KV_W, PAGE) for r in k_pages], axis=1).astype(BF16)
    vt_all = jnp.concatenate([r[0].reshape(KV_W, PAGE) for r in v_pages], axis=1).astype(BF16)
    s = _dot(q, kt_all) + bs_ref[0]
    blocks = pps * PAGE // SEL
    sel_blk = sel_s[pl.ds(pl.multiple_of(ch * blocks, blocks), blocks), :][:, :nrow].astype(BF16)
    mask = _dot_tn(sel_blk, e_ref[...]) > 0.5
    s = jnp.where(mask, s, NEG)
    m_old = m_s[...]
    m_new = jnp.maximum(m_old, jnp.max(s, axis=-1, keepdims=True))
    p = jnp.where(mask, jnp.exp(s - m_new), 0.0)
    alpha = jnp.exp(m_old - m_new)
    l_s[...] = alpha * l_s[...] + jnp.sum(p, axis=-1, keepdims=True)
    acc_s[...] = alpha * acc_s[...] + _dot_nt(p.astype(BF16), vt_all)
    m_s[...] = m_new

    @pl.when(ch == pl.num_programs(1) - 1)
    def _():
        o_slc = acc_s[...] / jnp.maximum(l_s[...], 1e-30)
        gates = gt_ref[0]
        hg_rows = NSA_HG * t_new
        for g in range(NSA_G):
            rs = slice(g * hg_rows, (g + 1) * hg_rows)
            cs = slice(g * DH, (g + 1) * DH)
            o_ref[0, rs, :] = (gates[rs, 0:1] * ocmp_s[rs, cs] + gates[rs, 1:2] * o_slc[rs, cs]
                               + gates[rs, 2:3] * owin_s[rs, cs])


def _nsa_sample(page_table, k_pool, v_pool, q_bd, kct, vc, k_new, v_new, kw_cache, vw_cache, kw_new, vw_new, gates,
                bias_c, bias_s, bias_n, bias_w, c2s, hsum_t, hexp, expand, *, pps, t_new):
    bsz, n_pages = page_table.shape
    past_len = n_pages * PAGE
    n_cmp = vc.shape[1]
    n_sel = past_len // SEL + 1
    n_sel_pad = -(-n_sel // 8) * 8
    nrow = NSA_HEADS * t_new

    def page_spec(j):
        return pl.BlockSpec((1, NSA_G, DH, PAGE), lambda b, s, pt: (pt[b, s * pps + j], 0, 0, 0))

    per_b = lambda shape: pl.BlockSpec((1,) + shape, lambda b, s, pt: (b,) + (0,) * len(shape))
    const = lambda arr: pl.BlockSpec(arr.shape, lambda b, s, pt: (0,) * arr.ndim)
    in_specs = ([page_spec(j) for j in range(pps)] * 2
                + [per_b((nrow, KV_W)), per_b((KV_W, n_cmp)), per_b((n_cmp, KV_W)),
                   per_b((PAGE, KV_W)), per_b((PAGE, KV_W)), per_b((KV_W, WIN)), per_b((KV_W, WIN)),
                   per_b((PAGE, KV_W)), per_b((PAGE, KV_W)), per_b((nrow, 3)),
                   const(bias_c),
                   pl.BlockSpec((1, nrow, pps * PAGE), lambda b, s, pt: (s, 0, 0)),
                   const(bias_n), const(bias_w), const(c2s), const(hsum_t), const(hexp), const(expand)])
    grid_spec = pltpu.PrefetchScalarGridSpec(
        num_scalar_prefetch=1,
        grid=(bsz, n_pages // pps),
        in_specs=in_specs,
        out_specs=pl.BlockSpec((1, nrow, DH), lambda b, s, pt: (b, 0, 0)),
        scratch_shapes=[pltpu.VMEM((n_sel_pad, LANES), F32),
                        pltpu.VMEM((nrow, 1), F32), pltpu.VMEM((nrow, 1), F32),
                        pltpu.VMEM((nrow, KV_W), F32), pltpu.VMEM((nrow, KV_W), F32),
                        pltpu.VMEM((nrow, KV_W), F32)],
    )
    return pl.pallas_call(
        functools.partial(_nsa_sample_kernel, pps=pps, n_cmp=n_cmp, n_sel=n_sel, n_sel_pad=n_sel_pad,
                          past_len=past_len, t_new=t_new),
        out_shape=jax.ShapeDtypeStruct((bsz, nrow, DH), F32),
        grid_spec=grid_spec,
        compiler_params=_cparams(("parallel", "arbitrary")),
        name="nsa_sample",
    )(page_table, *([k_pool] * pps), *([v_pool] * pps), q_bd, kct, vc, k_new, v_new, kw_cache, vw_cache, kw_new,
      vw_new, gates, bias_c, bias_s, bias_n, bias_w, c2s, hsum_t, hexp, expand)


def _post_kernel(x_ref, og_ref, on_ref, sma_ref, smb_ref, gate1_ref, sc2_ref, sh2_ref, gate2_ref, n2_ref, fg_ref,
                 wo_ref, wu_ref, wd_ref, y_ref):
    merged = sma_ref[...] * og_ref[...] + smb_ref[...] * on_ref[...]
    x1 = x_ref[...] + gate1_ref[0] * _dot(merged.astype(BF16), wo_ref[...])
    h2 = _rms(x1, n2_ref[...]) * (1.0 + sc2_ref[0]) + sh2_ref[0]
    up = jnp.maximum(_dot(h2.astype(BF16), wu_ref[...]), 0.0)
    x2 = x1 + gate2_ref[0] * _dot((up * up).astype(BF16), wd_ref[...])
    y_ref[...] = _rms(x2, fg_ref[...])


def _post(x, o_gla, o_nsa, sma, smb, gate1, scale2, shift2, gate2, n2, fg, wo, wu, wd, *, tm):
    m = x.shape[0]
    nb, r, _ = gate1.shape
    rows_per_mod = m // nb
    mod_spec = pl.BlockSpec((1, r, D_MODEL), lambda i: (i * tm // rows_per_mod, 0, 0))
    row = pl.BlockSpec((tm, D_MODEL), lambda i: (i, 0))
    return pl.pallas_call(
        _post_kernel,
        out_shape=jax.ShapeDtypeStruct((m, D_MODEL), F32),
        grid=(m // tm,),
        in_specs=[row] * 5 + [mod_spec] * 4 + [_const_spec((1, D_MODEL))] * 2
        + [_const_spec(wo.shape), _const_spec(wu.shape), _const_spec(wd.shape)],
        out_specs=row,
        compiler_params=_cparams(("parallel",)),
        name="post",
    )(x, o_gla, o_nsa, sma, smb, gate1, scale2, shift2, gate2, n2, fg, wo, wu, wd)


def _block_diag_w1(w1):
    w = w1.reshape(CMP_RATIO, CMP_STRIDE, DH, CMP_HID)
    eye = jnp.eye(NSA_G, dtype=w1.dtype)
    bd = jnp.einsum('redh,pg->epdgrh', w, eye)
    return bd.reshape(CMP_STRIDE * NSA_G * DH, NSA_G * CMP_RATIO * CMP_HID).astype(BF16)


def _tap_w1(w1):
    w = w1.reshape(CMP_RATIO, CMP_STRIDE // CMP_TAPS, CMP_TAPS, DH, CMP_HID)
    return jnp.transpose(w, (1, 2, 3, 0, 4)).reshape(CMP_STRIDE // CMP_TAPS, CMP_TAPS * DH,
                                                     CMP_RATIO * CMP_HID).astype(BF16)


def _cmp_to_sel_np(n_cmp, n_sel):
    cs = np.arange(n_cmp)[:, None] * CMP_STRIDE
    ss = np.arange(n_sel)[None, :] * SEL
    ov = np.minimum(cs + CMP_BLOCK, ss + SEL) - np.maximum(cs, ss)
    return np.clip(ov, 0, None).astype(np.float32) / CMP_BLOCK


def _bias_by_dist(rel_bias, n):
    return jnp.transpose(rel_bias[_bucket_np(np.arange(n))])


def _toeplitz(v, n):
    lead = v.shape[:-1]
    a = jnp.broadcast_to(v[..., None, :], lead + (n, 2 * n)).reshape(lead + (2 * n * n,))
    a = a[..., :n * (2 * n - 1)].reshape(lead + (n, 2 * n - 1))
    return a[..., n - 1:]


def _mod_split(mod, rows):
    parts = jnp.split(mod, 6, axis=-1)
    return [p.reshape(-1, rows, D_MODEL) for p in parts]


def kernel(x_prompt, x_sample, c_prompt, c_sample, state_gla, cache_k_cmp, cache_v_cmp, cache_k_slc, cache_v_slc, cache_k_win, cache_v_win, page_table, ada_w, ada_b, norm1_g, norm2_g, w_in, gla_a2, gla_a_b, gla_norm_g, cmp_pe_k, cmp_w1_k, cmp_w2_k, cmp_pe_v, cmp_w1_v, cmp_w2_v, w_o, w_up, w_down, rel_bias, final_g):
    depth = ada_w.shape[0]
    assert depth == 1, "single-layer trunk"
    bp, tp, _ = x_prompt.shape
    bs, ts, _ = x_sample.shape
    n_pages = page_table.shape[1]
    past_len = n_pages * PAGE
    mp, ms = bp * tp, bs * ts
    tm = 256
    tm_s = min(tm, ms)

    w = w_in[0]
    w_perm = jnp.concatenate([w[:, :3072], w[:, 3088:5648], w[:, 5696:7744], w[:, 5648:5696], w[:, 3072:3088],
                              jnp.zeros((D_MODEL, LANES - 64), w.dtype)], axis=1).astype(BF16)
    a2p = jnp.zeros((LANES, QK_W), F32).at[Z_GA:Z_GA + GLA_RANK].set(gla_a2[0]).astype(BF16)
    ab = gla_a_b[0].reshape(1, QK_W)
    g1 = norm1_g[0].reshape(1, D_MODEL)
    n2 = norm2_g[0].reshape(1, D_MODEL)
    fg = final_g.reshape(1, D_MODEL)
    gn = gla_norm_g[0].reshape(1, GLA_DV)
    wo, wu, wd = w_o[0].astype(BF16), w_up[0].astype(BF16), w_down[0].astype(BF16)
    cmp_k = (_block_diag_w1(cmp_w1_k[0]), jnp.broadcast_to(cmp_pe_k[0].reshape(1, -1), (8, CMP_BLOCK * DH)),
             cmp_w1_k[0].reshape(CMP_BLOCK * DH, CMP_HID), cmp_w2_k[0])
    cmp_v = (_block_diag_w1(cmp_w1_v[0]), jnp.broadcast_to(cmp_pe_v[0].reshape(1, -1), (8, CMP_BLOCK * DH)),
             cmp_w1_v[0].reshape(CMP_BLOCK * DH, CMP_HID), cmp_w2_v[0])

    mod = _ada(jnp.concatenate([c_prompt, c_sample], axis=0), ada_w[0], ada_b[0])
    sh1_p, sc1_p, gt1_p, sh2_p, sc2_p, gt2_p = _mod_split(mod[:bp], 1)
    sh1_s, sc1_s, gt1_s, sh2_s, sc2_s, gt2_s = _mod_split(jnp.repeat(mod[bp:], ts, axis=0), tm_s)

    xp = x_prompt.reshape(mp, D_MODEL)
    (gq, gk, gv, sgr, la, kc, vc, _, _, kw, vw, _, sma, smb, qt, ksh, vst, kwh, vwt, gtt,
     kc_t, vc_t, ks_t, vs_t) = _inproj(xp, sc1_p, sh1_p, g1, w_perm, a2p, ab, head_major=True, tm=tm, seq=tp)

    r3 = lambda a: a.reshape(bp, tp, a.shape[-1])
    o_gla_p, s_gla_p = _gla(r3(gq), r3(gk), r3(gv), r3(la), r3(sgr), gn, None, tb=256)

    n_seg = tp // CMP_STRIDE
    assert n_seg == LANES, "prompt attention keeps one compressed block per lane"
    n_cmp_p = n_seg - CMP_RATIO + 1
    seg = lambda a: a.reshape(mp // CMP_STRIDE, CMP_STRIDE * KV_W)
    tm_seg = min(tm, mp // CMP_STRIDE)
    kcc = _cmpfin(_mm(seg(kc), cmp_k[0], tm=tm_seg).reshape(bp, n_seg, -1), None, *cmp_k[1:], layout='rows')
    vcc = _cmpfin(_mm(seg(vc), cmp_v[0], tm=tm_seg).reshape(bp, n_seg, -1), None, *cmp_v[1:3],
                  jnp.transpose(cmp_v[3]), layout='lanes')

    n_sel_p = tp // SEL
    f = _bias_by_dist(rel_bias, 2 * TQ)
    v_same = jnp.concatenate([jnp.repeat(f[:, :1], TQ - 1, axis=1), f[:, :TQ + 1]], axis=1)
    v_prev = jnp.concatenate([f[:, 1:], f[:, -1:]], axis=1)
    tiles = _toeplitz(jnp.stack([v_same, v_prev]), TQ).reshape(2, NSA_G, NSA_HG, TQ, TQ)
    tiles = jnp.transpose(tiles, (0, 1, 3, 2, 4)).reshape(2, NSA_G, TQ, NSA_HG * TQ)
    far_cols = jnp.broadcast_to(f[:, -1].reshape(NSA_G, NSA_HG, 1), (NSA_G, NSA_HG, TQ)).reshape(NSA_G, 1, -1)
    bias_t = (tiles - far_cols[None]) * LOG2E
    d0 = [-CMP_STRIDE * (r - CMP_WIN_PAD) - (CMP_BLOCK - 1) for r in range(CMP_WIN_ROWS)]
    lo_pad, hi_pad = max(0, -min(d0)), max(0, max(d0) + TQ - 2 * TQ)
    f_pad = jnp.concatenate([jnp.repeat(f[:, :1], lo_pad, axis=1), f, jnp.repeat(f[:, -1:], hi_pad, axis=1)], axis=1)
    bias_wc = jnp.stack([f_pad[:, d + lo_pad:d + lo_pad + TQ] for d in d0], axis=1)
    bias_wc = bias_wc.reshape(NSA_G, NSA_HG, CMP_WIN_ROWS, TQ)
    bias_wc = jnp.transpose(bias_wc, (0, 2, 1, 3)).reshape(NSA_G, CMP_WIN_ROWS, NSA_HG * TQ)
    bias_wc = (bias_wc - far_cols) * LOG2E
    c2s_t = np.zeros((n_sel_p, n_seg), np.float32)
    c2s_t[:, :n_cmp_p] = _cmp_to_sel_np(n_cmp_p, n_sel_p).T
    k4 = lambda a: a.reshape(NSA_G, bp, tp, DH)
    o_nsa_p = _nsa_prompt(qt, kcc, vcc, k4(ksh), vst, k4(kwh), vwt, gtt, bias_t, bias_wc,
                          jnp.asarray(c2s_t), bsz=bp, n_cmp=n_cmp_p)

    y_p = _post(xp, o_gla_p.reshape(mp, V_W), o_nsa_p.reshape(mp, D_MODEL), sma, smb, gt1_p, sc2_p, sh2_p, gt2_p,
                n2, fg, wo, wu, wd, tm=tm)

    kv5 = lambda a, b, t: a.reshape(1, b, t, NSA_G, DH)
    wp = min(WIN, tp)
    rows5 = lambda a: jnp.transpose(a, (0, 3, 1, 2))[None]
    p_out = (s_gla_p[None], rows5(kc_t), rows5(vc_t), rows5(ks_t), rows5(vs_t),
             kv5(kw, bp, tp)[:, :, tp - wp:], kv5(vw, bp, tp)[:, :, tp - wp:])

    xs = x_sample.reshape(ms, D_MODEL)
    (gq, gk, gv, sgr, la, kc, vc, ks, vs, kw, vw, gates, sma, smb, nq) = _inproj(
        xs, sc1_s, sh1_s, g1, w_perm, a2p, ab, head_major=False, tm=tm_s)

    t_pad = 8
    pad_t = lambda a: jnp.pad(a.reshape(bs, ts, a.shape[-1]), ((0, 0), (0, t_pad - ts), (0, 0)))
    o_gla_s, s_gla_s = _gla(pad_t(gq), pad_t(gk), pad_t(gv), pad_t(la), pad_t(sgr), gn, state_gla[0], tb=t_pad)
    o_gla_s = o_gla_s[:, :ts].reshape(ms, V_W)

    pool_t = lambda a: jnp.transpose(a[0], (0, 2, 3, 1))
    new_seg = lambda a: jnp.pad(a.reshape(bs, ts, KV_W), ((0, 0), (0, CMP_STRIDE - ts), (0, 0))).reshape(
        bs, CMP_STRIDE * KV_W)
    n_cmp_s = past_len // CMP_STRIDE
    pps_cmp = min(32, n_pages)
    kcc = _cmpfin(_cmp_paged(pool_t(cache_k_cmp), page_table, _tap_w1(cmp_w1_k[0]), pages_per_step=pps_cmp),
                  _mm(new_seg(kc), cmp_k[0], tm=bs).reshape(bs, 1, -1), *cmp_k[1:3], jnp.transpose(cmp_k[3]),
                  layout='lanes').reshape(bs, KV_W, n_cmp_s)
    vcc = _cmpfin(_cmp_paged(pool_t(cache_v_cmp), page_table, _tap_w1(cmp_w1_v[0]), pages_per_step=pps_cmp),
                  _mm(new_seg(vc), cmp_v[0], tm=bs).reshape(bs, 1, -1), *cmp_v[1:], layout='flat')

    q5 = nq.reshape(bs, ts, NSA_G, NSA_HG, DH)
    q_bd = jnp.einsum('btghd,pg->bghtpd', q5, jnp.eye(NSA_G, dtype=F32)).reshape(bs, NSA_HEADS * ts, KV_W)
    q_bd = q_bd.astype(BF16)

    ncol = NSA_HEADS * ts
    col_head = np.arange(ncol) // ts
    col_tok = np.arange(ncol) % ts

    def bias_rows(key_pos):
        dist = past_len + col_tok[:, None] - key_pos[None, :]
        near = np.nonzero(dist.min(axis=0) < REL_MAX)[0]
        lo = int(near.min()) if near.size else len(key_pos)
        far_part = jnp.broadcast_to(f[col_head, -1][:, None], (ncol, lo))
        near_part = f[col_head[:, None], np.clip(dist[:, lo:], 0, 2 * TQ - 1)]
        return jnp.concatenate([far_part, near_part], axis=1)

    pps = min(16, n_pages)
    bias_c_s = bias_rows(np.arange(n_cmp_s) * CMP_STRIDE + CMP_BLOCK - 1)
    bias_s_s = jnp.transpose(bias_rows(np.arange(past_len)).reshape(ncol, n_pages // pps, pps * PAGE), (1, 0, 2))
    bias_n_s = bias_rows(past_len + np.arange(PAGE))
    bias_w_s = bias_rows(past_len - WIN + np.arange(WIN))
    n_sel_s = past_len // SEL + 1
    c2s_s = np.zeros((n_cmp_s, -(-n_sel_s // LANES) * LANES), np.float32)
    c2s_s[:, :n_sel_s] = _cmp_to_sel_np(n_cmp_s, n_sel_s)
    gt_col = np.arange(LANES) // (NSA_HG * ts) * ts + np.arange(LANES) % ts
    hsum = ((gt_col[:, None] == np.arange(LANES)[None, :]) & (np.arange(LANES)[:, None] < ncol)).astype(np.float32)
    expand_s = (np.arange(pps * PAGE)[None, :] // SEL == np.arange(pps * PAGE // SEL)[:, None]).astype(np.float32)
    gates_s = jnp.transpose(gates[:, :3 * NSA_HEADS].reshape(bs, ts, 3, NSA_HEADS), (0, 3, 1, 2)).reshape(bs, ncol, 3)
    new_rows = lambda a: jnp.pad(a.reshape(bs, ts, KV_W), ((0, 0), (0, PAGE - ts), (0, 0)))
    wb = cache_k_win.shape[2]
    win_t = lambda a: pool_t(a).reshape(bs, KV_W, wb)
    o_s = _nsa_sample(page_table, pool_t(cache_k_slc), pool_t(cache_v_slc), q_bd, kcc, vcc, new_rows(ks),
                      new_rows(vs), win_t(cache_k_win), win_t(cache_v_win), new_rows(kw),
                      new_rows(vw), gates_s, bias_c_s, bias_s_s, bias_n_s, bias_w_s, jnp.asarray(c2s_s),
                      jnp.asarray(hsum.T), jnp.asarray(hsum.T, dtype=BF16), jnp.asarray(expand_s, dtype=BF16),
                      pps=pps, t_new=ts)
    o_nsa_s = jnp.transpose(o_s.reshape(bs, NSA_HEADS, ts, DH), (0, 2, 1, 3)).reshape(ms, D_MODEL)

    y_s = _post(xs, o_gla_s, o_nsa_s, sma, smb, gt1_s, sc2_s, sh2_s, gt2_s, n2, fg, wo, wu, wd, tm=tm_s)

    kw_new = jnp.concatenate([cache_k_win[0], kv5(kw, bs, ts)[0]], axis=1)[:, ts:][None]
    vw_new = jnp.concatenate([cache_v_win[0], kv5(vw, bs, ts)[0]], axis=1)[:, ts:][None]
    s_out = (s_gla_s[None], kv5(kc, bs, ts), kv5(vc, bs, ts), kv5(ks, bs, ts), kv5(vs, bs, ts), kw_new, vw_new)

    return (y_p.reshape(bp, tp, D_MODEL), y_s.reshape(bs, ts, D_MODEL)) + p_out + s_out
```

```python
import functools
import math

import numpy as np
import jax
import jax.numpy as jnp
from jax import lax
from jax.experimental import pallas as pl
from jax.experimental.pallas import tpu as pltpu

F32 = jnp.float32
BF16 = jnp.bfloat16
HIGHEST = lax.Precision.HIGHEST

D_MODEL = 1024
GLA_HEADS = 4
GLA_DK = 128
GLA_DV = 256
GLA_RANK = 16
GLA_TAU = 16.0
GLA_CHUNK = 64
GLA_SUB = 16
NSA_HEADS = 16
DH = 64
NSA_G = 4
NSA_HG = 4
CMP_BLOCK = 32
CMP_STRIDE = 16
CMP_RATIO = 2
CMP_HID = 128
SEL = 64
SEL_SHIFT = 6
TOPK = 16
WIN = 512
REL_BUCKETS = 32
REL_MAX = 128
D_FF = 4096
EPS = 1e-6
NEG = -1e30
LOG2E = math.log2(math.e)
FORCE = 1e4
PAGE = 128

QK_W = GLA_HEADS * GLA_DK
V_W = GLA_HEADS * GLA_DV
KV_W = NSA_G * DH
LANES = 128
TQ = 128
VMEM_LIMIT = 56 * 1024 * 1024

C_GQ, C_GK, C_GV, C_GR, C_NQ = 0, 512, 1024, 2048, 3072
C_KV = 4096
C_MA, C_MB, C_Z = 5632, 6656, 7680
W_COLS = 7808
Z_GA = 48


def _cparams(sem):
    return pltpu.CompilerParams(dimension_semantics=sem, vmem_limit_bytes=VMEM_LIMIT)


def _const_spec(shape):
    return pl.BlockSpec(shape, lambda *a: (0,) * len(shape), pipeline_mode=pl.Buffered(1))


def _sigmoid(x):
    return 1.0 / (1.0 + jnp.exp(-x))


def _rms(x, g):
    return x * lax.rsqrt(jnp.mean(x * x, axis=-1, keepdims=True) + EPS) * g


def _dot(a, b, **kw):
    return jnp.dot(a, b, preferred_element_type=F32, **kw)


def _dot_nt(a, b):
    return lax.dot_general(a, b, (((1,), (1,)), ((), ())), preferred_element_type=F32)


def _dot_tn(a, b):
    return lax.dot_general(a, b, (((0,), (0,)), ((), ())), preferred_element_type=F32)


def _bucket_np(dist):
    n = np.maximum(dist, 0)
    nf = np.maximum(n, 1).astype(np.float64)
    large = 16 + (np.log(nf / 16.0) / math.log(REL_MAX / 16.0) * 16.0).astype(np.int64)
    large = np.minimum(large, REL_BUCKETS - 1)
    return np.where(n < 16, n, large).astype(np.int32)


def _ada_kernel(c_ref, w_ref, b_ref, o_ref):
    c = c_ref[...]
    o_ref[...] = _dot(c * _sigmoid(c), w_ref[...], precision=HIGHEST) + b_ref[...]


def _ada(c, w, b):
    r, d = c.shape
    n = w.shape[1]
    return pl.pallas_call(
        _ada_kernel,
        out_shape=jax.ShapeDtypeStruct((r, n), F32),
        grid=(n // d,),
        in_specs=[pl.BlockSpec((r, d), lambda j: (0, 0)),
                  pl.BlockSpec((d, d), lambda j: (0, j)),
                  pl.BlockSpec((1, d), lambda j: (0, j))],
        out_specs=pl.BlockSpec((r, d), lambda j: (0, j)),
        compiler_params=_cparams(("arbitrary",)),
        name="ada",
    )(c, w, b.reshape(1, n))


def _inproj_kernel(x_ref, sc_ref, sh_ref, g_ref, w_ref, a2_ref, ab_ref, *outs, head_major):
    (gq_o, gk_o, gv_o, sgr_o, la_o, kc_o, vc_o, ks_o, vs_o, kw_o, vw_o, gt_o, sma_o, smb_o, *rest) = outs
    h = _rms(x_ref[...], g_ref[...]) * (1.0 + sc_ref[0]) + sh_ref[0]
    hb = h.astype(BF16)

    def proj(lo, hi):
        return _dot(hb, w_ref[:, lo:hi])

    gq_o[...] = proj(C_GQ, C_GK) * (GLA_DK ** -0.5)
    gk_o[...] = proj(C_GK, C_GV)
    gv_o[...] = proj(C_GV, C_GR)
    gr = proj(C_GR, C_NQ)
    sgr_o[...] = gr * _sigmoid(gr)
    nq = proj(C_NQ, C_KV) * (DH ** -0.5)
    kv = proj(C_KV, C_MA)
    for n, o in enumerate((kc_o, vc_o, ks_o, vs_o, kw_o, vw_o)):
        o[...] = kv[:, n * KV_W:(n + 1) * KV_W]
    sma_o[...] = _sigmoid(proj(C_MA, C_MB))
    smb_o[...] = _sigmoid(proj(C_MB, C_Z))
    z = proj(C_Z, W_COLS)
    gt_o[...] = _sigmoid(z)
    pre = _dot(z.astype(BF16), a2_ref[...]) + ab_ref[...]
    la_o[...] = (jnp.minimum(pre, 0.0) - jnp.log(1.0 + jnp.exp(-jnp.abs(pre)))) * (1.0 / GLA_TAU)
    if head_major:
        qt_o, ksh_o, vst_o, kwh_o, vwt_o, gtt_o, kct_o, vct_o, kst_o, vstf_o = rest

        def put_transposed(x, *outs):
            for c in range(x.shape[1] // LANES):
                t = jnp.transpose(x[:, c * LANES:(c + 1) * LANES])
                for o, lead in outs:
                    o[lead + (2 * c,)] = t[:DH].astype(o.dtype)
                    o[lead + (2 * c + 1,)] = t[DH:].astype(o.dtype)

        put_transposed(nq * LOG2E, (qt_o, ()))
        put_transposed(kv[:, 0 * KV_W:1 * KV_W], (kct_o, (0,)))
        put_transposed(kv[:, 1 * KV_W:2 * KV_W], (vct_o, (0,)))
        put_transposed(kv[:, 2 * KV_W:3 * KV_W], (kst_o, (0,)))
        put_transposed(kv[:, 3 * KV_W:4 * KV_W], (vst_o, ()), (vstf_o, (0,)))
        put_transposed(kv[:, 5 * KV_W:6 * KV_W], (vwt_o, ()))
        for n, o in ((2, ksh_o), (4, kwh_o)):
            for gg in range(NSA_G):
                lo = n * KV_W + gg * DH
                o[gg] = kv[:, lo:lo + DH].astype(BF16)
        gtt_o[...] = jnp.transpose(_sigmoid(z))
    else:
        (nq_o,) = rest
        nq_o[...] = nq


def _inproj(x, scale, shift, g1, w, a2p, ab, *, head_major, tm, seq=None):
    m = x.shape[0]
    nb, r, _ = scale.shape
    rows_per_mod = m // nb
    mod_spec = pl.BlockSpec((1, r, D_MODEL), lambda i: (i * tm // rows_per_mod, 0, 0))
    row = lambda wdt: pl.BlockSpec((tm, wdt), lambda i: (i, 0))
    widths = [QK_W, QK_W, V_W, V_W, QK_W] + [KV_W] * 6 + [LANES, D_MODEL, D_MODEL]
    out_shape = [jax.ShapeDtypeStruct((m, wd), F32) for wd in widths]
    out_specs = [row(wd) for wd in widths]
    if head_major:
        def rows_major(n):
            out_shape.append(jax.ShapeDtypeStruct((n, m, DH), BF16))
            out_specs.append(pl.BlockSpec((n, tm, DH), lambda i: (0, i, 0)))

        def lanes_major(n):
            out_shape.append(jax.ShapeDtypeStruct((n, DH, m), BF16))
            out_specs.append(pl.BlockSpec((n, DH, tm), lambda i: (0, 0, i)))

        lanes_major(NSA_HEADS)
        rows_major(NSA_G)
        lanes_major(NSA_G)
        rows_major(NSA_G)
        lanes_major(NSA_G)
        out_shape.append(jax.ShapeDtypeStruct((LANES, m), F32))
        out_specs.append(pl.BlockSpec((LANES, tm), lambda i: (0, i)))
        tiles_per_seq = seq // tm
        for _ in range(4):
            out_shape.append(jax.ShapeDtypeStruct((m // seq, NSA_G, DH, seq), F32))
            out_specs.append(pl.BlockSpec((1, NSA_G, DH, tm),
                                          lambda i: (i // tiles_per_seq, 0, 0, i % tiles_per_seq)))
    else:
        out_shape.append(jax.ShapeDtypeStruct((m, D_MODEL), F32))
        out_specs.append(row(D_MODEL))
    return pl.pallas_call(
        functools.partial(_inproj_kernel, head_major=head_major),
        out_shape=out_shape,
        grid=(m // tm,),
        in_specs=[row(D_MODEL), mod_spec, mod_spec, _const_spec((1, D_MODEL)), _const_spec((D_MODEL, W_COLS)),
                  _const_spec((LANES, QK_W)), _const_spec((1, QK_W))],
        out_specs=out_specs,
        compiler_params=_cparams(("parallel",)),
        name="inproj",
    )(x, scale, shift, g1, w, a2p, ab)


def _gla_kernel(*refs, chunk, sub, n_chunk, n_seq, has_init):
    if has_init:
        q_ref, k_ref, v_ref, la_ref, sgr_ref, gn_ref, s0_ref, o_ref, so_ref, s_scr = refs
    else:
        q_ref, k_ref, v_ref, la_ref, sgr_ref, gn_ref, o_ref, so_ref, s_scr = refs
    t = pl.program_id(1)

    @pl.when(t == 0)
    def _():
        s_scr[...] = s0_ref[...] if has_init else jnp.zeros_like(s_scr)

    tril = (lax.broadcasted_iota(jnp.int32, (chunk, chunk), 0)
            >= lax.broadcasted_iota(jnp.int32, (chunk, chunk), 1)).astype(F32)
    sub_row = lax.broadcasted_iota(jnp.int32, (sub, 1), 0)

    def one_chunk(c, carry):
        r0 = pl.multiple_of(c * chunk, chunk)
        rows = pl.ds(r0, chunk)
        for n in range(n_seq):
            b_all = _dot(tril, la_ref[n, rows, :], precision=HIGHEST)
            for h in range(GLA_HEADS):
                one_head(n, h, rows, b_all[:, h * GLA_DK:(h + 1) * GLA_DK])
        return carry

    def one_head(n, h, rows, b):
        ks, vs = slice(h * GLA_DK, (h + 1) * GLA_DK), slice(h * GLA_DV, (h + 1) * GLA_DV)
        q = q_ref[n, rows, ks]
        k = k_ref[n, rows, ks]
        v = v_ref[n, rows, vs]
        s_old = s_scr[n, h]
        vb = v.astype(BF16)
        o_inter = _dot((q * jnp.exp(b)).astype(BF16), s_old.astype(BF16))
        blocks = []
        for blk in range(chunk // sub):
            sl = slice(blk * sub, (blk + 1) * sub)
            q_i, k_i, v_i, b_i = q[sl], k[sl], v[sl], b[sl]
            acc = jnp.zeros((sub, GLA_DV), F32)
            for j in range(sub):
                w = q_i * k_i[j:j + 1] * jnp.exp(jnp.minimum(b_i - b_i[j:j + 1], 0.0))
                a = jnp.sum(w, axis=-1, keepdims=True)
                acc = acc + jnp.where(sub_row >= j, a, 0.0) * v_i[j:j + 1]
            if blk > 0:
                prev = blk * sub
                ref_row = b_i[0:1]
                qp = (q_i * jnp.exp(b_i - ref_row)).astype(BF16)
                kp = (k[:prev] * jnp.exp(ref_row - b[:prev])).astype(BF16)
                acc = acc + _dot(_dot_nt(qp, kp).astype(BF16), vb[:prev])
            blocks.append(acc)
        o = o_inter + (jnp.concatenate(blocks, axis=0) if len(blocks) > 1 else blocks[0])
        b_last = b[chunk - 1:chunk]
        kp = (k * jnp.exp(b_last - b)).astype(BF16)
        decay_col = jnp.transpose(jnp.broadcast_to(jnp.exp(b_last), (GLA_DK, GLA_DK)))[:, 0:1]
        s_scr[n, h] = decay_col * s_old + _dot_tn(kp, vb)
        o_ref[n, rows, vs] = _rms(o, gn_ref[...]) * sgr_ref[n, rows, vs]

    lax.fori_loop(0, n_chunk, one_chunk, 0)

    @pl.when(t == pl.num_programs(1) - 1)
    def _():
        so_ref[...] = s_scr[...]


def _gla(q, k, v, la, sgr, gn, s0, *, tb, n_seq):
    bsz, t, _ = q.shape
    chunk = min(GLA_CHUNK, tb)
    sub = min(GLA_SUB, chunk)
    blk = lambda wd: pl.BlockSpec((n_seq, tb, wd), lambda b, i: (b, i, 0))
    st_spec = pl.BlockSpec((n_seq, GLA_HEADS, GLA_DK, GLA_DV), lambda b, i: (b, 0, 0, 0))
    in_specs = [blk(QK_W), blk(QK_W), blk(V_W), blk(QK_W), blk(V_W),
                pl.BlockSpec((1, GLA_DV), lambda b, i: (0, 0))]
    args = [q, k, v, la, sgr, gn]
    if s0 is not None:
        in_specs.append(st_spec)
        args.append(s0)
    return pl.pallas_call(
        functools.partial(_gla_kernel, chunk=chunk, sub=sub, n_chunk=tb // chunk, n_seq=n_seq,
                          has_init=s0 is not None),
        out_shape=[jax.ShapeDtypeStruct((bsz, t, V_W), F32),
                   jax.ShapeDtypeStruct((bsz, GLA_HEADS, GLA_DK, GLA_DV), F32)],
        grid=(bsz // n_seq, t // tb),
        in_specs=in_specs,
        out_specs=[blk(V_W), st_spec],
        scratch_shapes=[pltpu.VMEM((n_seq, GLA_HEADS, GLA_DK, GLA_DV), F32)],
        compiler_params=_cparams(("parallel", "arbitrary")),
        name="gla",
    )(*args)


def _mm_kernel(x_ref, w_ref, o_ref):
    o_ref[...] = _dot(x_ref[...].astype(BF16), w_ref[...])


def _mm(x, w, *, tm):
    m, kdim = x.shape
    n = w.shape[1]
    return pl.pallas_call(
        _mm_kernel,
        out_shape=jax.ShapeDtypeStruct((m, n), F32),
        grid=(m // tm,),
        in_specs=[pl.BlockSpec((tm, kdim), lambda i: (i, 0)), _const_spec((kdim, n))],
        out_specs=pl.BlockSpec((tm, n), lambda i: (i, 0)),
        compiler_params=_cparams(("parallel",)),
        name="cmp_mm",
    )(x, w)


CMP_TAPS = 4
SUBLANES = 8


def _odd_pitch(n):
    tiles = -(-n // SUBLANES)
    return (tiles + 1 - tiles % 2) * SUBLANES


def _cmp_paged_kernel(*refs, pps):
    pages = refs[1:1 + pps]
    w_ref, o_ref, tok_s = refs[1 + pps:]
    heads_per_slab = LANES // DH
    n_slab = NSA_G // heads_per_slab
    n_seg = pps * PAGE // CMP_STRIDE
    pitch = tok_s.shape[1] // CMP_STRIDE
    segs_per_page = PAGE // CMP_STRIDE
    for p in range(pps):
        t = jnp.transpose(pages[p][0].reshape(KV_W, PAGE))
        for s in range(segs_per_page):
            for sl in range(n_slab):
                tok_s[sl, pl.ds(p * segs_per_page + s, CMP_STRIDE, stride=pitch), :] = (
                    t[s * CMP_STRIDE:(s + 1) * CMP_STRIDE, sl * LANES:(sl + 1) * LANES])
    width = CMP_RATIO * CMP_HID
    first_half = lax.broadcasted_iota(jnp.int32, (1, LANES), 1) < DH
    for sl in range(n_slab):
        x = [tok_s[sl, e * pitch:e * pitch + n_seg, :] for e in range(CMP_STRIDE)]
        r = [pltpu.roll(v, DH, 1) for v in x]
        for odd in range(heads_per_slab):
            g = sl * heads_per_slab + odd
            acc = None
            for j in range(CMP_STRIDE // CMP_TAPS):
                pieces = []
                for e in range(j * CMP_TAPS, (j + 1) * CMP_TAPS, 2):
                    lo, hi = (r[e], x[e + 1]) if odd else (x[e], r[e + 1])
                    pieces.append(jnp.where(first_half, lo, hi))
                term = _dot(jnp.concatenate(pieces, axis=1).astype(BF16), w_ref[j])
                acc = term if acc is None else acc + term
            o_ref[0, :, g * width:(g + 1) * width] = acc


def _cmp_paged(pool_t, page_table, w_taps, *, pages_per_step):
    bsz, n_pages = page_table.shape
    pps = pages_per_step
    n_seg = pps * PAGE // CMP_STRIDE
    width = NSA_G * CMP_RATIO * CMP_HID

    def page_spec(j):
        return pl.BlockSpec((1, NSA_G, DH, PAGE), lambda b, s, pt: (pt[b, s * pps + j], 0, 0, 0))

    grid_spec = pltpu.PrefetchScalarGridSpec(
        num_scalar_prefetch=1,
        grid=(bsz, n_pages // pps),
        in_specs=[page_spec(j) for j in range(pps)]
        + [pl.BlockSpec(w_taps.shape, lambda b, s, pt: (0, 0, 0), pipeline_mode=pl.Buffered(1))],
        out_specs=pl.BlockSpec((1, n_seg, width), lambda b, s, pt: (b, s, 0)),
        scratch_shapes=[pltpu.VMEM((NSA_G * DH // LANES, CMP_STRIDE * _odd_pitch(n_seg), LANES), F32)],
    )
    return pl.pallas_call(
        functools.partial(_cmp_paged_kernel, pps=pps),
        out_shape=jax.ShapeDtypeStruct((bsz, n_pages * PAGE // CMP_STRIDE, width), F32),
        grid_spec=grid_spec,
        compiler_params=_cparams(("parallel", "arbitrary")),
        name="cmp_paged",
    )(page_table, *([pool_t] * pps), w_taps)


def _cmpfin_kernel(*refs, n, has_new, layout):
    if has_new:
        p_ref, pn_ref, pe_ref, w1_ref, w2_ref, o_ref = refs
    else:
        p_ref, pe_ref, w1_ref, w2_ref, o_ref = refs
    pe_term = _dot(pe_ref[...], w1_ref[...], precision=HIGHEST)[0:1]
    w2 = w2_ref[...].astype(BF16)
    rowi = lax.broadcasted_iota(jnp.int32, (n, 1), 0)
    for g in range(NSA_G):
        base = g * CMP_RATIO * CMP_HID
        first = p_ref[0, :, base:base + CMP_HID]
        second = pltpu.roll(p_ref[0, :, base + CMP_HID:base + 2 * CMP_HID], n - 1, 0)
        if has_new:
            second = jnp.where(rowi == n - 1, pn_ref[0, :, base + CMP_HID:base + 2 * CMP_HID], second)
        acc = first + second + pe_term
        hid = (acc * _sigmoid(acc)).astype(BF16)
        if layout == 'flat':
            o_ref[0, :, g * DH:(g + 1) * DH] = _dot(hid, w2)
        elif layout == 'rows':
            o_ref[0, g] = _dot(hid, w2)
        else:
            o_ref[0, g] = _dot_nt(w2, hid)


def _cmpfin(parts, parts_new, pe, w1, w2, *, layout):
    bsz, n, width = parts.shape
    in_specs = [pl.BlockSpec((1, n, width), lambda b: (b, 0, 0))]
    args = [parts]
    if parts_new is not None:
        in_specs.append(pl.BlockSpec((1, 1, width), lambda b: (b, 0, 0)))
        args.append(parts_new)
    in_specs += [_const_spec(pe.shape), _const_spec(w1.shape), _const_spec(w2.shape)]
    if layout == 'flat':
        out_shape = jax.ShapeDtypeStruct((bsz, n, KV_W), F32)
        out_spec = pl.BlockSpec((1, n, KV_W), lambda b: (b, 0, 0))
    else:
        inner = (n, DH) if layout == 'rows' else (DH, n)
        out_shape = jax.ShapeDtypeStruct((bsz, NSA_G) + inner, F32)
        out_spec = pl.BlockSpec((1, NSA_G) + inner, lambda b: (b, 0, 0, 0))
    return pl.pallas_call(
        functools.partial(_cmpfin_kernel, n=n, has_new=parts_new is not None, layout=layout),
        out_shape=out_shape,
        grid=(bsz,),
        in_specs=in_specs,
        out_specs=out_spec,
        compiler_params=_cparams(("parallel",)),
        name="cmp_finish",
    )(*args, pe, w1, w2)


def _masked_softmax(s, mask, axis, exp=jnp.exp):
    s = jnp.where(mask, s, NEG)
    m = jnp.max(s, axis=axis, keepdims=True)
    e = jnp.where(mask, exp(s - m), 0.0)
    return e / jnp.maximum(jnp.sum(e, axis=axis, keepdims=True), 1e-30)


CMP_WIN_ROWS = 24
CMP_WIN_PAD = 16
FAR_TILES = 4


def _nsa_prompt_kernel(qt_ref, kc_ref, vct_ref, ks_ref, vst_ref, kw_ref, vwt_ref, gtt_ref, bt_ref, wc_ref,
                       c2s_ref, o_ref, sel_s, bc_s, m_s, l_s, acc_s, *, n_cmp, n_sel, n_tiles):
    i = pl.program_id(1)
    tok = i * TQ + lax.broadcasted_iota(jnp.int32, (1, TQ), 1)
    tok4 = jnp.concatenate([tok] * NSA_HG, axis=1)
    row = lax.broadcasted_iota(jnp.int32, (TQ, 1), 0)
    groups = range(NSA_G)
    q4 = [jnp.concatenate([qt_ref[g * NSA_HG + h] for h in range(NSA_HG)], axis=1) for g in groups]

    def tile4(x):
        return jnp.concatenate([x] * NSA_HG, axis=1)

    o_cmp = []
    for g in groups:
        bc_s[g] = jnp.zeros(bc_s.shape[1:], F32)
        bc_s[g, pl.ds(pl.multiple_of(i * (TQ // CMP_STRIDE), TQ // CMP_STRIDE), CMP_WIN_ROWS), :] = wc_ref[g]
        s_c = _dot(kc_ref[0, g].astype(BF16), q4[g]) + bc_s[g, CMP_WIN_PAD:CMP_WIN_PAD + TQ, :]
        mask_c = (tok4 - (row * CMP_STRIDE + CMP_BLOCK - 1) >= 0) & (row < n_cmp)
        p_c = _masked_softmax(s_c, mask_c, 0, exp=jnp.exp2)
        o_cmp.append(_dot(vct_ref[0, g].astype(BF16), p_c.astype(BF16)))
        p_sum = p_c[:, 0:TQ] + p_c[:, TQ:2 * TQ] + p_c[:, 2 * TQ:3 * TQ] + p_c[:, 3 * TQ:4 * TQ]
        imp = _dot(c2s_ref[...], p_sum, precision=HIGHEST)
        blk = lax.broadcasted_iota(jnp.int32, (n_sel, 1), 0)
        cur = jnp.right_shift(tok, SEL_SHIFT)
        forced = (blk == 0) | (blk == cur) | (blk == cur - 1)
        valid = blk * SEL <= tok
        score = jnp.where(forced, FORCE, jnp.where(valid, imp, -FORCE))
        cnt = jnp.zeros((n_sel, TQ), jnp.int32)
        for b in range(n_sel):
            s_b = score[b:b + 1]
            ahead = (s_b > score) | ((s_b == score) & (blk > b))
            cnt = cnt + ahead.astype(jnp.int32)
        sel_s[g] = jnp.where(cnt < min(TOPK, n_sel), 1.0, 0.0)

    def tile_scores(k_ref, g, kt):
        k0 = pl.multiple_of(jnp.clip(kt, 0, n_tiles - 1) * TQ, TQ)
        return _dot(k_ref[g, 0, pl.ds(k0, TQ), :], q4[g]), k0

    def sel_mask(g, kt, valid):
        blocks_per_tile = TQ // SEL
        first = jnp.clip(kt, 0, n_tiles - 1) * blocks_per_tile
        out = None
        for j in reversed(range(blocks_per_tile)):
            add_j = jnp.where((sel_s[g, pl.ds(first + j, 1), :] > 0.5) & valid, 0.0, NEG)
            out = add_j if out is None else jnp.where(row < (j + 1) * SEL, add_j, out)
        return out

    def update(state, tiles):
        m_run, l_run, acc = state
        m_new = m_run
        for s, _ in tiles:
            m_new = jnp.maximum(m_new, jnp.max(s, axis=0, keepdims=True))
        alpha = jnp.exp2(m_run - m_new)
        l_new = alpha * l_run
        acc = alpha * acc
        for s, vt in tiles:
            p = jnp.exp2(s - m_new)
            l_new = l_new + jnp.sum(p, axis=0, keepdims=True)
            acc = acc + _dot(vt, p.astype(BF16))
        return m_new, l_new, acc

    def near_tiles(k_ref, vt_ref, g, mask_fn):
        tiles = []
        for typ, kt in ((1, i - 1), (0, i)):
            s, k0 = tile_scores(k_ref, g, kt)
            madd = mask_fn(kt, tok - (kt * TQ + row))
            tiles.append((s + bt_ref[typ, g] + tile4(madd), vt_ref[g, :, pl.ds(k0, TQ)]))
        return tiles

    cols = NSA_HG * TQ
    empty = (jnp.full((1, cols), NEG, F32), jnp.zeros((1, cols), F32), jnp.zeros((DH, cols), F32))

    def normalised(state):
        return state[2] / jnp.maximum(state[1], 1e-30)

    for g in groups:
        m_s[g], l_s[g], acc_s[g] = empty
    n_far = jnp.maximum(i - 1, 0)

    def far_body(c, carry):
        for g in groups:
            tiles = []
            for u in range(FAR_TILES):
                kt = FAR_TILES * c + u
                s, k0 = tile_scores(ks_ref, g, kt)
                tiles.append((s + tile4(sel_mask(g, kt, kt < n_far)), vst_ref[g, :, pl.ds(k0, TQ)]))
            m_s[g], l_s[g], acc_s[g] = update((m_s[g], l_s[g], acc_s[g]), tiles)
        return carry

    lax.fori_loop(0, (n_far + FAR_TILES - 1) // FAR_TILES, far_body, 0)

    heads = []
    for g in groups:
        slc_near = near_tiles(ks_ref, vst_ref, g,
                              lambda kt, dist: jnp.where(dist >= 0, sel_mask(g, kt, kt >= 0), NEG))
        o_slc = normalised(update((m_s[g], l_s[g], acc_s[g]), slc_near))

        tiles = []
        for back in range(WIN // TQ, 1, -1):
            kt = i - back
            s, k0 = tile_scores(kw_ref, g, kt)
            if back == WIN // TQ:
                dist = tok - (kt * TQ + row)
                s = s + tile4(jnp.where((dist < WIN) & (kt >= 0), 0.0, NEG))
            else:
                s = s + jnp.where(kt >= 0, 0.0, NEG)
            tiles.append((s, vwt_ref[g, :, pl.ds(k0, TQ)]))
        win_near = near_tiles(kw_ref, vwt_ref, g, lambda kt, dist: jnp.where((dist >= 0) & (kt >= 0), 0.0, NEG))
        o_win = normalised(update(update(empty, tiles), win_near))

        for h in range(NSA_HG):
            cs = slice(h * TQ, (h + 1) * TQ)
            o_h = jnp.zeros((DH, TQ), F32)
            for br, o_br in enumerate((o_cmp[g], o_slc, o_win)):
                gate_row = br * NSA_HEADS + g * NSA_HG + h
                o_h = o_h + gtt_ref[gate_row:gate_row + 1, :] * o_br[:, cs]
            heads.append(o_h)
    o_ref[0] = jnp.transpose(jnp.concatenate(heads, axis=0))


def _nsa_prompt(qt, kc, vct, ksh, vst, kwh, vwt, gtt, bias_t, bias_wc, c2s_t, *, bsz, n_cmp):
    t = qt.shape[2] // bsz
    nq = t // TQ
    n_sel = t // SEL
    cols = NSA_HG * TQ
    k_spec = pl.BlockSpec((NSA_G, 1, t, DH), lambda b, i: (0, b, 0, 0))
    vt_spec = pl.BlockSpec((NSA_G, DH, t), lambda b, i: (0, 0, b))
    const = lambda arr: pl.BlockSpec(arr.shape, lambda b, i: (0,) * arr.ndim)
    return pl.pallas_call(
        functools.partial(_nsa_prompt_kernel, n_cmp=n_cmp, n_sel=n_sel, n_tiles=nq),
        out_shape=jax.ShapeDtypeStruct((bsz, t, D_MODEL), F32),
        grid=(bsz, nq),
        in_specs=[pl.BlockSpec((NSA_HEADS, DH, TQ), lambda b, i: (0, 0, b * nq + i)),
                  pl.BlockSpec((1, NSA_G, kc.shape[2], DH), lambda b, i: (b, 0, 0, 0)),
                  pl.BlockSpec((1, NSA_G, DH, vct.shape[3]), lambda b, i: (b, 0, 0, 0)),
                  k_spec, vt_spec, k_spec, vt_spec,
                  pl.BlockSpec((LANES, TQ), lambda b, i: (0, b * nq + i)),
                  const(bias_t), const(bias_wc), const(c2s_t)],
        out_specs=pl.BlockSpec((1, TQ, D_MODEL), lambda b, i: (b, i, 0)),
        scratch_shapes=[pltpu.VMEM((NSA_G, n_sel, TQ), F32), pltpu.VMEM((NSA_G, CMP_WIN_PAD + TQ, cols), F32),
                        pltpu.VMEM((NSA_G, 1, cols), F32), pltpu.VMEM((NSA_G, 1, cols), F32),
                        pltpu.VMEM((NSA_G, DH, cols), F32)],
        compiler_params=_cparams(("parallel", "arbitrary")),
        name="nsa_prompt",
    )(qt, kc, vct, ksh, vst, kwh, vwt, gtt, bias_t, bias_wc, c2s_t)


def _nsa_sample_kernel(*refs, pps, n_cmp, n_sel, n_sel_pad, past_len, t_new):
    k_pages = refs[1:1 + pps]
    v_pages = refs[1 + pps:1 + 2 * pps]
    (q_ref, kct_ref, vc_ref, kn_ref, vn_ref, kwt_ref, vwt_ref, kwn_ref, vwn_ref, gt_ref, bc_ref, bs_ref, bn_ref,
     bw_ref, c2s_ref, hsum_ref, hexp_ref, e_ref, o_ref,
     sel_s, m_s, l_s, acc_s, ocmp_s, owin_s) = refs[1 + 2 * pps:]
    ch = pl.program_id(1)
    nrow = NSA_HEADS * t_new
    q = q_ref[0]
    rtok = lax.broadcasted_iota(jnp.int32, (nrow, 1), 0) & (t_new - 1)
    qpos = past_len + rtok
    lane = lax.broadcasted_iota(jnp.int32, (1, LANES), 1)

    def to_col(row_vec):
        return jnp.transpose(jnp.broadcast_to(row_vec, (LANES, LANES)))[:nrow, 0:1]

    @pl.when(ch == 0)
    def _():
        c_end = lax.broadcasted_iota(jnp.int32, (1, n_cmp), 1) * CMP_STRIDE + CMP_BLOCK - 1
        s_c = _dot(q, kct_ref[0].astype(BF16)) + bc_ref[...]
        p_c = _masked_softmax(s_c, qpos - c_end >= 0, -1)
        ocmp_s[...] = _dot(p_c.astype(BF16), vc_ref[0].astype(BF16))

        n_gt = NSA_G * t_new
        p_sum = _dot(hsum_ref[...], jnp.concatenate([p_c, jnp.zeros((LANES - nrow, n_cmp), F32)], axis=0),
                     precision=HIGHEST)[:n_gt]
        imp = _dot(p_sum, c2s_ref[...], precision=HIGHEST)
        n_blk_lanes = imp.shape[1]
        blk = lax.broadcasted_iota(jnp.int32, (1, n_blk_lanes), 1)
        gpos = past_len + (lax.broadcasted_iota(jnp.int32, (n_gt, 1), 0) & (t_new - 1))
        cur = jnp.right_shift(gpos, SEL_SHIFT)
        forced = (blk == 0) | (blk == cur) | (blk == cur - 1)
        valid = blk * SEL <= gpos
        score = jnp.where(forced, FORCE, jnp.where(valid, imp, -FORCE))
        score = jnp.where(blk < n_sel, score, -3.0 * FORCE)
        cnt = jnp.zeros((n_gt, n_blk_lanes), jnp.int32)
        for b in range(n_sel):
            s_b = score[:, b:b + 1]
            ahead = (s_b > score) | ((s_b == score) & (blk > b))
            cnt = cnt + ahead.astype(jnp.int32)
        sel = jnp.where((cnt < min(TOPK, n_sel)) & (blk < n_sel), 1.0, 0.0)
        sel_t = jnp.transpose(jnp.concatenate([sel, jnp.zeros((LANES - n_gt, n_blk_lanes), F32)], axis=0))
        sel_s[...] = _dot(sel_t[:n_sel_pad].astype(BF16), hexp_ref[...])

        key_w = past_len - WIN + lax.broadcasted_iota(jnp.int32, (1, WIN), 1)
        s_w = _dot(q, kwt_ref[0].astype(BF16)) + bw_ref[...]
        mask_w = lax.bitcast_convert_type(qpos - key_w, jnp.uint32) < WIN
        s_n = _dot_nt(q, kwn_ref[0].astype(BF16)) + bn_ref[...]
        mask_n = (lane < t_new) & (rtok - lane >= 0)
        s_w = jnp.where(mask_w, s_w, NEG)
        s_n = jnp.where(mask_n, s_n, NEG)
        m = jnp.maximum(jnp.max(s_w, axis=-1, keepdims=True), jnp.max(s_n, axis=-1, keepdims=True))
        e_w = jnp.where(mask_w, jnp.exp(s_w - m), 0.0)
        e_n = jnp.where(mask_n, jnp.exp(s_n - m), 0.0)
        den = jnp.maximum(jnp.sum(e_w, axis=-1, keepdims=True) + jnp.sum(e_n, axis=-1, keepdims=True), 1e-30)
        owin_s[...] = (_dot_nt((e_w / den).astype(BF16), vwt_ref[0].astype(BF16))
                       + _dot((e_n / den).astype(BF16), vwn_ref[0].astype(BF16)))

        s_t = _dot_nt(q, kn_ref[0].astype(BF16)) + bn_ref[...]
        mask_t = mask_n & (to_col(sel_s[pl.ds(past_len // SEL, 1), :]) > 0.5)
        s_t = jnp.where(mask_t, s_t, NEG)
        m0 = jnp.max(s_t, axis=-1, keepdims=True)
        p0 = jnp.where(mask_t, jnp.exp(s_t - m0), 0.0)
        m_s[...] = m0
        l_s[...] = jnp.sum(p0, axis=-1, keepdims=True)
        acc_s[...] = _dot(p0.astype(BF16), vn_ref[0].astype(BF16))

    kt_all = jnp.concatenate([r[0].reshape(KV_W, PAGE) for r in k_pages], axis=1).astype(BF16)
    vt_all = jnp.concatenate([r[0].reshape(KV_W, PAGE) for r in v_pages], axis=1).astype(BF16)
    s = _dot(q, kt_all) + bs_ref[0]
    blocks = pps * PAGE // SEL
    sel_blk = sel_s[pl.ds(pl.multiple_of(ch * blocks, blocks), blocks), :][:, :nrow].astype(BF16)
    mask = _dot_tn(sel_blk, e_ref[...]) > 0.5
    s = jnp.where(mask, s, NEG)
    m_old = m_s[...]
    m_new = jnp.maximum(m_old, jnp.max(s, axis=-1, keepdims=True))
    p = jnp.where(mask, jnp.exp(s - m_new), 0.0)
    alpha = jnp.exp(m_old - m_new)
    l_s[...] = alpha * l_s[...] + jnp.sum(p, axis=-1, keepdims=True)
    acc_s[...] = alpha * acc_s[...] + _dot_nt(p.astype(BF16), vt_all)
    m_s[...] = m_new

    @pl.when(ch == pl.num_programs(1) - 1)
    def _():
        o_slc = acc_s[...] / jnp.maximum(l_s[...], 1e-30)
        gates = gt_ref[0]
        hg_rows = NSA_HG * t_new
        for g in range(NSA_G):
            rs = slice(g * hg_rows, (g + 1) * hg_rows)
            cs = slice(g * DH, (g + 1) * DH)
            o_ref[0, rs, :] = (gates[rs, 0:1] * ocmp_s[rs, cs] + gates[rs, 1:2] * o_slc[rs, cs]
                               + gates[rs, 2:3] * owin_s[rs, cs])


def _nsa_sample(page_table, k_pool, v_pool, q_bd, kct, vc, k_new, v_new, kw_cache, vw_cache, kw_new, vw_new, gates,
                bias_c, bias_s, bias_n, bias_w, c2s, hsum_t, hexp, expand, *, pps, t_new):
    bsz, n_pages = page_table.shape
    past_len = n_pages * PAGE
    n_cmp = vc.shape[1]
    n_sel = past_len // SEL + 1
    n_sel_pad = -(-n_sel // 8) * 8
    nrow = NSA_HEADS * t_new

    def page_spec(j):
        return pl.BlockSpec((1, NSA_G, DH, PAGE), lambda b, s, pt: (pt[b, s * pps + j], 0, 0, 0))

    per_b = lambda shape: pl.BlockSpec((1,) + shape, lambda b, s, pt: (b,) + (0,) * len(shape))
    const = lambda arr: pl.BlockSpec(arr.shape, lambda b, s, pt: (0,) * arr.ndim)
    in_specs = ([page_spec(j) for j in range(pps)] * 2
                + [per_b((nrow, KV_W)), per_b((KV_W, n_cmp)), per_b((n_cmp, KV_W)),
                   per_b((PAGE, KV_W)), per_b((PAGE, KV_W)), per_b((KV_W, WIN)), per_b((KV_W, WIN)),
                   per_b((PAGE, KV_W)), per_b((PAGE, KV_W)), per_b((nrow, 3)),
                   const(bias_c),
                   pl.BlockSpec((1, nrow, pps * PAGE), lambda b, s, pt: (s, 0, 0)),
                   const(bias_n), const(bias_w), const(c2s), const(hsum_t), const(hexp), const(expand)])
    grid_spec = pltpu.PrefetchScalarGridSpec(
        num_scalar_prefetch=1,
        grid=(bsz, n_pages // pps),
        in_specs=in_specs,
        out_specs=pl.BlockSpec((1, nrow, DH), lambda b, s, pt: (b, 0, 0)),
        scratch_shapes=[pltpu.VMEM((n_sel_pad, LANES), F32),
                        pltpu.VMEM((nrow, 1), F32), pltpu.VMEM((nrow, 1), F32),
                        pltpu.VMEM((nrow, KV_W), F32), pltpu.VMEM((nrow, KV_W), F32),
                        pltpu.VMEM((nrow, KV_W), F32)],
    )
    return pl.pallas_call(
        functools.partial(_nsa_sample_kernel, pps=pps, n_cmp=n_cmp, n_sel=n_sel, n_sel_pad=n_sel_pad,
                          past_len=past_len, t_new=t_new),
        out_shape=jax.ShapeDtypeStruct((bsz, nrow, DH), F32),
        grid_spec=grid_spec,
        compiler_params=_cparams(("parallel", "arbitrary")),
        name="nsa_sample",
    )(page_table, *([k_pool] * pps), *([v_pool] * pps), q_bd, kct, vc, k_new, v_new, kw_cache, vw_cache, kw_new,
      vw_new, gates, bias_c, bias_s, bias_n, bias_w, c2s, hsum_t, hexp, expand)


def _post_kernel(x_ref, og_ref, on_ref, sma_ref, smb_ref, gate1_ref, sc2_ref, sh2_ref, gate2_ref, n2_ref, fg_ref,
                 wo_ref, wu_ref, wd_ref, y_ref):
    merged = sma_ref[...] * og_ref[...] + smb_ref[...] * on_ref[...]
    x1 = x_ref[...] + gate1_ref[0] * _dot(merged.astype(BF16), wo_ref[...])
    h2 = _rms(x1, n2_ref[...]) * (1.0 + sc2_ref[0]) + sh2_ref[0]
    up = jnp.maximum(_dot(h2.astype(BF16), wu_ref[...]), 0.0)
    x2 = x1 + gate2_ref[0] * _dot((up * up).astype(BF16), wd_ref[...])
    y_ref[...] = _rms(x2, fg_ref[...])


def _post(x, o_gla, o_nsa, sma, smb, gate1, scale2, shift2, gate2, n2, fg, wo, wu, wd, *, tm):
    m = x.shape[0]
    nb, r, _ = gate1.shape
    rows_per_mod = m // nb
    mod_spec = pl.BlockSpec((1, r, D_MODEL), lambda i: (i * tm // rows_per_mod, 0, 0))
    row = pl.BlockSpec((tm, D_MODEL), lambda i: (i, 0))
    return pl.pallas_call(
        _post_kernel,
        out_shape=jax.ShapeDtypeStruct((m, D_MODEL), F32),
        grid=(m // tm,),
        in_specs=[row] * 5 + [mod_spec] * 4 + [_const_spec((1, D_MODEL))] * 2
        + [_const_spec(wo.shape), _const_spec(wu.shape), _const_spec(wd.shape)],
        out_specs=row,
        compiler_params=_cparams(("parallel",)),
        name="post",
    )(x, o_gla, o_nsa, sma, smb, gate1, scale2, shift2, gate2, n2, fg, wo, wu, wd)


def _block_diag_w1(w1):
    w = w1.reshape(CMP_RATIO, CMP_STRIDE, DH, CMP_HID)
    eye = jnp.eye(NSA_G, dtype=w1.dtype)
    bd = jnp.einsum('redh,pg->epdgrh', w, eye)
    return bd.reshape(CMP_STRIDE * NSA_G * DH, NSA_G * CMP_RATIO * CMP_HID).astype(BF16)


def _tap_w1(w1):
    w = w1.reshape(CMP_RATIO, CMP_STRIDE // CMP_TAPS, CMP_TAPS, DH, CMP_HID)
    return jnp.transpose(w, (1, 2, 3, 0, 4)).reshape(CMP_STRIDE // CMP_TAPS, CMP_TAPS * DH,
                                                     CMP_RATIO * CMP_HID).astype(BF16)


def _cmp_to_sel_np(n_cmp, n_sel):
    cs = np.arange(n_cmp)[:, None] * CMP_STRIDE
    ss = np.arange(n_sel)[None, :] * SEL
    ov = np.minimum(cs + CMP_BLOCK, ss + SEL) - np.maximum(cs, ss)
    return np.clip(ov, 0, None).astype(np.float32) / CMP_BLOCK


def _bias_at(rel_bias, dist):
    onehot = (_bucket_np(dist)[..., None] == np.arange(REL_BUCKETS)).astype(np.float32)
    return jnp.moveaxis(jnp.dot(jnp.asarray(onehot), rel_bias, precision=HIGHEST), -1, 0)


def _bias_by_dist(rel_bias, n):
    return _bias_at(rel_bias, np.arange(n))


def _toeplitz(v, n):
    lead = v.shape[:-1]
    a = jnp.broadcast_to(v[..., None, :], lead + (n, 2 * n)).reshape(lead + (2 * n * n,))
    a = a[..., :n * (2 * n - 1)].reshape(lead + (n, 2 * n - 1))
    return a[..., n - 1:]


def _mod_split(mod, rows):
    parts = jnp.split(mod, 6, axis=-1)
    return [p.reshape(-1, rows, D_MODEL) for p in parts]


def kernel(x_prompt, x_sample, c_prompt, c_sample, state_gla, cache_k_cmp, cache_v_cmp, cache_k_slc, cache_v_slc, cache_k_win, cache_v_win, page_table, ada_w, ada_b, norm1_g, norm2_g, w_in, gla_a2, gla_a_b, gla_norm_g, cmp_pe_k, cmp_w1_k, cmp_w2_k, cmp_pe_v, cmp_w1_v, cmp_w2_v, w_o, w_up, w_down, rel_bias, final_g):
    depth = ada_w.shape[0]
    assert depth == 1, "single-layer trunk"
    bp, tp, _ = x_prompt.shape
    bs, ts, _ = x_sample.shape
    n_pages = page_table.shape[1]
    past_len = n_pages * PAGE
    mp, ms = bp * tp, bs * ts
    tm = 256
    tm_s = min(tm, ms)

    w = w_in[0]
    w_perm = jnp.concatenate([w[:, :3072], w[:, 3088:5648], w[:, 5696:7744], w[:, 5648:5696], w[:, 3072:3088],
                              jnp.zeros((D_MODEL, LANES - 64), w.dtype)], axis=1).astype(BF16)
    a2p = jnp.zeros((LANES, QK_W), F32).at[Z_GA:Z_GA + GLA_RANK].set(gla_a2[0]).astype(BF16)
    ab = gla_a_b[0].reshape(1, QK_W)
    g1 = norm1_g[0].reshape(1, D_MODEL)
    n2 = norm2_g[0].reshape(1, D_MODEL)
    fg = final_g.reshape(1, D_MODEL)
    gn = gla_norm_g[0].reshape(1, GLA_DV)
    wo, wu, wd = w_o[0].astype(BF16), w_up[0].astype(BF16), w_down[0].astype(BF16)
    cmp_k = (_block_diag_w1(cmp_w1_k[0]), jnp.broadcast_to(cmp_pe_k[0].reshape(1, -1), (8, CMP_BLOCK * DH)),
             cmp_w1_k[0].reshape(CMP_BLOCK * DH, CMP_HID), cmp_w2_k[0])
    cmp_v = (_block_diag_w1(cmp_w1_v[0]), jnp.broadcast_to(cmp_pe_v[0].reshape(1, -1), (8, CMP_BLOCK * DH)),
             cmp_w1_v[0].reshape(CMP_BLOCK * DH, CMP_HID), cmp_w2_v[0])

    mod = _ada(jnp.concatenate([c_prompt, c_sample], axis=0), ada_w[0], ada_b[0])
    sh1_p, sc1_p, gt1_p, sh2_p, sc2_p, gt2_p = _mod_split(mod[:bp], 1)
    sh1_s, sc1_s, gt1_s, sh2_s, sc2_s, gt2_s = _mod_split(jnp.repeat(mod[bp:], ts, axis=0), tm_s)

    xp = x_prompt.reshape(mp, D_MODEL)
    (gq, gk, gv, sgr, la, kc, vc, _, _, kw, vw, _, sma, smb, qt, ksh, vst, kwh, vwt, gtt,
     kc_t, vc_t, ks_t, vs_t) = _inproj(xp, sc1_p, sh1_p, g1, w_perm, a2p, ab, head_major=True, tm=tm, seq=tp)

    r3 = lambda a: a.reshape(bp, tp, a.shape[-1])
    o_gla_p, s_gla_p = _gla(r3(gq), r3(gk), r3(gv), r3(la), r3(sgr), gn, None, tb=256,
                            n_seq=2 if bp % 2 == 0 else 1)

    n_seg = tp // CMP_STRIDE
    assert n_seg == LANES, "prompt attention keeps one compressed block per lane"
    n_cmp_p = n_seg - CMP_RATIO + 1
    seg = lambda a: a.reshape(mp // CMP_STRIDE, CMP_STRIDE * KV_W)
    tm_seg = min(tm, mp // CMP_STRIDE)
    kcc = _cmpfin(_mm(seg(kc), cmp_k[0], tm=tm_seg).reshape(bp, n_seg, -1), None, *cmp_k[1:], layout='rows')
    vcc = _cmpfin(_mm(seg(vc), cmp_v[0], tm=tm_seg).reshape(bp, n_seg, -1), None, *cmp_v[1:3],
                  jnp.transpose(cmp_v[3]), layout='lanes')

    n_sel_p = tp // SEL
    f = _bias_by_dist(rel_bias, 2 * TQ)
    v_same = jnp.concatenate([jnp.repeat(f[:, :1], TQ - 1, axis=1), f[:, :TQ + 1]], axis=1)
    v_prev = jnp.concatenate([f[:, 1:], f[:, -1:]], axis=1)
    tiles = _toeplitz(jnp.stack([v_same, v_prev]), TQ).reshape(2, NSA_G, NSA_HG, TQ, TQ)
    tiles = jnp.transpose(tiles, (0, 1, 3, 2, 4)).reshape(2, NSA_G, TQ, NSA_HG * TQ)
    far_cols = jnp.broadcast_to(f[:, -1].reshape(NSA_G, NSA_HG, 1), (NSA_G, NSA_HG, TQ)).reshape(NSA_G, 1, -1)
    bias_t = (tiles - far_cols[None]) * LOG2E
    d0 = [-CMP_STRIDE * (r - CMP_WIN_PAD) - (CMP_BLOCK - 1) for r in range(CMP_WIN_ROWS)]
    lo_pad, hi_pad = max(0, -min(d0)), max(0, max(d0) + TQ - 2 * TQ)
    f_pad = jnp.concatenate([jnp.repeat(f[:, :1], lo_pad, axis=1), f, jnp.repeat(f[:, -1:], hi_pad, axis=1)], axis=1)
    bias_wc = jnp.stack([f_pad[:, d + lo_pad:d + lo_pad + TQ] for d in d0], axis=1)
    bias_wc = bias_wc.reshape(NSA_G, NSA_HG, CMP_WIN_ROWS, TQ)
    bias_wc = jnp.transpose(bias_wc, (0, 2, 1, 3)).reshape(NSA_G, CMP_WIN_ROWS, NSA_HG * TQ)
    bias_wc = (bias_wc - far_cols) * LOG2E
    c2s_t = np.zeros((n_sel_p, n_seg), np.float32)
    c2s_t[:, :n_cmp_p] = _cmp_to_sel_np(n_cmp_p, n_sel_p).T
    k4 = lambda a: a.reshape(NSA_G, bp, tp, DH)
    o_nsa_p = _nsa_prompt(qt, kcc, vcc, k4(ksh), vst, k4(kwh), vwt, gtt, bias_t, bias_wc,
                          jnp.asarray(c2s_t), bsz=bp, n_cmp=n_cmp_p)

    y_p = _post(xp, o_gla_p.reshape(mp, V_W), o_nsa_p.reshape(mp, D_MODEL), sma, smb, gt1_p, sc2_p, sh2_p, gt2_p,
                n2, fg, wo, wu, wd, tm=tm)

    kv5 = lambda a, b, t: a.reshape(1, b, t, NSA_G, DH)
    wp = min(WIN, tp)
    rows5 = lambda a: jnp.transpose(a, (0, 3, 1, 2))[None]
    p_out = (s_gla_p[None], rows5(kc_t), rows5(vc_t), rows5(ks_t), rows5(vs_t),
             kv5(kw, bp, tp)[:, :, tp - wp:], kv5(vw, bp, tp)[:, :, tp - wp:])

    xs = x_sample.reshape(ms, D_MODEL)
    (gq, gk, gv, sgr, la, kc, vc, ks, vs, kw, vw, gates, sma, smb, nq) = _inproj(
        xs, sc1_s, sh1_s, g1, w_perm, a2p, ab, head_major=False, tm=tm_s)

    t_pad = 8
    pad_t = lambda a: jnp.pad(a.reshape(bs, ts, a.shape[-1]), ((0, 0), (0, t_pad - ts), (0, 0)))
    o_gla_s, s_gla_s = _gla(pad_t(gq), pad_t(gk), pad_t(gv), pad_t(la), pad_t(sgr), gn, state_gla[0], tb=t_pad,
                            n_seq=2 if bs % 2 == 0 else 1)
    o_gla_s = o_gla_s[:, :ts].reshape(ms, V_W)

    pool_t = lambda a: jnp.transpose(a[0], (0, 2, 3, 1))
    new_seg = lambda a: jnp.pad(a.reshape(bs, ts, KV_W), ((0, 0), (0, CMP_STRIDE - ts), (0, 0))).reshape(
        bs, CMP_STRIDE * KV_W)
    n_cmp_s = past_len // CMP_STRIDE
    pps_cmp = min(32, n_pages)
    kcc = _cmpfin(_cmp_paged(pool_t(cache_k_cmp), page_table, _tap_w1(cmp_w1_k[0]), pages_per_step=pps_cmp),
                  _mm(new_seg(kc), cmp_k[0], tm=bs).reshape(bs, 1, -1), *cmp_k[1:3], jnp.transpose(cmp_k[3]),
                  layout='lanes').reshape(bs, KV_W, n_cmp_s)
    vcc = _cmpfin(_cmp_paged(pool_t(cache_v_cmp), page_table, _tap_w1(cmp_w1_v[0]), pages_per_step=pps_cmp),
                  _mm(new_seg(vc), cmp_v[0], tm=bs).reshape(bs, 1, -1), *cmp_v[1:], layout='flat')

    q5 = nq.reshape(bs, ts, NSA_G, NSA_HG, DH)
    q_bd = jnp.einsum('btghd,pg->bghtpd', q5, jnp.eye(NSA_G, dtype=F32)).reshape(bs, NSA_HEADS * ts, KV_W)
    q_bd = q_bd.astype(BF16)

    ncol = NSA_HEADS * ts
    col_tok = np.arange(ncol) % ts

    def bias_rows(key_pos):
        dist = past_len + col_tok[:, None] - key_pos[None, :]
        near = np.nonzero(dist.min(axis=0) < REL_MAX)[0]
        lo = int(near.min()) if near.size else len(key_pos)
        far_part = jnp.broadcast_to(jnp.repeat(f[:, -1], ts)[:, None], (ncol, lo))
        near = jnp.stack([_bias_at(rel_bias, dist[t, lo:]) for t in range(ts)], axis=1)
        return jnp.concatenate([far_part, near.reshape(ncol, -1)], axis=1)

    pps = min(16, n_pages)
    bias_c_s = bias_rows(np.arange(n_cmp_s) * CMP_STRIDE + CMP_BLOCK - 1)
    bias_s_s = jnp.transpose(bias_rows(np.arange(past_len)).reshape(ncol, n_pages // pps, pps * PAGE), (1, 0, 2))
    bias_n_s = bias_rows(past_len + np.arange(PAGE))
    bias_w_s = bias_rows(past_len - WIN + np.arange(WIN))
    n_sel_s = past_len // SEL + 1
    c2s_s = np.zeros((n_cmp_s, -(-n_sel_s // LANES) * LANES), np.float32)
    c2s_s[:, :n_sel_s] = _cmp_to_sel_np(n_cmp_s, n_sel_s)
    gt_col = np.arange(LANES) // (NSA_HG * ts) * ts + np.arange(LANES) % ts
    hsum = ((gt_col[:, None] == np.arange(LANES)[None, :]) & (np.arange(LANES)[:, None] < ncol)).astype(np.float32)
    expand_s = (np.arange(pps * PAGE)[None, :] // SEL == np.arange(pps * PAGE // SEL)[:, None]).astype(np.float32)
    gates_s = jnp.transpose(gates[:, :3 * NSA_HEADS].reshape(bs, ts, 3, NSA_HEADS), (0, 3, 1, 2)).reshape(bs, ncol, 3)
    new_rows = lambda a: jnp.pad(a.reshape(bs, ts, KV_W), ((0, 0), (0, PAGE - ts), (0, 0)))
    wb = cache_k_win.shape[2]
    win_t = lambda a: pool_t(a).reshape(bs, KV_W, wb)
    o_s = _nsa_sample(page_table, pool_t(cache_k_slc), pool_t(cache_v_slc), q_bd, kcc, vcc, new_rows(ks),
                      new_rows(vs), win_t(cache_k_win), win_t(cache_v_win), new_rows(kw),
                      new_rows(vw), gates_s, bias_c_s, bias_s_s, bias_n_s, bias_w_s, jnp.asarray(c2s_s),
                      jnp.asarray(hsum.T), jnp.asarray(hsum.T, dtype=BF16), jnp.asarray(expand_s, dtype=BF16),
                      pps=pps, t_new=ts)
    o_nsa_s = jnp.transpose(o_s.reshape(bs, NSA_HEADS, ts, DH), (0, 2, 1, 3)).reshape(ms, D_MODEL)

    y_s = _post(xs, o_gla_s, o_nsa_s, sma, smb, gt1_s, sc2_s, sh2_s, gt2_s, n2, fg, wo, wu, wd, tm=tm_s)

    kw_new = jnp.concatenate([cache_k_win[0], kv5(kw, bs, ts)[0]], axis=1)[:, ts:][None]
    vw_new = jnp.concatenate([cache_v_win[0], kv5(vw, bs, ts)[0]], axis=1)[:, ts:][None]
    s_out = (s_gla_s[None], kv5(kc, bs, ts), kv5(vc, bs, ts), kv5(ks, bs, ts), kv5(vs, bs, ts), kw_new, vw_new)

    return (y_p.reshape(bp, tp, D_MODEL), y_s.reshape(bs, ts, D_MODEL)) + p_out + s_out
```

```python
import functools
import math

import numpy as np
import jax
import jax.numpy as jnp
from jax import lax
from jax.experimental import pallas as pl
from jax.experimental.pallas import tpu as pltpu

F32 = jnp.float32
BF16 = jnp.bfloat16
HIGHEST = lax.Precision.HIGHEST

D_MODEL = 1024
GLA_HEADS = 4
GLA_DK = 128
GLA_DV = 256
GLA_RANK = 16
GLA_TAU = 16.0
GLA_CHUNK = 64
GLA_SUB = 16
NSA_HEADS = 16
DH = 64
NSA_G = 4
NSA_HG = 4
CMP_BLOCK = 32
CMP_STRIDE = 16
CMP_RATIO = 2
CMP_HID = 128
SEL = 64
SEL_SHIFT = 6
TOPK = 16
WIN = 512
REL_BUCKETS = 32
REL_MAX = 128
D_FF = 4096
EPS = 1e-6
NEG = -1e30
LOG2E = math.log2(math.e)
FORCE = 1e4
PAGE = 128

QK_W = GLA_HEADS * GLA_DK
V_W = GLA_HEADS * GLA_DV
KV_W = NSA_G * DH
LANES = 128
TQ = 128
VMEM_LIMIT = 56 * 1024 * 1024

C_GQ, C_GK, C_GV, C_GR, C_NQ = 0, 512, 1024, 2048, 3072
C_KV = 4096
C_MA, C_MB, C_Z = 5632, 6656, 7680
W_COLS = 7808
Z_GA = 48


def _cparams(sem):
    return pltpu.CompilerParams(dimension_semantics=sem, vmem_limit_bytes=VMEM_LIMIT)


def _const_spec(shape):
    return pl.BlockSpec(shape, lambda *a: (0,) * len(shape), pipeline_mode=pl.Buffered(1))


def _sigmoid(x):
    return 1.0 / (1.0 + jnp.exp(-x))


def _rms(x, g):
    return x * lax.rsqrt(jnp.mean(x * x, axis=-1, keepdims=True) + EPS) * g


def _dot(a, b, **kw):
    return jnp.dot(a, b, preferred_element_type=F32, **kw)


def _dot_nt(a, b):
    return lax.dot_general(a, b, (((1,), (1,)), ((), ())), preferred_element_type=F32)


def _dot_tn(a, b):
    return lax.dot_general(a, b, (((0,), (0,)), ((), ())), preferred_element_type=F32)


def _bucket_np(dist):
    n = np.maximum(dist, 0)
    nf = np.maximum(n, 1).astype(np.float64)
    large = 16 + (np.log(nf / 16.0) / math.log(REL_MAX / 16.0) * 16.0).astype(np.int64)
    large = np.minimum(large, REL_BUCKETS - 1)
    return np.where(n < 16, n, large).astype(np.int32)


def _ada_kernel(c_ref, w_ref, b_ref, o_ref):
    c = c_ref[...]
    o_ref[...] = _dot(c * _sigmoid(c), w_ref[...], precision=HIGHEST) + b_ref[...]


def _ada(c, w, b):
    r, d = c.shape
    n = w.shape[1]
    return pl.pallas_call(
        _ada_kernel,
        out_shape=jax.ShapeDtypeStruct((r, n), F32),
        grid=(n // d,),
        in_specs=[pl.BlockSpec((r, d), lambda j: (0, 0)),
                  pl.BlockSpec((d, d), lambda j: (0, j)),
                  pl.BlockSpec((1, d), lambda j: (0, j))],
        out_specs=pl.BlockSpec((r, d), lambda j: (0, j)),
        compiler_params=_cparams(("arbitrary",)),
        name="ada",
    )(c, w, b.reshape(1, n))


def _inproj_kernel(x_ref, sc_ref, sh_ref, g_ref, w_ref, a2_ref, ab_ref, *outs, head_major):
    (gq_o, gk_o, gv_o, sgr_o, la_o, kc_o, vc_o, ks_o, vs_o, kw_o, vw_o, gt_o, sma_o, smb_o, *rest) = outs
    h = _rms(x_ref[...], g_ref[...]) * (1.0 + sc_ref[0]) + sh_ref[0]
    hb = h.astype(BF16)

    def proj(lo, hi):
        return _dot(hb, w_ref[:, lo:hi])

    gq_o[...] = proj(C_GQ, C_GK) * (GLA_DK ** -0.5)
    gk_o[...] = proj(C_GK, C_GV)
    gv_o[...] = proj(C_GV, C_GR)
    gr = proj(C_GR, C_NQ)
    sgr_o[...] = gr * _sigmoid(gr)
    nq = proj(C_NQ, C_KV) * (DH ** -0.5)
    kv = proj(C_KV, C_MA)
    for n, o in enumerate((kc_o, vc_o, ks_o, vs_o, kw_o, vw_o)):
        o[...] = kv[:, n * KV_W:(n + 1) * KV_W]
    sma_o[...] = _sigmoid(proj(C_MA, C_MB))
    smb_o[...] = _sigmoid(proj(C_MB, C_Z))
    z = proj(C_Z, W_COLS)
    gt_o[...] = _sigmoid(z)
    pre = _dot(z.astype(BF16), a2_ref[...]) + ab_ref[...]
    la_o[...] = (jnp.minimum(pre, 0.0) - jnp.log(1.0 + jnp.exp(-jnp.abs(pre)))) * (1.0 / GLA_TAU)
    if head_major:
        qt_o, ksh_o, vst_o, kwh_o, vwt_o, gtt_o, kct_o, vct_o, kst_o, vstf_o = rest

        def put_transposed(x, *outs):
            for c in range(x.shape[1] // LANES):
                t = jnp.transpose(x[:, c * LANES:(c + 1) * LANES])
                for o, lead in outs:
                    o[lead + (2 * c,)] = t[:DH].astype(o.dtype)
                    o[lead + (2 * c + 1,)] = t[DH:].astype(o.dtype)

        put_transposed(nq * LOG2E, (qt_o, ()))
        put_transposed(kv[:, 0 * KV_W:1 * KV_W], (kct_o, (0,)))
        put_transposed(kv[:, 1 * KV_W:2 * KV_W], (vct_o, (0,)))
        put_transposed(kv[:, 2 * KV_W:3 * KV_W], (kst_o, (0,)))
        put_transposed(kv[:, 3 * KV_W:4 * KV_W], (vst_o, ()), (vstf_o, (0,)))
        put_transposed(kv[:, 5 * KV_W:6 * KV_W], (vwt_o, ()))
        for n, o in ((2, ksh_o), (4, kwh_o)):
            for gg in range(NSA_G):
                lo = n * KV_W + gg * DH
                o[gg] = kv[:, lo:lo + DH].astype(BF16)
        gtt_o[...] = jnp.transpose(_sigmoid(z))
    else:
        (nq_o,) = rest
        nq_o[...] = nq


def _inproj(x, scale, shift, g1, w, a2p, ab, *, head_major, tm, seq=None):
    m = x.shape[0]
    nb, r, _ = scale.shape
    rows_per_mod = m // nb
    mod_spec = pl.BlockSpec((1, r, D_MODEL), lambda i: (i * tm // rows_per_mod, 0, 0))
    row = lambda wdt: pl.BlockSpec((tm, wdt), lambda i: (i, 0))
    widths = [QK_W, QK_W, V_W, V_W, QK_W] + [KV_W] * 6 + [LANES, D_MODEL, D_MODEL]
    out_shape = [jax.ShapeDtypeStruct((m, wd), F32) for wd in widths]
    out_specs = [row(wd) for wd in widths]
    if head_major:
        def rows_major(n):
            out_shape.append(jax.ShapeDtypeStruct((n, m, DH), BF16))
            out_specs.append(pl.BlockSpec((n, tm, DH), lambda i: (0, i, 0)))

        def lanes_major(n):
            out_shape.append(jax.ShapeDtypeStruct((n, DH, m), BF16))
            out_specs.append(pl.BlockSpec((n, DH, tm), lambda i: (0, 0, i)))

        lanes_major(NSA_HEADS)
        rows_major(NSA_G)
        lanes_major(NSA_G)
        rows_major(NSA_G)
        lanes_major(NSA_G)
        out_shape.append(jax.ShapeDtypeStruct((LANES, m), F32))
        out_specs.append(pl.BlockSpec((LANES, tm), lambda i: (0, i)))
        tiles_per_seq = seq // tm
        for _ in range(4):
            out_shape.append(jax.ShapeDtypeStruct((m // seq, NSA_G, DH, seq), F32))
            out_specs.append(pl.BlockSpec((1, NSA_G, DH, tm),
                                          lambda i: (i // tiles_per_seq, 0, 0, i % tiles_per_seq)))
    else:
        out_shape.append(jax.ShapeDtypeStruct((m, D_MODEL), F32))
        out_specs.append(row(D_MODEL))
    return pl.pallas_call(
        functools.partial(_inproj_kernel, head_major=head_major),
        out_shape=out_shape,
        grid=(m // tm,),
        in_specs=[row(D_MODEL), mod_spec, mod_spec, _const_spec((1, D_MODEL)), _const_spec((D_MODEL, W_COLS)),
                  _const_spec((LANES, QK_W)), _const_spec((1, QK_W))],
        out_specs=out_specs,
        compiler_params=_cparams(("parallel",)),
        name="inproj",
    )(x, scale, shift, g1, w, a2p, ab)


def _gla_kernel(*refs, chunk, sub, n_chunk, n_seq, has_init):
    if has_init:
        q_ref, k_ref, v_ref, la_ref, sgr_ref, gn_ref, s0_ref, o_ref, so_ref, s_scr = refs
    else:
        q_ref, k_ref, v_ref, la_ref, sgr_ref, gn_ref, o_ref, so_ref, s_scr = refs
    t = pl.program_id(1)

    @pl.when(t == 0)
    def _():
        s_scr[...] = s0_ref[...] if has_init else jnp.zeros_like(s_scr)

    tril = (lax.broadcasted_iota(jnp.int32, (chunk, chunk), 0)
            >= lax.broadcasted_iota(jnp.int32, (chunk, chunk), 1)).astype(F32)
    sub_row = lax.broadcasted_iota(jnp.int32, (sub, 1), 0)

    def one_chunk(c, carry):
        r0 = pl.multiple_of(c * chunk, chunk)
        rows = pl.ds(r0, chunk)
        for n in range(n_seq):
            b_all = _dot(tril, la_ref[n, rows, :], precision=HIGHEST)
            for h in range(GLA_HEADS):
                one_head(n, h, rows, b_all[:, h * GLA_DK:(h + 1) * GLA_DK])
        return carry

    def one_head(n, h, rows, b):
        ks, vs = slice(h * GLA_DK, (h + 1) * GLA_DK), slice(h * GLA_DV, (h + 1) * GLA_DV)
        q = q_ref[n, rows, ks]
        k = k_ref[n, rows, ks]
        v = v_ref[n, rows, vs]
        s_old = s_scr[n, h]
        vb = v.astype(BF16)
        o_inter = _dot((q * jnp.exp(b)).astype(BF16), s_old.astype(BF16))
        blocks = []
        for blk in range(chunk // sub):
            sl = slice(blk * sub, (blk + 1) * sub)
            q_i, k_i, v_i, b_i = q[sl], k[sl], v[sl], b[sl]
            acc = jnp.zeros((sub, GLA_DV), F32)
            for j in range(sub):
                w = q_i * k_i[j:j + 1] * jnp.exp(jnp.minimum(b_i - b_i[j:j + 1], 0.0))
                a = jnp.sum(w, axis=-1, keepdims=True)
                acc = acc + jnp.where(sub_row >= j, a, 0.0) * v_i[j:j + 1]
            if blk > 0:
                prev = blk * sub
                ref_row = b_i[0:1]
                qp = (q_i * jnp.exp(b_i - ref_row)).astype(BF16)
                kp = (k[:prev] * jnp.exp(ref_row - b[:prev])).astype(BF16)
                acc = acc + _dot(_dot_nt(qp, kp).astype(BF16), vb[:prev])
            blocks.append(acc)
        o = o_inter + (jnp.concatenate(blocks, axis=0) if len(blocks) > 1 else blocks[0])
        b_last = b[chunk - 1:chunk]
        kp = (k * jnp.exp(b_last - b)).astype(BF16)
        decay_col = jnp.transpose(jnp.broadcast_to(jnp.exp(b_last), (GLA_DK, GLA_DK)))[:, 0:1]
        s_scr[n, h] = decay_col * s_old + _dot_tn(kp, vb)
        o_ref[n, rows, vs] = _rms(o, gn_ref[...]) * sgr_ref[n, rows, vs]

    lax.fori_loop(0, n_chunk, one_chunk, 0)

    @pl.when(t == pl.num_programs(1) - 1)
    def _():
        so_ref[...] = s_scr[...]


def _gla(q, k, v, la, sgr, gn, s0, *, tb, n_seq):
    bsz, t, _ = q.shape
    chunk = min(GLA_CHUNK, tb)
    sub = min(GLA_SUB, chunk)
    blk = lambda wd: pl.BlockSpec((n_seq, tb, wd), lambda b, i: (b, i, 0))
    st_spec = pl.BlockSpec((n_seq, GLA_HEADS, GLA_DK, GLA_DV), lambda b, i: (b, 0, 0, 0))
    in_specs = [blk(QK_W), blk(QK_W), blk(V_W), blk(QK_W), blk(V_W),
                pl.BlockSpec((1, GLA_DV), lambda b, i: (0, 0))]
    args = [q, k, v, la, sgr, gn]
    if s0 is not None:
        in_specs.append(st_spec)
        args.append(s0)
    return pl.pallas_call(
        functools.partial(_gla_kernel, chunk=chunk, sub=sub, n_chunk=tb // chunk, n_seq=n_seq,
                          has_init=s0 is not None),
        out_shape=[jax.ShapeDtypeStruct((bsz, t, V_W), F32),
                   jax.ShapeDtypeStruct((bsz, GLA_HEADS, GLA_DK, GLA_DV), F32)],
        grid=(bsz // n_seq, t // tb),
        in_specs=in_specs,
        out_specs=[blk(V_W), st_spec],
        scratch_shapes=[pltpu.VMEM((n_seq, GLA_HEADS, GLA_DK, GLA_DV), F32)],
        compiler_params=_cparams(("parallel", "arbitrary")),
        name="gla",
    )(*args)


def _mm_kernel(x_ref, w_ref, o_ref):
    o_ref[...] = _dot(x_ref[...].astype(BF16), w_ref[...])


def _mm(x, w, *, tm):
    m, kdim = x.shape
    n = w.shape[1]
    return pl.pallas_call(
        _mm_kernel,
        out_shape=jax.ShapeDtypeStruct((m, n), F32),
        grid=(m // tm,),
        in_specs=[pl.BlockSpec((tm, kdim), lambda i: (i, 0)), _const_spec((kdim, n))],
        out_specs=pl.BlockSpec((tm, n), lambda i: (i, 0)),
        compiler_params=_cparams(("parallel",)),
        name="cmp_mm",
    )(x, w)


CMP_TAPS = 4
SUBLANES = 8


def _odd_pitch(n):
    tiles = -(-n // SUBLANES)
    return (tiles + 1 - tiles % 2) * SUBLANES


def _cmp_paged_kernel(*refs, pps):
    pages = refs[1:1 + pps]
    w_ref, o_ref, tok_s = refs[1 + pps:]
    heads_per_slab = LANES // DH
    n_slab = NSA_G // heads_per_slab
    n_seg = pps * PAGE // CMP_STRIDE
    pitch = tok_s.shape[1] // CMP_STRIDE
    segs_per_page = PAGE // CMP_STRIDE
    for p in range(pps):
        t = jnp.transpose(pages[p][0].reshape(KV_W, PAGE))
        for s in range(segs_per_page):
            for sl in range(n_slab):
                tok_s[sl, pl.ds(p * segs_per_page + s, CMP_STRIDE, stride=pitch), :] = (
                    t[s * CMP_STRIDE:(s + 1) * CMP_STRIDE, sl * LANES:(sl + 1) * LANES])
    width = CMP_RATIO * CMP_HID
    first_half = lax.broadcasted_iota(jnp.int32, (1, LANES), 1) < DH
    for sl in range(n_slab):
        x = [tok_s[sl, e * pitch:e * pitch + n_seg, :] for e in range(CMP_STRIDE)]
        r = [pltpu.roll(v, DH, 1) for v in x]
        for odd in range(heads_per_slab):
            g = sl * heads_per_slab + odd
            acc = None
            for j in range(CMP_STRIDE // CMP_TAPS):
                pieces = []
                for e in range(j * CMP_TAPS, (j + 1) * CMP_TAPS, 2):
                    lo, hi = (r[e], x[e + 1]) if odd else (x[e], r[e + 1])
                    pieces.append(jnp.where(first_half, lo, hi))
                term = _dot(jnp.concatenate(pieces, axis=1).astype(BF16), w_ref[j])
                acc = term if acc is None else acc + term
            o_ref[0, :, g * width:(g + 1) * width] = acc


def _cmp_paged(pool_t, page_table, w_taps, *, pages_per_step):
    bsz, n_pages = page_table.shape
    pps = pages_per_step
    n_seg = pps * PAGE // CMP_STRIDE
    width = NSA_G * CMP_RATIO * CMP_HID

    def page_spec(j):
        return pl.BlockSpec((1, NSA_G, DH, PAGE), lambda b, s, pt: (pt[b, s * pps + j], 0, 0, 0))

    grid_spec = pltpu.PrefetchScalarGridSpec(
        num_scalar_prefetch=1,
        grid=(bsz, n_pages // pps),
        in_specs=[page_spec(j) for j in range(pps)]
        + [pl.BlockSpec(w_taps.shape, lambda b, s, pt: (0, 0, 0), pipeline_mode=pl.Buffered(1))],
        out_specs=pl.BlockSpec((1, n_seg, width), lambda b, s, pt: (b, s, 0)),
        scratch_shapes=[pltpu.VMEM((NSA_G * DH // LANES, CMP_STRIDE * _odd_pitch(n_seg), LANES), F32)],
    )
    return pl.pallas_call(
        functools.partial(_cmp_paged_kernel, pps=pps),
        out_shape=jax.ShapeDtypeStruct((bsz, n_pages * PAGE // CMP_STRIDE, width), F32),
        grid_spec=grid_spec,
        compiler_params=_cparams(("parallel", "arbitrary")),
        name="cmp_paged",
    )(page_table, *([pool_t] * pps), w_taps)


def _cmpfin_kernel(*refs, n, has_new, layout):
    if has_new:
        p_ref, pn_ref, pe_ref, w1_ref, w2_ref, o_ref = refs
    else:
        p_ref, pe_ref, w1_ref, w2_ref, o_ref = refs
    pe_term = _dot(pe_ref[...], w1_ref[...], precision=HIGHEST)[0:1]
    w2 = w2_ref[...].astype(BF16)
    rowi = lax.broadcasted_iota(jnp.int32, (n, 1), 0)
    for g in range(NSA_G):
        base = g * CMP_RATIO * CMP_HID
        first = p_ref[0, :, base:base + CMP_HID]
        second = pltpu.roll(p_ref[0, :, base + CMP_HID:base + 2 * CMP_HID], n - 1, 0)
        if has_new:
            second = jnp.where(rowi == n - 1, pn_ref[0, :, base + CMP_HID:base + 2 * CMP_HID], second)
        acc = first + second + pe_term
        hid = (acc * _sigmoid(acc)).astype(BF16)
        if layout == 'flat':
            o_ref[0, :, g * DH:(g + 1) * DH] = _dot(hid, w2)
        elif layout == 'rows':
            o_ref[0, g] = _dot(hid, w2)
        else:
            o_ref[0, g] = _dot_nt(w2, hid)


def _cmpfin(parts, parts_new, pe, w1, w2, *, layout):
    bsz, n, width = parts.shape
    in_specs = [pl.BlockSpec((1, n, width), lambda b: (b, 0, 0))]
    args = [parts]
    if parts_new is not None:
        in_specs.append(pl.BlockSpec((1, 1, width), lambda b: (b, 0, 0)))
        args.append(parts_new)
    in_specs += [_const_spec(pe.shape), _const_spec(w1.shape), _const_spec(w2.shape)]
    if layout == 'flat':
        out_shape = jax.ShapeDtypeStruct((bsz, n, KV_W), F32)
        out_spec = pl.BlockSpec((1, n, KV_W), lambda b: (b, 0, 0))
    else:
        inner = (n, DH) if layout == 'rows' else (DH, n)
        out_shape = jax.ShapeDtypeStruct((bsz, NSA_G) + inner, F32)
        out_spec = pl.BlockSpec((1, NSA_G) + inner, lambda b: (b, 0, 0, 0))
    return pl.pallas_call(
        functools.partial(_cmpfin_kernel, n=n, has_new=parts_new is not None, layout=layout),
        out_shape=out_shape,
        grid=(bsz,),
        in_specs=in_specs,
        out_specs=out_spec,
        compiler_params=_cparams(("parallel",)),
        name="cmp_finish",
    )(*args, pe, w1, w2)


def _masked_softmax(s, mask, axis, exp=jnp.exp):
    s = jnp.where(mask, s, NEG)
    m = jnp.max(s, axis=axis, keepdims=True)
    e = jnp.where(mask, exp(s - m), 0.0)
    return e / jnp.maximum(jnp.sum(e, axis=axis, keepdims=True), 1e-30)


CMP_WIN_ROWS = 24
CMP_WIN_PAD = 16
FAR_TILES = 4


def _nsa_prompt_kernel(qt_ref, kc_ref, vct_ref, ks_ref, vst_ref, kw_ref, vwt_ref, gtt_ref, bt_ref, wc_ref,
                       c2s_ref, o_ref, sel_s, bc_s, m_s, l_s, acc_s, *, n_cmp, n_sel, n_tiles):
    i = pl.program_id(1)
    tok = i * TQ + lax.broadcasted_iota(jnp.int32, (1, TQ), 1)
    tok4 = jnp.concatenate([tok] * NSA_HG, axis=1)
    row = lax.broadcasted_iota(jnp.int32, (TQ, 1), 0)
    groups = range(NSA_G)
    q4 = [jnp.concatenate([qt_ref[g * NSA_HG + h] for h in range(NSA_HG)], axis=1) for g in groups]

    def tile4(x):
        return jnp.concatenate([x] * NSA_HG, axis=1)

    o_cmp = []
    for g in groups:
        bc_s[g] = jnp.zeros(bc_s.shape[1:], F32)
        bc_s[g, pl.ds(pl.multiple_of(i * (TQ // CMP_STRIDE), TQ // CMP_STRIDE), CMP_WIN_ROWS), :] = wc_ref[g]
        s_c = _dot(kc_ref[0, g].astype(BF16), q4[g]) + bc_s[g, CMP_WIN_PAD:CMP_WIN_PAD + TQ, :]
        mask_c = (tok4 - (row * CMP_STRIDE + CMP_BLOCK - 1) >= 0) & (row < n_cmp)
        p_c = _masked_softmax(s_c, mask_c, 0, exp=jnp.exp2)
        o_cmp.append(_dot(vct_ref[0, g].astype(BF16), p_c.astype(BF16)))
        p_sum = p_c[:, 0:TQ] + p_c[:, TQ:2 * TQ] + p_c[:, 2 * TQ:3 * TQ] + p_c[:, 3 * TQ:4 * TQ]
        imp = _dot(c2s_ref[...], p_sum, precision=HIGHEST)
        blk = lax.broadcasted_iota(jnp.int32, (n_sel, 1), 0)
        cur = jnp.right_shift(tok, SEL_SHIFT)
        forced = (blk == 0) | (blk == cur) | (blk == cur - 1)
        valid = blk * SEL <= tok
        score = jnp.where(forced, FORCE, jnp.where(valid, imp, -FORCE))
        cnt = jnp.zeros((n_sel, TQ), jnp.int32)
        for b in range(n_sel):
            s_b = score[b:b + 1]
            ahead = (s_b > score) | ((s_b == score) & (blk > b))
            cnt = cnt + ahead.astype(jnp.int32)
        sel_s[g] = jnp.where(cnt < min(TOPK, n_sel), 1.0, 0.0)

    def tile_scores(k_ref, g, kt):
        k0 = pl.multiple_of(jnp.clip(kt, 0, n_tiles - 1) * TQ, TQ)
        return _dot(k_ref[g, 0, pl.ds(k0, TQ), :], q4[g]), k0

    def sel_mask(g, kt, valid):
        blocks_per_tile = TQ // SEL
        first = jnp.clip(kt, 0, n_tiles - 1) * blocks_per_tile
        out = None
        for j in reversed(range(blocks_per_tile)):
            add_j = jnp.where((sel_s[g, pl.ds(first + j, 1), :] > 0.5) & valid, 0.0, NEG)
            out = add_j if out is None else jnp.where(row < (j + 1) * SEL, add_j, out)
        return out

    def update(state, tiles):
        m_run, l_run, acc = state
        m_new = m_run
        for s, _ in tiles:
            m_new = jnp.maximum(m_new, jnp.max(s, axis=0, keepdims=True))
        alpha = jnp.exp2(m_run - m_new)
        l_new = alpha * l_run
        acc = alpha * acc
        for s, vt in tiles:
            p = jnp.exp2(s - m_new)
            l_new = l_new + jnp.sum(p, axis=0, keepdims=True)
            acc = acc + _dot(vt, p.astype(BF16))
        return m_new, l_new, acc

    def near_tiles(k_ref, vt_ref, g, mask_fn):
        tiles = []
        for typ, kt in ((1, i - 1), (0, i)):
            s, k0 = tile_scores(k_ref, g, kt)
            madd = mask_fn(kt, tok - (kt * TQ + row))
            tiles.append((s + bt_ref[typ, g] + tile4(madd), vt_ref[g, :, pl.ds(k0, TQ)]))
        return tiles

    cols = NSA_HG * TQ
    empty = (jnp.full((1, cols), NEG, F32), jnp.zeros((1, cols), F32), jnp.zeros((DH, cols), F32))

    def normalised(state):
        return state[2] / jnp.maximum(state[1], 1e-30)

    for g in groups:
        m_s[g], l_s[g], acc_s[g] = empty
    n_far = jnp.maximum(i - 1, 0)

    def far_body(c, carry):
        for g in groups:
            tiles = []
            for u in range(FAR_TILES):
                kt = FAR_TILES * c + u
                s, k0 = tile_scores(ks_ref, g, kt)
                tiles.append((s + tile4(sel_mask(g, kt, kt < n_far)), vst_ref[g, :, pl.ds(k0, TQ)]))
            m_s[g], l_s[g], acc_s[g] = update((m_s[g], l_s[g], acc_s[g]), tiles)
        return carry

    lax.fori_loop(0, (n_far + FAR_TILES - 1) // FAR_TILES, far_body, 0)

    heads = []
    for g in groups:
        slc_near = near_tiles(ks_ref, vst_ref, g,
                              lambda kt, dist: jnp.where(dist >= 0, sel_mask(g, kt, kt >= 0), NEG))
        o_slc = normalised(update((m_s[g], l_s[g], acc_s[g]), slc_near))

        tiles = []
        for back in range(WIN // TQ, 1, -1):
            kt = i - back
            s, k0 = tile_scores(kw_ref, g, kt)
            if back == WIN // TQ:
                dist = tok - (kt * TQ + row)
                s = s + tile4(jnp.where((dist < WIN) & (kt >= 0), 0.0, NEG))
            else:
                s = s + jnp.where(kt >= 0, 0.0, NEG)
            tiles.append((s, vwt_ref[g, :, pl.ds(k0, TQ)]))
        win_near = near_tiles(kw_ref, vwt_ref, g, lambda kt, dist: jnp.where((dist >= 0) & (kt >= 0), 0.0, NEG))
        o_win = normalised(update(update(empty, tiles), win_near))

        for h in range(NSA_HG):
            cs = slice(h * TQ, (h + 1) * TQ)
            o_h = jnp.zeros((DH, TQ), F32)
            for br, o_br in enumerate((o_cmp[g], o_slc, o_win)):
                gate_row = br * NSA_HEADS + g * NSA_HG + h
                o_h = o_h + gtt_ref[gate_row:gate_row + 1, :] * o_br[:, cs]
            heads.append(o_h)
    o_ref[0] = jnp.transpose(jnp.concatenate(heads, axis=0))


def _nsa_prompt(qt, kc, vct, ksh, vst, kwh, vwt, gtt, bias_t, bias_wc, c2s_t, *, bsz, n_cmp):
    t = qt.shape[2] // bsz
    nq = t // TQ
    n_sel = t // SEL
    cols = NSA_HG * TQ
    k_spec = pl.BlockSpec((NSA_G, 1, t, DH), lambda b, i: (0, b, 0, 0))
    vt_spec = pl.BlockSpec((NSA_G, DH, t), lambda b, i: (0, 0, b))
    const = lambda arr: pl.BlockSpec(arr.shape, lambda b, i: (0,) * arr.ndim)
    return pl.pallas_call(
        functools.partial(_nsa_prompt_kernel, n_cmp=n_cmp, n_sel=n_sel, n_tiles=nq),
        out_shape=jax.ShapeDtypeStruct((bsz, t, D_MODEL), F32),
        grid=(bsz, nq),
        in_specs=[pl.BlockSpec((NSA_HEADS, DH, TQ), lambda b, i: (0, 0, b * nq + i)),
                  pl.BlockSpec((1, NSA_G, kc.shape[2], DH), lambda b, i: (b, 0, 0, 0)),
                  pl.BlockSpec((1, NSA_G, DH, vct.shape[3]), lambda b, i: (b, 0, 0, 0)),
                  k_spec, vt_spec, k_spec, vt_spec,
                  pl.BlockSpec((LANES, TQ), lambda b, i: (0, b * nq + i)),
                  const(bias_t), const(bias_wc), const(c2s_t)],
        out_specs=pl.BlockSpec((1, TQ, D_MODEL), lambda b, i: (b, i, 0)),
        scratch_shapes=[pltpu.VMEM((NSA_G, n_sel, TQ), F32), pltpu.VMEM((NSA_G, CMP_WIN_PAD + TQ, cols), F32),
                        pltpu.VMEM((NSA_G, 1, cols), F32), pltpu.VMEM((NSA_G, 1, cols), F32),
                        pltpu.VMEM((NSA_G, DH, cols), F32)],
        compiler_params=_cparams(("parallel", "arbitrary")),
        name="nsa_prompt",
    )(qt, kc, vct, ksh, vst, kwh, vwt, gtt, bias_t, bias_wc, c2s_t)


def _nsa_sample_kernel(*refs, pps, n_cmp, n_sel, n_sel_pad, past_len, t_new):
    k_pages = refs[1:1 + pps]
    v_pages = refs[1 + pps:1 + 2 * pps]
    (q_ref, kct_ref, vc_ref, kn_ref, vn_ref, kwt_ref, vwt_ref, kwn_ref, vwn_ref, gt_ref, bc_ref, bs_ref, bn_ref,
     bw_ref, c2s_ref, hsum_ref, hexp_ref, e_ref, o_ref,
     sel_s, m_s, l_s, acc_s, ocmp_s, owin_s) = refs[1 + 2 * pps:]
    ch = pl.program_id(1)
    nrow = NSA_HEADS * t_new
    q = q_ref[0]
    rtok = lax.broadcasted_iota(jnp.int32, (nrow, 1), 0) & (t_new - 1)
    qpos = past_len + rtok
    lane = lax.broadcasted_iota(jnp.int32, (1, LANES), 1)

    def to_col(row_vec):
        return jnp.transpose(jnp.broadcast_to(row_vec, (LANES, LANES)))[:nrow, 0:1]

    @pl.when(ch == 0)
    def _():
        c_end = lax.broadcasted_iota(jnp.int32, (1, n_cmp), 1) * CMP_STRIDE + CMP_BLOCK - 1
        s_c = _dot(q, kct_ref[0].astype(BF16)) + bc_ref[...]
        p_c = _masked_softmax(s_c, qpos - c_end >= 0, -1)
        ocmp_s[...] = _dot(p_c.astype(BF16), vc_ref[0].astype(BF16))

        n_gt = NSA_G * t_new
        p_sum = _dot(hsum_ref[...], jnp.concatenate([p_c, jnp.zeros((LANES - nrow, n_cmp), F32)], axis=0),
                     precision=HIGHEST)[:n_gt]
        imp = _dot(p_sum, c2s_ref[...], precision=HIGHEST)
        n_blk_lanes = imp.shape[1]
        blk = lax.broadcasted_iota(jnp.int32, (1, n_blk_lanes), 1)
        gpos = past_len + (lax.broadcasted_iota(jnp.int32, (n_gt, 1), 0) & (t_new - 1))
        cur = jnp.right_shift(gpos, SEL_SHIFT)
        forced = (blk == 0) | (blk == cur) | (blk == cur - 1)
        valid = blk * SEL <= gpos
        score = jnp.where(forced, FORCE, jnp.where(valid, imp, -FORCE))
        score = jnp.where(blk < n_sel, score, -3.0 * FORCE)
        cnt = jnp.zeros((n_gt, n_blk_lanes), jnp.int32)
        for b in range(n_sel):
            s_b = score[:, b:b + 1]
            ahead = (s_b > score) | ((s_b == score) & (blk > b))
            cnt = cnt + ahead.astype(jnp.int32)
        sel = jnp.where((cnt < min(TOPK, n_sel)) & (blk < n_sel), 1.0, 0.0)
        sel_t = jnp.transpose(jnp.concatenate([sel, jnp.zeros((LANES - n_gt, n_blk_lanes), F32)], axis=0))
        sel_s[...] = _dot(sel_t[:n_sel_pad].astype(BF16), hexp_ref[...])

        key_w = past_len - WIN + lax.broadcasted_iota(jnp.int32, (1, WIN), 1)
        s_w = _dot(q, kwt_ref[0].astype(BF16)) + bw_ref[...]
        mask_w = lax.bitcast_convert_type(qpos - key_w, jnp.uint32) < WIN
        s_n = _dot_nt(q, kwn_ref[0].astype(BF16)) + bn_ref[...]
        mask_n = (lane < t_new) & (rtok - lane >= 0)
        s_w = jnp.where(mask_w, s_w, NEG)
        s_n = jnp.where(mask_n, s_n, NEG)
        m = jnp.maximum(jnp.max(s_w, axis=-1, keepdims=True), jnp.max(s_n, axis=-1, keepdims=True))
        e_w = jnp.where(mask_w, jnp.exp(s_w - m), 0.0)
        e_n = jnp.where(mask_n, jnp.exp(s_n - m), 0.0)
        den = jnp.maximum(jnp.sum(e_w, axis=-1, keepdims=True) + jnp.sum(e_n, axis=-1, keepdims=True), 1e-30)
        owin_s[...] = (_dot_nt((e_w / den).astype(BF16), vwt_ref[0].astype(BF16))
                       + _dot((e_n / den).astype(BF16), vwn_ref[0].astype(BF16)))

        s_t = _dot_nt(q, kn_ref[0].astype(BF16)) + bn_ref[...]
        mask_t = mask_n & (to_col(sel_s[pl.ds(past_len // SEL, 1), :]) > 0.5)
        s_t = jnp.where(mask_t, s_t, NEG)
        m0 = jnp.max(s_t, axis=-1, keepdims=True)
        p0 = jnp.where(mask_t, jnp.exp(s_t - m0), 0.0)
        m_s[...] = m0
        l_s[...] = jnp.sum(p0, axis=-1, keepdims=True)
        acc_s[...] = _dot(p0.astype(BF16), vn_ref[0].astype(BF16))

    kt_all = jnp.concatenate([r[0].reshape(KV_W, PAGE) for r in k_pages], axis=1).astype(BF16)
    vt_all = jnp.concatenate([r[0].reshape(KV_W, PAGE) for r in v_pages], axis=1).astype(BF16)
    s = _dot(q, kt_all) + bs_ref[0]
    blocks = pps * PAGE // SEL
    sel_blk = sel_s[pl.ds(pl.multiple_of(ch * blocks, blocks), blocks), :][:, :nrow].astype(BF16)
    mask = _dot_tn(sel_blk, e_ref[...]) > 0.5
    s = jnp.where(mask, s, NEG)
    m_old = m_s[...]
    m_new = jnp.maximum(m_old, jnp.max(s, axis=-1, keepdims=True))
    p = jnp.where(mask, jnp.exp(s - m_new), 0.0)
    alpha = jnp.exp(m_old - m_new)
    l_s[...] = alpha * l_s[...] + jnp.sum(p, axis=-1, keepdims=True)
    acc_s[...] = alpha * acc_s[...] + _dot_nt(p.astype(BF16), vt_all)
    m_s[...] = m_new

    @pl.when(ch == pl.num_programs(1) - 1)
    def _():
        o_slc = acc_s[...] / jnp.maximum(l_s[...], 1e-30)
        gates = gt_ref[0]
        hg_rows = NSA_HG * t_new
        for g in range(NSA_G):
            rs = slice(g * hg_rows, (g + 1) * hg_rows)
            cs = slice(g * DH, (g + 1) * DH)
            o_ref[0, rs, :] = (gates[rs, 0:1] * ocmp_s[rs, cs] + gates[rs, 1:2] * o_slc[rs, cs]
                               + gates[rs, 2:3] * owin_s[rs, cs])


def _nsa_sample(page_table, k_pool, v_pool, q_bd, kct, vc, k_new, v_new, kw_cache, vw_cache, kw_new, vw_new, gates,
                bias_c, bias_s, bias_n, bias_w, c2s, hsum_t, hexp, expand, *, pps, t_new):
    bsz, n_pages = page_table.shape
    past_len = n_pages * PAGE
    n_cmp = vc.shape[1]
    n_sel = past_len // SEL + 1
    n_sel_pad = -(-n_sel // 8) * 8
    nrow = NSA_HEADS * t_new

    def page_spec(j):
        return pl.BlockSpec((1, NSA_G, DH, PAGE), lambda b, s, pt: (pt[b, s * pps + j], 0, 0, 0))

    per_b = lambda shape: pl.BlockSpec((1,) + shape, lambda b, s, pt: (b,) + (0,) * len(shape))
    const = lambda arr: pl.BlockSpec(arr.shape, lambda b, s, pt: (0,) * arr.ndim)
    in_specs = ([page_spec(j) for j in range(pps)] * 2
                + [per_b((nrow, KV_W)), per_b((KV_W, n_cmp)), per_b((n_cmp, KV_W)),
                   per_b((PAGE, KV_W)), per_b((PAGE, KV_W)), per_b((KV_W, WIN)), per_b((KV_W, WIN)),
                   per_b((PAGE, KV_W)), per_b((PAGE, KV_W)), per_b((nrow, 3)),
                   const(bias_c),
                   pl.BlockSpec((1, nrow, pps * PAGE), lambda b, s, pt: (s, 0, 0)),
                   const(bias_n), const(bias_w), const(c2s), const(hsum_t), const(hexp), const(expand)])
    grid_spec = pltpu.PrefetchScalarGridSpec(
        num_scalar_prefetch=1,
        grid=(bsz, n_pages // pps),
        in_specs=in_specs,
        out_specs=pl.BlockSpec((1, nrow, DH), lambda b, s, pt: (b, 0, 0)),
        scratch_shapes=[pltpu.VMEM((n_sel_pad, LANES), F32),
                        pltpu.VMEM((nrow, 1), F32), pltpu.VMEM((nrow, 1), F32),
                        pltpu.VMEM((nrow, KV_W), F32), pltpu.VMEM((nrow, KV_W), F32),
                        pltpu.VMEM((nrow, KV_W), F32)],
    )
    return pl.pallas_call(
        functools.partial(_nsa_sample_kernel, pps=pps, n_cmp=n_cmp, n_sel=n_sel, n_sel_pad=n_sel_pad,
                          past_len=past_len, t_new=t_new),
        out_shape=jax.ShapeDtypeStruct((bsz, nrow, DH), F32),
        grid_spec=grid_spec,
        compiler_params=_cparams(("parallel", "arbitrary")),
        name="nsa_sample",
    )(page_table, *([k_pool] * pps), *([v_pool] * pps), q_bd, kct, vc, k_new, v_new, kw_cache, vw_cache, kw_new,
      vw_new, gates, bias_c, bias_s, bias_n, bias_w, c2s, hsum_t, hexp, expand)


def _post_kernel(x_ref, og_ref, on_ref, sma_ref, smb_ref, gate1_ref, sc2_ref, sh2_ref, gate2_ref, n2_ref, fg_ref,
                 wo_ref, wu_ref, wd_ref, y_ref):
    merged = sma_ref[...] * og_ref[...] + smb_ref[...] * on_ref[...]
    x1 = x_ref[...] + gate1_ref[0] * _dot(merged.astype(BF16), wo_ref[...])
    h2 = _rms(x1, n2_ref[...]) * (1.0 + sc2_ref[0]) + sh2_ref[0]
    up = jnp.maximum(_dot(h2.astype(BF16), wu_ref[...]), 0.0)
    x2 = x1 + gate2_ref[0] * _dot((up * up).astype(BF16), wd_ref[...])
    y_ref[...] = _rms(x2, fg_ref[...])


def _post(x, o_gla, o_nsa, sma, smb, gate1, scale2, shift2, gate2, n2, fg, wo, wu, wd, *, tm):
    m = x.shape[0]
    nb, r, _ = gate1.shape
    rows_per_mod = m // nb
    mod_spec = pl.BlockSpec((1, r, D_MODEL), lambda i: (i * tm // rows_per_mod, 0, 0))
    row = pl.BlockSpec((tm, D_MODEL), lambda i: (i, 0))
    return pl.pallas_call(
        _post_kernel,
        out_shape=jax.ShapeDtypeStruct((m, D_MODEL), F32),
        grid=(m // tm,),
        in_specs=[row] * 5 + [mod_spec] * 4 + [_const_spec((1, D_MODEL))] * 2
        + [_const_spec(wo.shape), _const_spec(wu.shape), _const_spec(wd.shape)],
        out_specs=row,
        compiler_params=_cparams(("parallel",)),
        name="post",
    )(x, o_gla, o_nsa, sma, smb, gate1, scale2, shift2, gate2, n2, fg, wo, wu, wd)


def _block_diag_w1(w1):
    w = w1.reshape(CMP_RATIO, CMP_STRIDE, DH, CMP_HID)
    eye = jnp.eye(NSA_G, dtype=w1.dtype)
    bd = jnp.einsum('redh,pg->epdgrh', w, eye)
    return bd.reshape(CMP_STRIDE * NSA_G * DH, NSA_G * CMP_RATIO * CMP_HID).astype(BF16)


def _tap_w1(w1):
    w = w1.reshape(CMP_RATIO, CMP_STRIDE // CMP_TAPS, CMP_TAPS, DH, CMP_HID)
    return jnp.transpose(w, (1, 2, 3, 0, 4)).reshape(CMP_STRIDE // CMP_TAPS, CMP_TAPS * DH,
                                                     CMP_RATIO * CMP_HID).astype(BF16)


def _cmp_to_sel_np(n_cmp, n_sel):
    cs = np.arange(n_cmp)[:, None] * CMP_STRIDE
    ss = np.arange(n_sel)[None, :] * SEL
    ov = np.minimum(cs + CMP_BLOCK, ss + SEL) - np.maximum(cs, ss)
    return np.clip(ov, 0, None).astype(np.float32) / CMP_BLOCK


def _bias_at(rel_bias, dist):
    onehot = (_bucket_np(dist)[..., None] == np.arange(REL_BUCKETS)).astype(np.float32)
    return jnp.moveaxis(jnp.dot(jnp.asarray(onehot), rel_bias, precision=HIGHEST), -1, 0)


def _bias_by_dist(rel_bias, n):
    return _bias_at(rel_bias, np.arange(n))


def _toeplitz(v, n):
    lead = v.shape[:-1]
    a = jnp.broadcast_to(v[..., None, :], lead + (n, 2 * n)).reshape(lead + (2 * n * n,))
    a = a[..., :n * (2 * n - 1)].reshape(lead + (n, 2 * n - 1))
    return a[..., n - 1:]


def _mod_split(mod, rows):
    parts = jnp.split(mod, 6, axis=-1)
    return [p.reshape(-1, rows, D_MODEL) for p in parts]


def kernel(x_prompt, x_sample, c_prompt, c_sample, state_gla, cache_k_cmp, cache_v_cmp, cache_k_slc, cache_v_slc, cache_k_win, cache_v_win, page_table, ada_w, ada_b, norm1_g, norm2_g, w_in, gla_a2, gla_a_b, gla_norm_g, cmp_pe_k, cmp_w1_k, cmp_w2_k, cmp_pe_v, cmp_w1_v, cmp_w2_v, w_o, w_up, w_down, rel_bias, final_g):
    depth = ada_w.shape[0]
    assert depth == 1, "single-layer trunk"
    bp, tp, _ = x_prompt.shape
    bs, ts, _ = x_sample.shape
    n_pages = page_table.shape[1]
    past_len = n_pages * PAGE
    mp, ms = bp * tp, bs * ts
    tm = 256
    tm_s = min(tm, ms)

    w = w_in[0]
    w_perm = jnp.concatenate([w[:, :3072], w[:, 3088:5648], w[:, 5696:7744], w[:, 5648:5696], w[:, 3072:3088],
                              jnp.zeros((D_MODEL, LANES - 64), w.dtype)], axis=1).astype(BF16)
    a2p = jnp.zeros((LANES, QK_W), F32).at[Z_GA:Z_GA + GLA_RANK].set(gla_a2[0]).astype(BF16)
    ab = gla_a_b[0].reshape(1, QK_W)
    g1 = norm1_g[0].reshape(1, D_MODEL)
    n2 = norm2_g[0].reshape(1, D_MODEL)
    fg = final_g.reshape(1, D_MODEL)
    gn = gla_norm_g[0].reshape(1, GLA_DV)
    wo, wu, wd = w_o[0].astype(BF16), w_up[0].astype(BF16), w_down[0].astype(BF16)
    cmp_k = (_block_diag_w1(cmp_w1_k[0]), jnp.broadcast_to(cmp_pe_k[0].reshape(1, -1), (8, CMP_BLOCK * DH)),
             cmp_w1_k[0].reshape(CMP_BLOCK * DH, CMP_HID), cmp_w2_k[0])
    cmp_v = (_block_diag_w1(cmp_w1_v[0]), jnp.broadcast_to(cmp_pe_v[0].reshape(1, -1), (8, CMP_BLOCK * DH)),
             cmp_w1_v[0].reshape(CMP_BLOCK * DH, CMP_HID), cmp_w2_v[0])

    mod = _ada(jnp.concatenate([c_prompt, c_sample], axis=0), ada_w[0], ada_b[0])
    sh1_p, sc1_p, gt1_p, sh2_p, sc2_p, gt2_p = _mod_split(mod[:bp], 1)
    sh1_s, sc1_s, gt1_s, sh2_s, sc2_s, gt2_s = _mod_split(jnp.repeat(mod[bp:], ts, axis=0), tm_s)

    xp = x_prompt.reshape(mp, D_MODEL)
    (gq, gk, gv, sgr, la, kc, vc, _, _, kw, vw, _, sma, smb, qt, ksh, vst, kwh, vwt, gtt,
     kc_t, vc_t, ks_t, vs_t) = _inproj(xp, sc1_p, sh1_p, g1, w_perm, a2p, ab, head_major=True, tm=tm, seq=tp)

    r3 = lambda a: a.reshape(bp, tp, a.shape[-1])
    o_gla_p, s_gla_p = _gla(r3(gq), r3(gk), r3(gv), r3(la), r3(sgr), gn, None, tb=256,
                            n_seq=2 if bp % 2 == 0 else 1)

    n_seg = tp // CMP_STRIDE
    assert n_seg == LANES, "prompt attention keeps one compressed block per lane"
    n_cmp_p = n_seg - CMP_RATIO + 1
    seg = lambda a: a.reshape(mp // CMP_STRIDE, CMP_STRIDE * KV_W)
    tm_seg = min(tm, mp // CMP_STRIDE)
    kcc = _cmpfin(_mm(seg(kc), cmp_k[0], tm=tm_seg).reshape(bp, n_seg, -1), None, *cmp_k[1:], layout='rows')
    vcc = _cmpfin(_mm(seg(vc), cmp_v[0], tm=tm_seg).reshape(bp, n_seg, -1), None, *cmp_v[1:3],
                  jnp.transpose(cmp_v[3]), layout='lanes')

    n_sel_p = tp // SEL
    f = _bias_by_dist(rel_bias, 2 * TQ)
    v_same = jnp.concatenate([jnp.repeat(f[:, :1], TQ - 1, axis=1), f[:, :TQ + 1]], axis=1)
    v_prev = jnp.concatenate([f[:, 1:], f[:, -1:]], axis=1)
    tiles = _toeplitz(jnp.stack([v_same, v_prev]), TQ).reshape(2, NSA_G, NSA_HG, TQ, TQ)
    tiles = jnp.transpose(tiles, (0, 1, 3, 2, 4)).reshape(2, NSA_G, TQ, NSA_HG * TQ)
    far_cols = jnp.broadcast_to(f[:, -1].reshape(NSA_G, NSA_HG, 1), (NSA_G, NSA_HG, TQ)).reshape(NSA_G, 1, -1)
    bias_t = (tiles - far_cols[None]) * LOG2E
    d0 = [-CMP_STRIDE * (r - CMP_WIN_PAD) - (CMP_BLOCK - 1) for r in range(CMP_WIN_ROWS)]
    lo_pad, hi_pad = max(0, -min(d0)), max(0, max(d0) + TQ - 2 * TQ)
    f_pad = jnp.concatenate([jnp.repeat(f[:, :1], lo_pad, axis=1), f, jnp.repeat(f[:, -1:], hi_pad, axis=1)], axis=1)
    bias_wc = jnp.stack([f_pad[:, d + lo_pad:d + lo_pad + TQ] for d in d0], axis=1)
    bias_wc = bias_wc.reshape(NSA_G, NSA_HG, CMP_WIN_ROWS, TQ)
    bias_wc = jnp.transpose(bias_wc, (0, 2, 1, 3)).reshape(NSA_G, CMP_WIN_ROWS, NSA_HG * TQ)
    bias_wc = (bias_wc - far_cols) * LOG2E
    c2s_t = np.zeros((n_sel_p, n_seg), np.float32)
    c2s_t[:, :n_cmp_p] = _cmp_to_sel_np(n_cmp_p, n_sel_p).T
    k4 = lambda a: a.reshape(NSA_G, bp, tp, DH)
    o_nsa_p = _nsa_prompt(qt, kcc, vcc, k4(ksh), vst, k4(kwh), vwt, gtt, bias_t, bias_wc,
                          jnp.asarray(c2s_t), bsz=bp, n_cmp=n_cmp_p)

    y_p = _post(xp, o_gla_p.reshape(mp, V_W), o_nsa_p.reshape(mp, D_MODEL), sma, smb, gt1_p, sc2_p, sh2_p, gt2_p,
                n2, fg, wo, wu, wd, tm=tm)

    kv5 = lambda a, b, t: a.reshape(1, b, t, NSA_G, DH)
    wp = min(WIN, tp)
    rows5 = lambda a: jnp.transpose(a, (0, 3, 1, 2))[None]
    p_out = (s_gla_p[None], rows5(kc_t), rows5(vc_t), rows5(ks_t), rows5(vs_t),
             kv5(kw, bp, tp)[:, :, tp - wp:], kv5(vw, bp, tp)[:, :, tp - wp:])

    xs = x_sample.reshape(ms, D_MODEL)
    (gq, gk, gv, sgr, la, kc, vc, ks, vs, kw, vw, gates, sma, smb, nq) = _inproj(
        xs, sc1_s, sh1_s, g1, w_perm, a2p, ab, head_major=False, tm=tm_s)

    t_pad = 8
    pad_t = lambda a: jnp.pad(a.reshape(bs, ts, a.shape[-1]), ((0, 0), (0, t_pad - ts), (0, 0)))
    o_gla_s, s_gla_s = _gla(pad_t(gq), pad_t(gk), pad_t(gv), pad_t(la), pad_t(sgr), gn, state_gla[0], tb=t_pad,
                            n_seq=2 if bs % 2 == 0 else 1)
    o_gla_s = o_gla_s[:, :ts].reshape(ms, V_W)

    pool_t = lambda a: jnp.transpose(a[0], (0, 2, 3, 1))
    new_seg = lambda a: jnp.pad(a.reshape(bs, ts, KV_W), ((0, 0), (0, CMP_STRIDE - ts), (0, 0))).reshape(
        bs, CMP_STRIDE * KV_W)
    n_cmp_s = past_len // CMP_STRIDE
    pps_cmp = min(32, n_pages)
    kcc = _cmpfin(_cmp_paged(pool_t(cache_k_cmp), page_table, _tap_w1(cmp_w1_k[0]), pages_per_step=pps_cmp),
                  _mm(new_seg(kc), cmp_k[0], tm=bs).reshape(bs, 1, -1), *cmp_k[1:3], jnp.transpose(cmp_k[3]),
                  layout='lanes').reshape(bs, KV_W, n_cmp_s)
    vcc = _cmpfin(_cmp_paged(pool_t(cache_v_cmp), page_table, _tap_w1(cmp_w1_v[0]), pages_per_step=pps_cmp),
                  _mm(new_seg(vc), cmp_v[0], tm=bs).reshape(bs, 1, -1), *cmp_v[1:], layout='flat')

    q5 = nq.reshape(bs, ts, NSA_G, NSA_HG, DH)
    q_bd = jnp.einsum('btghd,pg->bghtpd', q5, jnp.eye(NSA_G, dtype=F32)).reshape(bs, NSA_HEADS * ts, KV_W)
    q_bd = q_bd.astype(BF16)

    ncol = NSA_HEADS * ts
    col_tok = np.arange(ncol) % ts

    def bias_rows(key_pos):
        dist = past_len + col_tok[:, None] - key_pos[None, :]
        near = np.nonzero(dist.min(axis=0) < REL_MAX)[0]
        lo = int(near.min()) if near.size else len(key_pos)
        far_part = jnp.broadcast_to(jnp.repeat(f[:, -1], ts)[:, None], (ncol, lo))
        near = jnp.stack([_bias_at(rel_bias, dist[t, lo:]) for t in range(ts)], axis=1)
        return jnp.concatenate([far_part, near.reshape(ncol, -1)], axis=1)

    pps = min(32, n_pages)
    bias_c_s = bias_rows(np.arange(n_cmp_s) * CMP_STRIDE + CMP_BLOCK - 1)
    bias_s_s = jnp.transpose(bias_rows(np.arange(past_len)).reshape(ncol, n_pages // pps, pps * PAGE), (1, 0, 2))
    bias_n_s = bias_rows(past_len + np.arange(PAGE))
    bias_w_s = bias_rows(past_len - WIN + np.arange(WIN))
    n_sel_s = past_len // SEL + 1
    c2s_s = np.zeros((n_cmp_s, -(-n_sel_s // LANES) * LANES), np.float32)
    c2s_s[:, :n_sel_s] = _cmp_to_sel_np(n_cmp_s, n_sel_s)
    gt_col = np.arange(LANES) // (NSA_HG * ts) * ts + np.arange(LANES) % ts
    hsum = ((gt_col[:, None] == np.arange(LANES)[None, :]) & (np.arange(LANES)[:, None] < ncol)).astype(np.float32)
    expand_s = (np.arange(pps * PAGE)[None, :] // SEL == np.arange(pps * PAGE // SEL)[:, None]).astype(np.float32)
    gates_s = jnp.transpose(gates[:, :3 * NSA_HEADS].reshape(bs, ts, 3, NSA_HEADS), (0, 3, 1, 2)).reshape(bs, ncol, 3)
    new_rows = lambda a: jnp.pad(a.reshape(bs, ts, KV_W), ((0, 0), (0, PAGE - ts), (0, 0)))
    wb = cache_k_win.shape[2]
    win_t = lambda a: pool_t(a).reshape(bs, KV_W, wb)
    o_s = _nsa_sample(page_table, pool_t(cache_k_slc), pool_t(cache_v_slc), q_bd, kcc, vcc, new_rows(ks),
                      new_rows(vs), win_t(cache_k_win), win_t(cache_v_win), new_rows(kw),
                      new_rows(vw), gates_s, bias_c_s, bias_s_s, bias_n_s, bias_w_s, jnp.asarray(c2s_s),
                      jnp.asarray(hsum.T), jnp.asarray(hsum.T, dtype=BF16), jnp.asarray(expand_s, dtype=BF16),
                      pps=pps, t_new=ts)
    o_nsa_s = jnp.transpose(o_s.reshape(bs, NSA_HEADS, ts, DH), (0, 2, 1, 3)).reshape(ms, D_MODEL)

    y_s = _post(xs, o_gla_s, o_nsa_s, sma, smb, gt1_s, sc2_s, sh2_s, gt2_s, n2, fg, wo, wu, wd, tm=tm_s)

    kw_new = jnp.concatenate([cache_k_win[0], kv5(kw, bs, ts)[0]], axis=1)[:, ts:][None]
    vw_new = jnp.concatenate([cache_v_win[0], kv5(vw, bs, ts)[0]], axis=1)[:, ts:][None]
    s_out = (s_gla_s[None], kv5(kc, bs, ts), kv5(vc, bs, ts), kv5(ks, bs, ts), kv5(vs, bs, ts), kw_new, vw_new)

    return (y_p.reshape(bp, tp, D_MODEL), y_s.reshape(bs, ts, D_MODEL)) + p_out + s_out
```

```python
import functools
import math

import numpy as np
import jax
import jax.numpy as jnp
from jax import lax
from jax.experimental import pallas as pl
from jax.experimental.pallas import tpu as pltpu

F32 = jnp.float32
BF16 = jnp.bfloat16
HIGHEST = lax.Precision.HIGHEST

D_MODEL = 1024
GLA_HEADS = 4
GLA_DK = 128
GLA_DV = 256
GLA_RANK = 16
GLA_TAU = 16.0
GLA_CHUNK = 64
GLA_SUB = 16
NSA_HEADS = 16
DH = 64
NSA_G = 4
NSA_HG = 4
CMP_BLOCK = 32
CMP_STRIDE = 16
CMP_RATIO = 2
CMP_HID = 128
SEL = 64
SEL_SHIFT = 6
TOPK = 16
WIN = 512
REL_BUCKETS = 32
REL_MAX = 128
D_FF = 4096
EPS = 1e-6
NEG = -1e30
LOG2E = math.log2(math.e)
FORCE = 1e4
PAGE = 128

QK_W = GLA_HEADS * GLA_DK
V_W = GLA_HEADS * GLA_DV
KV_W = NSA_G * DH
LANES = 128
TQ = 128
VMEM_LIMIT = 56 * 1024 * 1024

C_GQ, C_GK, C_GV, C_GR, C_NQ = 0, 512, 1024, 2048, 3072
C_KV = 4096
C_MA, C_MB, C_Z = 5632, 6656, 7680
W_COLS = 7808
Z_GA = 48


def _cparams(sem):
    return pltpu.CompilerParams(dimension_semantics=sem, vmem_limit_bytes=VMEM_LIMIT)


def _const_spec(shape):
    return pl.BlockSpec(shape, lambda *a: (0,) * len(shape), pipeline_mode=pl.Buffered(1))


def _sigmoid(x):
    return 1.0 / (1.0 + jnp.exp(-x))


def _rms(x, g):
    return x * lax.rsqrt(jnp.mean(x * x, axis=-1, keepdims=True) + EPS) * g


def _dot(a, b, **kw):
    return jnp.dot(a, b, preferred_element_type=F32, **kw)


def _dot_nt(a, b):
    return lax.dot_general(a, b, (((1,), (1,)), ((), ())), preferred_element_type=F32)


def _dot_tn(a, b):
    return lax.dot_general(a, b, (((0,), (0,)), ((), ())), preferred_element_type=F32)


def _bucket_np(dist):
    n = np.maximum(dist, 0)
    nf = np.maximum(n, 1).astype(np.float64)
    large = 16 + (np.log(nf / 16.0) / math.log(REL_MAX / 16.0) * 16.0).astype(np.int64)
    large = np.minimum(large, REL_BUCKETS - 1)
    return np.where(n < 16, n, large).astype(np.int32)


def _ada_kernel(c_ref, w_ref, b_ref, o_ref):
    c = c_ref[...]
    o_ref[...] = _dot(c * _sigmoid(c), w_ref[...], precision=HIGHEST) + b_ref[...]


def _ada(c, w, b):
    r, d = c.shape
    n = w.shape[1]
    return pl.pallas_call(
        _ada_kernel,
        out_shape=jax.ShapeDtypeStruct((r, n), F32),
        grid=(n // d,),
        in_specs=[pl.BlockSpec((r, d), lambda j: (0, 0)),
                  pl.BlockSpec((d, d), lambda j: (0, j)),
                  pl.BlockSpec((1, d), lambda j: (0, j))],
        out_specs=pl.BlockSpec((r, d), lambda j: (0, j)),
        compiler_params=_cparams(("arbitrary",)),
        name="ada",
    )(c, w, b.reshape(1, n))


def _inproj_kernel(x_ref, sc_ref, sh_ref, g_ref, w_ref, a2_ref, ab_ref, *outs, head_major):
    (gq_o, gk_o, gv_o, sgr_o, la_o, kc_o, vc_o, ks_o, vs_o, kw_o, vw_o, gt_o, sma_o, smb_o, *rest) = outs
    h = _rms(x_ref[...], g_ref[...]) * (1.0 + sc_ref[0]) + sh_ref[0]
    hb = h.astype(BF16)

    def proj(lo, hi):
        return _dot(hb, w_ref[:, lo:hi])

    gq_o[...] = proj(C_GQ, C_GK) * (GLA_DK ** -0.5)
    gk_o[...] = proj(C_GK, C_GV)
    gv_o[...] = proj(C_GV, C_GR)
    gr = proj(C_GR, C_NQ)
    sgr_o[...] = gr * _sigmoid(gr)
    nq = proj(C_NQ, C_KV) * (DH ** -0.5)
    kv = proj(C_KV, C_MA)
    for n, o in enumerate((kc_o, vc_o, ks_o, vs_o, kw_o, vw_o)):
        o[...] = kv[:, n * KV_W:(n + 1) * KV_W]
    sma_o[...] = _sigmoid(proj(C_MA, C_MB))
    smb_o[...] = _sigmoid(proj(C_MB, C_Z))
    z = proj(C_Z, W_COLS)
    gt_o[...] = _sigmoid(z)
    pre = _dot(z.astype(BF16), a2_ref[...]) + ab_ref[...]
    la_o[...] = (jnp.minimum(pre, 0.0) - jnp.log(1.0 + jnp.exp(-jnp.abs(pre)))) * (1.0 / GLA_TAU)
    if head_major:
        qt_o, ksh_o, vst_o, kwh_o, vwt_o, gtt_o, kct_o, vct_o, kst_o, vstf_o = rest

        def put_transposed(x, *outs):
            for c in range(x.shape[1] // LANES):
                t = jnp.transpose(x[:, c * LANES:(c + 1) * LANES])
                for o, lead in outs:
                    o[lead + (2 * c,)] = t[:DH].astype(o.dtype)
                    o[lead + (2 * c + 1,)] = t[DH:].astype(o.dtype)

        put_transposed(nq * LOG2E, (qt_o, ()))
        put_transposed(kv[:, 0 * KV_W:1 * KV_W], (kct_o, (0,)))
        put_transposed(kv[:, 1 * KV_W:2 * KV_W], (vct_o, (0,)))
        put_transposed(kv[:, 2 * KV_W:3 * KV_W], (kst_o, (0,)))
        put_transposed(kv[:, 3 * KV_W:4 * KV_W], (vst_o, ()), (vstf_o, (0,)))
        put_transposed(kv[:, 5 * KV_W:6 * KV_W], (vwt_o, ()))
        for n, o in ((2, ksh_o), (4, kwh_o)):
            for gg in range(NSA_G):
                lo = n * KV_W + gg * DH
                o[gg] = kv[:, lo:lo + DH].astype(BF16)
        gtt_o[...] = jnp.transpose(_sigmoid(z))
    else:
        (nq_o,) = rest
        nq_o[...] = nq


def _inproj(x, scale, shift, g1, w, a2p, ab, *, head_major, tm, seq=None):
    m = x.shape[0]
    nb, r, _ = scale.shape
    rows_per_mod = m // nb
    mod_spec = pl.BlockSpec((1, r, D_MODEL), lambda i: (i * tm // rows_per_mod, 0, 0))
    row = lambda wdt: pl.BlockSpec((tm, wdt), lambda i: (i, 0))
    widths = [QK_W, QK_W, V_W, V_W, QK_W] + [KV_W] * 6 + [LANES, D_MODEL, D_MODEL]
    out_shape = [jax.ShapeDtypeStruct((m, wd), F32) for wd in widths]
    out_specs = [row(wd) for wd in widths]
    if head_major:
        def rows_major(n):
            out_shape.append(jax.ShapeDtypeStruct((n, m, DH), BF16))
            out_specs.append(pl.BlockSpec((n, tm, DH), lambda i: (0, i, 0)))

        def lanes_major(n):
            out_shape.append(jax.ShapeDtypeStruct((n, DH, m), BF16))
            out_specs.append(pl.BlockSpec((n, DH, tm), lambda i: (0, 0, i)))

        lanes_major(NSA_HEADS)
        rows_major(NSA_G)
        lanes_major(NSA_G)
        rows_major(NSA_G)
        lanes_major(NSA_G)
        out_shape.append(jax.ShapeDtypeStruct((LANES, m), F32))
        out_specs.append(pl.BlockSpec((LANES, tm), lambda i: (0, i)))
        tiles_per_seq = seq // tm
        for _ in range(4):
            out_shape.append(jax.ShapeDtypeStruct((m // seq, NSA_G, DH, seq), F32))
            out_specs.append(pl.BlockSpec((1, NSA_G, DH, tm),
                                          lambda i: (i // tiles_per_seq, 0, 0, i % tiles_per_seq)))
    else:
        out_shape.append(jax.ShapeDtypeStruct((m, D_MODEL), F32))
        out_specs.append(row(D_MODEL))
    return pl.pallas_call(
        functools.partial(_inproj_kernel, head_major=head_major),
        out_shape=out_shape,
        grid=(m // tm,),
        in_specs=[row(D_MODEL), mod_spec, mod_spec, _const_spec((1, D_MODEL)), _const_spec((D_MODEL, W_COLS)),
                  _const_spec((LANES, QK_W)), _const_spec((1, QK_W))],
        out_specs=out_specs,
        compiler_params=_cparams(("parallel",)),
        name="inproj",
    )(x, scale, shift, g1, w, a2p, ab)


def _gla_kernel(*refs, chunk, sub, n_chunk, n_seq, has_init):
    if has_init:
        q_ref, k_ref, v_ref, la_ref, sgr_ref, gn_ref, s0_ref, o_ref, so_ref, s_scr = refs
    else:
        q_ref, k_ref, v_ref, la_ref, sgr_ref, gn_ref, o_ref, so_ref, s_scr = refs
    t = pl.program_id(1)

    @pl.when(t == 0)
    def _():
        s_scr[...] = s0_ref[...] if has_init else jnp.zeros_like(s_scr)

    tril = (lax.broadcasted_iota(jnp.int32, (chunk, chunk), 0)
            >= lax.broadcasted_iota(jnp.int32, (chunk, chunk), 1)).astype(F32)
    sub_row = lax.broadcasted_iota(jnp.int32, (sub, 1), 0)

    def one_chunk(c, carry):
        r0 = pl.multiple_of(c * chunk, chunk)
        rows = pl.ds(r0, chunk)
        for n in range(n_seq):
            b_all = _dot(tril, la_ref[n, rows, :], precision=HIGHEST)
            for h in range(GLA_HEADS):
                one_head(n, h, rows, b_all[:, h * GLA_DK:(h + 1) * GLA_DK])
        return carry

    def one_head(n, h, rows, b):
        ks, vs = slice(h * GLA_DK, (h + 1) * GLA_DK), slice(h * GLA_DV, (h + 1) * GLA_DV)
        q = q_ref[n, rows, ks]
        k = k_ref[n, rows, ks]
        v = v_ref[n, rows, vs]
        s_old = s_scr[n, h]
        vb = v.astype(BF16)
        o_inter = _dot((q * jnp.exp(b)).astype(BF16), s_old.astype(BF16))
        blocks = []
        for blk in range(chunk // sub):
            sl = slice(blk * sub, (blk + 1) * sub)
            q_i, k_i, v_i, b_i = q[sl], k[sl], v[sl], b[sl]
            acc = jnp.zeros((sub, GLA_DV), F32)
            for j in range(sub):
                w = q_i * k_i[j:j + 1] * jnp.exp(jnp.minimum(b_i - b_i[j:j + 1], 0.0))
                a = jnp.sum(w, axis=-1, keepdims=True)
                acc = acc + jnp.where(sub_row >= j, a, 0.0) * v_i[j:j + 1]
            if blk > 0:
                prev = blk * sub
                ref_row = b_i[0:1]
                qp = (q_i * jnp.exp(b_i - ref_row)).astype(BF16)
                kp = (k[:prev] * jnp.exp(ref_row - b[:prev])).astype(BF16)
                acc = acc + _dot(_dot_nt(qp, kp).astype(BF16), vb[:prev])
            blocks.append(acc)
        o = o_inter + (jnp.concatenate(blocks, axis=0) if len(blocks) > 1 else blocks[0])
        b_last = b[chunk - 1:chunk]
        kp = (k * jnp.exp(b_last - b)).astype(BF16)
        decay_col = jnp.transpose(jnp.broadcast_to(jnp.exp(b_last), (GLA_DK, GLA_DK)))[:, 0:1]
        s_scr[n, h] = decay_col * s_old + _dot_tn(kp, vb)
        o_ref[n, rows, vs] = _rms(o, gn_ref[...]) * sgr_ref[n, rows, vs]

    lax.fori_loop(0, n_chunk, one_chunk, 0)

    @pl.when(t == pl.num_programs(1) - 1)
    def _():
        so_ref[...] = s_scr[...]


def _gla(q, k, v, la, sgr, gn, s0, *, tb, n_seq):
    bsz, t, _ = q.shape
    chunk = min(GLA_CHUNK, tb)
    sub = min(GLA_SUB, chunk)
    blk = lambda wd: pl.BlockSpec((n_seq, tb, wd), lambda b, i: (b, i, 0))
    st_spec = pl.BlockSpec((n_seq, GLA_HEADS, GLA_DK, GLA_DV), lambda b, i: (b, 0, 0, 0))
    in_specs = [blk(QK_W), blk(QK_W), blk(V_W), blk(QK_W), blk(V_W),
                pl.BlockSpec((1, GLA_DV), lambda b, i: (0, 0))]
    args = [q, k, v, la, sgr, gn]
    if s0 is not None:
        in_specs.append(st_spec)
        args.append(s0)
    return pl.pallas_call(
        functools.partial(_gla_kernel, chunk=chunk, sub=sub, n_chunk=tb // chunk, n_seq=n_seq,
                          has_init=s0 is not None),
        out_shape=[jax.ShapeDtypeStruct((bsz, t, V_W), F32),
                   jax.ShapeDtypeStruct((bsz, GLA_HEADS, GLA_DK, GLA_DV), F32)],
        grid=(bsz // n_seq, t // tb),
        in_specs=in_specs,
        out_specs=[blk(V_W), st_spec],
        scratch_shapes=[pltpu.VMEM((n_seq, GLA_HEADS, GLA_DK, GLA_DV), F32)],
        compiler_params=_cparams(("parallel", "arbitrary")),
        name="gla",
    )(*args)


def _mm_kernel(x_ref, w_ref, o_ref):
    o_ref[...] = _dot(x_ref[...].astype(BF16), w_ref[...])


def _mm(x, w, *, tm):
    m, kdim = x.shape
    n = w.shape[1]
    return pl.pallas_call(
        _mm_kernel,
        out_shape=jax.ShapeDtypeStruct((m, n), F32),
        grid=(m // tm,),
        in_specs=[pl.BlockSpec((tm, kdim), lambda i: (i, 0)), _const_spec((kdim, n))],
        out_specs=pl.BlockSpec((tm, n), lambda i: (i, 0)),
        compiler_params=_cparams(("parallel",)),
        name="cmp_mm",
    )(x, w)


CMP_TAPS = 4
SUBLANES = 8


def _odd_pitch(n):
    tiles = -(-n // SUBLANES)
    return (tiles + 1 - tiles % 2) * SUBLANES


def _cmp_paged_kernel(*refs, pps):
    pages = refs[1:1 + pps]
    w_ref, o_ref, tok_s = refs[1 + pps:]
    heads_per_slab = LANES // DH
    n_slab = NSA_G // heads_per_slab
    n_seg = pps * PAGE // CMP_STRIDE
    pitch = tok_s.shape[1] // CMP_STRIDE
    segs_per_page = PAGE // CMP_STRIDE
    for p in range(pps):
        t = jnp.transpose(pages[p][0].reshape(KV_W, PAGE))
        for s in range(segs_per_page):
            for sl in range(n_slab):
                tok_s[sl, pl.ds(p * segs_per_page + s, CMP_STRIDE, stride=pitch), :] = (
                    t[s * CMP_STRIDE:(s + 1) * CMP_STRIDE, sl * LANES:(sl + 1) * LANES])
    width = CMP_RATIO * CMP_HID
    first_half = lax.broadcasted_iota(jnp.int32, (1, LANES), 1) < DH
    for sl in range(n_slab):
        x = [tok_s[sl, e * pitch:e * pitch + n_seg, :] for e in range(CMP_STRIDE)]
        r = [pltpu.roll(v, DH, 1) for v in x]
        for odd in range(heads_per_slab):
            g = sl * heads_per_slab + odd
            acc = None
            for j in range(CMP_STRIDE // CMP_TAPS):
                pieces = []
                for e in range(j * CMP_TAPS, (j + 1) * CMP_TAPS, 2):
                    lo, hi = (r[e], x[e + 1]) if odd else (x[e], r[e + 1])
                    pieces.append(jnp.where(first_half, lo, hi))
                term = _dot(jnp.concatenate(pieces, axis=1).astype(BF16), w_ref[j])
                acc = term if acc is None else acc + term
            o_ref[0, :, g * width:(g + 1) * width] = acc


def _cmp_paged(pool_t, page_table, w_taps, *, pages_per_step):
    bsz, n_pages = page_table.shape
    pps = pages_per_step
    n_seg = pps * PAGE // CMP_STRIDE
    width = NSA_G * CMP_RATIO * CMP_HID

    def page_spec(j):
        return pl.BlockSpec((1, NSA_G, DH, PAGE), lambda b, s, pt: (pt[b, s * pps + j], 0, 0, 0))

    grid_spec = pltpu.PrefetchScalarGridSpec(
        num_scalar_prefetch=1,
        grid=(bsz, n_pages // pps),
        in_specs=[page_spec(j) for j in range(pps)]
        + [pl.BlockSpec(w_taps.shape, lambda b, s, pt: (0, 0, 0), pipeline_mode=pl.Buffered(1))],
        out_specs=pl.BlockSpec((1, n_seg, width), lambda b, s, pt: (b, s, 0)),
        scratch_shapes=[pltpu.VMEM((NSA_G * DH // LANES, CMP_STRIDE * _odd_pitch(n_seg), LANES), F32)],
    )
    return pl.pallas_call(
        functools.partial(_cmp_paged_kernel, pps=pps),
        out_shape=jax.ShapeDtypeStruct((bsz, n_pages * PAGE // CMP_STRIDE, width), F32),
        grid_spec=grid_spec,
        compiler_params=_cparams(("parallel", "arbitrary")),
        name="cmp_paged",
    )(page_table, *([pool_t] * pps), w_taps)


def _cmpfin_kernel(*refs, n, has_new, layout):
    if has_new:
        p_ref, pn_ref, pe_ref, w1_ref, w2_ref, o_ref = refs
    else:
        p_ref, pe_ref, w1_ref, w2_ref, o_ref = refs
    pe_term = _dot(pe_ref[...], w1_ref[...], precision=HIGHEST)[0:1]
    w2 = w2_ref[...].astype(BF16)
    rowi = lax.broadcasted_iota(jnp.int32, (n, 1), 0)
    for g in range(NSA_G):
        base = g * CMP_RATIO * CMP_HID
        first = p_ref[0, :, base:base + CMP_HID]
        second = pltpu.roll(p_ref[0, :, base + CMP_HID:base + 2 * CMP_HID], n - 1, 0)
        if has_new:
            second = jnp.where(rowi == n - 1, pn_ref[0, :, base + CMP_HID:base + 2 * CMP_HID], second)
        acc = first + second + pe_term
        hid = (acc * _sigmoid(acc)).astype(BF16)
        if layout == 'flat':
            o_ref[0, :, g * DH:(g + 1) * DH] = _dot(hid, w2)
        elif layout == 'rows':
            o_ref[0, g] = _dot(hid, w2)
        else:
            o_ref[0, g] = _dot_nt(w2, hid)


def _cmpfin(parts, parts_new, pe, w1, w2, *, layout):
    bsz, n, width = parts.shape
    in_specs = [pl.BlockSpec((1, n, width), lambda b: (b, 0, 0))]
    args = [parts]
    if parts_new is not None:
        in_specs.append(pl.BlockSpec((1, 1, width), lambda b: (b, 0, 0)))
        args.append(parts_new)
    in_specs += [_const_spec(pe.shape), _const_spec(w1.shape), _const_spec(w2.shape)]
    if layout == 'flat':
        out_shape = jax.ShapeDtypeStruct((bsz, n, KV_W), F32)
        out_spec = pl.BlockSpec((1, n, KV_W), lambda b: (b, 0, 0))
    else:
        inner = (n, DH) if layout == 'rows' else (DH, n)
        out_shape = jax.ShapeDtypeStruct((bsz, NSA_G) + inner, F32)
        out_spec = pl.BlockSpec((1, NSA_G) + inner, lambda b: (b, 0, 0, 0))
    return pl.pallas_call(
        functools.partial(_cmpfin_kernel, n=n, has_new=parts_new is not None, layout=layout),
        out_shape=out_shape,
        grid=(bsz,),
        in_specs=in_specs,
        out_specs=out_spec,
        compiler_params=_cparams(("parallel",)),
        name="cmp_finish",
    )(*args, pe, w1, w2)


def _masked_softmax(s, mask, axis, exp=jnp.exp):
    s = jnp.where(mask, s, NEG)
    m = jnp.max(s, axis=axis, keepdims=True)
    e = jnp.where(mask, exp(s - m), 0.0)
    return e / jnp.maximum(jnp.sum(e, axis=axis, keepdims=True), 1e-30)


CMP_WIN_ROWS = 24
CMP_WIN_PAD = 16
FAR_TILES = 4


def _nsa_prompt_kernel(qt_ref, kc_ref, vct_ref, ks_ref, vst_ref, kw_ref, vwt_ref, gtt_ref, bt_ref, wc_ref,
                       c2s_ref, o_ref, sel_s, bc_s, m_s, l_s, acc_s, *, n_cmp, n_sel, n_tiles):
    i = pl.program_id(1)
    tok = i * TQ + lax.broadcasted_iota(jnp.int32, (1, TQ), 1)
    tok4 = jnp.concatenate([tok] * NSA_HG, axis=1)
    row = lax.broadcasted_iota(jnp.int32, (TQ, 1), 0)
    groups = range(NSA_G)
    q4 = [jnp.concatenate([qt_ref[g * NSA_HG + h] for h in range(NSA_HG)], axis=1) for g in groups]

    def tile4(x):
        return jnp.concatenate([x] * NSA_HG, axis=1)

    o_cmp = []
    for g in groups:
        bc_s[g] = jnp.zeros(bc_s.shape[1:], F32)
        bc_s[g, pl.ds(pl.multiple_of(i * (TQ // CMP_STRIDE), TQ // CMP_STRIDE), CMP_WIN_ROWS), :] = wc_ref[g]
        s_c = _dot(kc_ref[0, g].astype(BF16), q4[g]) + bc_s[g, CMP_WIN_PAD:CMP_WIN_PAD + TQ, :]
        mask_c = (tok4 - (row * CMP_STRIDE + CMP_BLOCK - 1) >= 0) & (row < n_cmp)
        p_c = _masked_softmax(s_c, mask_c, 0, exp=jnp.exp2)
        o_cmp.append(_dot(vct_ref[0, g].astype(BF16), p_c.astype(BF16)))
        p_sum = p_c[:, 0:TQ] + p_c[:, TQ:2 * TQ] + p_c[:, 2 * TQ:3 * TQ] + p_c[:, 3 * TQ:4 * TQ]
        imp = _dot(c2s_ref[...], p_sum, precision=HIGHEST)
        blk = lax.broadcasted_iota(jnp.int32, (n_sel, 1), 0)
        cur = jnp.right_shift(tok, SEL_SHIFT)
        forced = (blk == 0) | (blk == cur) | (blk == cur - 1)
        valid = blk * SEL <= tok
        score = jnp.where(forced, FORCE, jnp.where(valid, imp, -FORCE))
        cnt = jnp.zeros((n_sel, TQ), jnp.int32)
        for b in range(n_sel):
            s_b = score[b:b + 1]
            ahead = (s_b > score) | ((s_b == score) & (blk > b))
            cnt = cnt + ahead.astype(jnp.int32)
        sel_s[g] = jnp.where(cnt < min(TOPK, n_sel), 1.0, 0.0)

    def tile_scores(k_ref, g, kt):
        k0 = pl.multiple_of(jnp.clip(kt, 0, n_tiles - 1) * TQ, TQ)
        return _dot(k_ref[g, 0, pl.ds(k0, TQ), :], q4[g]), k0

    def sel_mask(g, kt, valid):
        blocks_per_tile = TQ // SEL
        first = jnp.clip(kt, 0, n_tiles - 1) * blocks_per_tile
        out = None
        for j in reversed(range(blocks_per_tile)):
            add_j = jnp.where((sel_s[g, pl.ds(first + j, 1), :] > 0.5) & valid, 0.0, NEG)
            out = add_j if out is None else jnp.where(row < (j + 1) * SEL, add_j, out)
        return out

    def update(state, tiles):
        m_run, l_run, acc = state
        m_new = m_run
        for s, _ in tiles:
            m_new = jnp.maximum(m_new, jnp.max(s, axis=0, keepdims=True))
        alpha = jnp.exp2(m_run - m_new)
        l_new = alpha * l_run
        acc = alpha * acc
        for s, vt in tiles:
            p = jnp.exp2(s - m_new)
            l_new = l_new + jnp.sum(p, axis=0, keepdims=True)
            acc = acc + _dot(vt, p.astype(BF16))
        return m_new, l_new, acc

    def near_tiles(k_ref, vt_ref, g, mask_fn):
        tiles = []
        for typ, kt in ((1, i - 1), (0, i)):
            s, k0 = tile_scores(k_ref, g, kt)
            madd = mask_fn(kt, tok - (kt * TQ + row))
            tiles.append((s + bt_ref[typ, g] + tile4(madd), vt_ref[g, :, pl.ds(k0, TQ)]))
        return tiles

    cols = NSA_HG * TQ
    empty = (jnp.full((1, cols), NEG, F32), jnp.zeros((1, cols), F32), jnp.zeros((DH, cols), F32))

    def normalised(state):
        return state[2] / jnp.maximum(state[1], 1e-30)

    for g in groups:
        m_s[g], l_s[g], acc_s[g] = empty
    n_far = jnp.maximum(i - 1, 0)

    def far_body(c, carry):
        for g in groups:
            tiles = []
            for u in range(FAR_TILES):
                kt = FAR_TILES * c + u
                s, k0 = tile_scores(ks_ref, g, kt)
                tiles.append((s + tile4(sel_mask(g, kt, kt < n_far)), vst_ref[g, :, pl.ds(k0, TQ)]))
            m_s[g], l_s[g], acc_s[g] = update((m_s[g], l_s[g], acc_s[g]), tiles)
        return carry

    lax.fori_loop(0, (n_far + FAR_TILES - 1) // FAR_TILES, far_body, 0)

    heads = []
    for g in groups:
        slc_near = near_tiles(ks_ref, vst_ref, g,
                              lambda kt, dist: jnp.where(dist >= 0, sel_mask(g, kt, kt >= 0), NEG))
        o_slc = normalised(update((m_s[g], l_s[g], acc_s[g]), slc_near))

        tiles = []
        for back in range(WIN // TQ, 1, -1):
            kt = i - back
            s, k0 = tile_scores(kw_ref, g, kt)
            if back == WIN // TQ:
                dist = tok - (kt * TQ + row)
                s = s + tile4(jnp.where((dist < WIN) & (kt >= 0), 0.0, NEG))
            else:
                s = s + jnp.where(kt >= 0, 0.0, NEG)
            tiles.append((s, vwt_ref[g, :, pl.ds(k0, TQ)]))
        win_near = near_tiles(kw_ref, vwt_ref, g, lambda kt, dist: jnp.where((dist >= 0) & (kt >= 0), 0.0, NEG))
        o_win = normalised(update(update(empty, tiles), win_near))

        for h in range(NSA_HG):
            cs = slice(h * TQ, (h + 1) * TQ)
            o_h = jnp.zeros((DH, TQ), F32)
            for br, o_br in enumerate((o_cmp[g], o_slc, o_win)):
                gate_row = br * NSA_HEADS + g * NSA_HG + h
                o_h = o_h + gtt_ref[gate_row:gate_row + 1, :] * o_br[:, cs]
            heads.append(o_h)
    o_ref[0] = jnp.transpose(jnp.concatenate(heads, axis=0))


def _nsa_prompt(qt, kc, vct, ksh, vst, kwh, vwt, gtt, bias_t, bias_wc, c2s_t, *, bsz, n_cmp):
    t = qt.shape[2] // bsz
    nq = t // TQ
    n_sel = t // SEL
    cols = NSA_HG * TQ
    k_spec = pl.BlockSpec((NSA_G, 1, t, DH), lambda b, i: (0, b, 0, 0))
    vt_spec = pl.BlockSpec((NSA_G, DH, t), lambda b, i: (0, 0, b))
    const = lambda arr: pl.BlockSpec(arr.shape, lambda b, i: (0,) * arr.ndim)
    return pl.pallas_call(
        functools.partial(_nsa_prompt_kernel, n_cmp=n_cmp, n_sel=n_sel, n_tiles=nq),
        out_shape=jax.ShapeDtypeStruct((bsz, t, D_MODEL), F32),
        grid=(bsz, nq),
        in_specs=[pl.BlockSpec((NSA_HEADS, DH, TQ), lambda b, i: (0, 0, b * nq + i)),
                  pl.BlockSpec((1, NSA_G, kc.shape[2], DH), lambda b, i: (b, 0, 0, 0)),
                  pl.BlockSpec((1, NSA_G, DH, vct.shape[3]), lambda b, i: (b, 0, 0, 0)),
                  k_spec, vt_spec, k_spec, vt_spec,
                  pl.BlockSpec((LANES, TQ), lambda b, i: (0, b * nq + i)),
                  const(bias_t), const(bias_wc), const(c2s_t)],
        out_specs=pl.BlockSpec((1, TQ, D_MODEL), lambda b, i: (b, i, 0)),
        scratch_shapes=[pltpu.VMEM((NSA_G, n_sel, TQ), F32), pltpu.VMEM((NSA_G, CMP_WIN_PAD + TQ, cols), F32),
                        pltpu.VMEM((NSA_G, 1, cols), F32), pltpu.VMEM((NSA_G, 1, cols), F32),
                        pltpu.VMEM((NSA_G, DH, cols), F32)],
        compiler_params=_cparams(("parallel", "arbitrary")),
        name="nsa_prompt",
    )(qt, kc, vct, ksh, vst, kwh, vwt, gtt, bias_t, bias_wc, c2s_t)


def _nsa_sample_kernel(*refs, pps, n_cmp, n_sel, n_sel_pad, past_len, t_new):
    k_pages = refs[1:1 + pps]
    v_pages = refs[1 + pps:1 + 2 * pps]
    (q_ref, kct_ref, vc_ref, kn_ref, vn_ref, kwt_ref, vwt_ref, kwn_ref, vwn_ref, gt_ref, bc_ref, bs_ref, bn_ref,
     bw_ref, c2s_ref, hsum_ref, hexp_ref, e_ref, o_ref,
     sel_s, m_s, l_s, acc_s, ocmp_s, owin_s) = refs[1 + 2 * pps:]
    ch = pl.program_id(1)
    nrow = NSA_HEADS * t_new
    q = q_ref[0]
    rtok = lax.broadcasted_iota(jnp.int32, (nrow, 1), 0) & (t_new - 1)
    qpos = past_len + rtok
    lane = lax.broadcasted_iota(jnp.int32, (1, LANES), 1)

    def to_col(row_vec):
        return jnp.transpose(jnp.broadcast_to(row_vec, (LANES, LANES)))[:nrow, 0:1]

    @pl.when(ch == 0)
    def _():
        c_end = lax.broadcasted_iota(jnp.int32, (1, n_cmp), 1) * CMP_STRIDE + CMP_BLOCK - 1
        s_c = _dot(q, kct_ref[0].astype(BF16)) + bc_ref[...]
        p_c = _masked_softmax(s_c, qpos - c_end >= 0, -1)
        ocmp_s[...] = _dot(p_c.astype(BF16), vc_ref[0].astype(BF16))

        n_gt = NSA_G * t_new
        p_sum = _dot(hsum_ref[...], jnp.concatenate([p_c, jnp.zeros((LANES - nrow, n_cmp), F32)], axis=0),
                     precision=HIGHEST)[:n_gt]
        imp = _dot(p_sum, c2s_ref[...], precision=HIGHEST)
        n_blk_lanes = imp.shape[1]
        blk = lax.broadcasted_iota(jnp.int32, (1, n_blk_lanes), 1)
        gpos = past_len + (lax.broadcasted_iota(jnp.int32, (n_gt, 1), 0) & (t_new - 1))
        cur = jnp.right_shift(gpos, SEL_SHIFT)
        forced = (blk == 0) | (blk == cur) | (blk == cur - 1)
        valid = blk * SEL <= gpos
        score = jnp.where(forced, FORCE, jnp.where(valid, imp, -FORCE))
        score = jnp.where(blk < n_sel, score, -3.0 * FORCE)
        cnt = jnp.zeros((n_gt, n_blk_lanes), jnp.int32)
        for b in range(n_sel):
            s_b = score[:, b:b + 1]
            ahead = (s_b > score) | ((s_b == score) & (blk > b))
            cnt = cnt + ahead.astype(jnp.int32)
        sel = jnp.where((cnt < min(TOPK, n_sel)) & (blk < n_sel), 1.0, 0.0)
        sel_t = jnp.transpose(jnp.concatenate([sel, jnp.zeros((LANES - n_gt, n_blk_lanes), F32)], axis=0))
        sel_s[...] = _dot(sel_t[:n_sel_pad].astype(BF16), hexp_ref[...])

        key_w = past_len - WIN + lax.broadcasted_iota(jnp.int32, (1, WIN), 1)
        s_w = _dot(q, kwt_ref[0].astype(BF16)) + bw_ref[...]
        mask_w = lax.bitcast_convert_type(qpos - key_w, jnp.uint32) < WIN
        s_n = _dot_nt(q, kwn_ref[0].astype(BF16)) + bn_ref[...]
        mask_n = (lane < t_new) & (rtok - lane >= 0)
        s_w = jnp.where(mask_w, s_w, NEG)
        s_n = jnp.where(mask_n, s_n, NEG)
        m = jnp.maximum(jnp.max(s_w, axis=-1, keepdims=True), jnp.max(s_n, axis=-1, keepdims=True))
        e_w = jnp.where(mask_w, jnp.exp(s_w - m), 0.0)
        e_n = jnp.where(mask_n, jnp.exp(s_n - m), 0.0)
        den = jnp.maximum(jnp.sum(e_w, axis=-1, keepdims=True) + jnp.sum(e_n, axis=-1, keepdims=True), 1e-30)
        owin_s[...] = (_dot_nt((e_w / den).astype(BF16), vwt_ref[0].astype(BF16))
                       + _dot((e_n / den).astype(BF16), vwn_ref[0].astype(BF16)))

        s_t = _dot_nt(q, kn_ref[0].astype(BF16)) + bn_ref[...]
        mask_t = mask_n & (to_col(sel_s[pl.ds(past_len // SEL, 1), :]) > 0.5)
        s_t = jnp.where(mask_t, s_t, NEG)
        m0 = jnp.max(s_t, axis=-1, keepdims=True)
        p0 = jnp.where(mask_t, jnp.exp(s_t - m0), 0.0)
        m_s[...] = m0
        l_s[...] = jnp.sum(p0, axis=-1, keepdims=True)
        acc_s[...] = _dot(p0.astype(BF16), vn_ref[0].astype(BF16))

    kt_all = jnp.concatenate([r[0].reshape(KV_W, PAGE) for r in k_pages], axis=1).astype(BF16)
    vt_all = jnp.concatenate([r[0].reshape(KV_W, PAGE) for r in v_pages], axis=1).astype(BF16)
    s = _dot(q, kt_all) + bs_ref[0]
    blocks = pps * PAGE // SEL
    sel_blk = sel_s[pl.ds(pl.multiple_of(ch * blocks, blocks), blocks), :][:, :nrow].astype(BF16)
    mask = _dot_tn(sel_blk, e_ref[...]) > 0.5
    s = jnp.where(mask, s, NEG)
    m_old = m_s[...]
    m_new = jnp.maximum(m_old, jnp.max(s, axis=-1, keepdims=True))
    p = jnp.where(mask, jnp.exp(s - m_new), 0.0)
    alpha = jnp.exp(m_old - m_new)
    l_s[...] = alpha * l_s[...] + jnp.sum(p, axis=-1, keepdims=True)
    acc_s[...] = alpha * acc_s[...] + _dot_nt(p.astype(BF16), vt_all)
    m_s[...] = m_new

    @pl.when(ch == pl.num_programs(1) - 1)
    def _():
        o_slc = acc_s[...] / jnp.maximum(l_s[...], 1e-30)
        gates = gt_ref[0]
        hg_rows = NSA_HG * t_new
        for g in range(NSA_G):
            rs = slice(g * hg_rows, (g + 1) * hg_rows)
            cs = slice(g * DH, (g + 1) * DH)
            o_ref[0, rs, :] = (gates[rs, 0:1] * ocmp_s[rs, cs] + gates[rs, 1:2] * o_slc[rs, cs]
                               + gates[rs, 2:3] * owin_s[rs, cs])


def _nsa_sample(page_table, k_pool, v_pool, q_bd, kct, vc, k_new, v_new, kw_cache, vw_cache, kw_new, vw_new, gates,
                bias_c, bias_s, bias_n, bias_w, c2s, hsum_t, hexp, expand, *, pps, t_new):
    bsz, n_pages = page_table.shape
    past_len = n_pages * PAGE
    n_cmp = vc.shape[1]
    n_sel = past_len // SEL + 1
    n_sel_pad = -(-n_sel // 8) * 8
    nrow = NSA_HEADS * t_new

    def page_spec(j):
        return pl.BlockSpec((1, NSA_G, DH, PAGE), lambda b, s, pt: (pt[b, s * pps + j], 0, 0, 0))

    per_b = lambda shape: pl.BlockSpec((1,) + shape, lambda b, s, pt: (b,) + (0,) * len(shape))
    const = lambda arr: pl.BlockSpec(arr.shape, lambda b, s, pt: (0,) * arr.ndim)
    in_specs = ([page_spec(j) for j in range(pps)] * 2
                + [per_b((nrow, KV_W)), per_b((KV_W, n_cmp)), per_b((n_cmp, KV_W)),
                   per_b((PAGE, KV_W)), per_b((PAGE, KV_W)), per_b((KV_W, WIN)), per_b((KV_W, WIN)),
                   per_b((PAGE, KV_W)), per_b((PAGE, KV_W)), per_b((nrow, 3)),
                   const(bias_c),
                   pl.BlockSpec((1, nrow, pps * PAGE), lambda b, s, pt: (s, 0, 0)),
                   const(bias_n), const(bias_w), const(c2s), const(hsum_t), const(hexp), const(expand)])
    grid_spec = pltpu.PrefetchScalarGridSpec(
        num_scalar_prefetch=1,
        grid=(bsz, n_pages // pps),
        in_specs=in_specs,
        out_specs=pl.BlockSpec((1, nrow, DH), lambda b, s, pt: (b, 0, 0)),
        scratch_shapes=[pltpu.VMEM((n_sel_pad, LANES), F32),
                        pltpu.VMEM((nrow, 1), F32), pltpu.VMEM((nrow, 1), F32),
                        pltpu.VMEM((nrow, KV_W), F32), pltpu.VMEM((nrow, KV_W), F32),
                        pltpu.VMEM((nrow, KV_W), F32)],
    )
    return pl.pallas_call(
        functools.partial(_nsa_sample_kernel, pps=pps, n_cmp=n_cmp, n_sel=n_sel, n_sel_pad=n_sel_pad,
                          past_len=past_len, t_new=t_new),
        out_shape=jax.ShapeDtypeStruct((bsz, nrow, DH), F32),
        grid_spec=grid_spec,
        compiler_params=_cparams(("parallel", "arbitrary")),
        name="nsa_sample",
    )(page_table, *([k_pool] * pps), *([v_pool] * pps), q_bd, kct, vc, k_new, v_new, kw_cache, vw_cache, kw_new,
      vw_new, gates, bias_c, bias_s, bias_n, bias_w, c2s, hsum_t, hexp, expand)


def _post_kernel(x_ref, og_ref, on_ref, sma_ref, smb_ref, gate1_ref, sc2_ref, sh2_ref, gate2_ref, n2_ref, fg_ref,
                 wo_ref, wu_ref, wd_ref, y_ref):
    merged = sma_ref[...] * og_ref[...] + smb_ref[...] * on_ref[...]
    x1 = x_ref[...] + gate1_ref[0] * _dot(merged.astype(BF16), wo_ref[...])
    h2 = _rms(x1, n2_ref[...]) * (1.0 + sc2_ref[0]) + sh2_ref[0]
    up = jnp.maximum(_dot(h2.astype(BF16), wu_ref[...]), 0.0)
    x2 = x1 + gate2_ref[0] * _dot((up * up).astype(BF16), wd_ref[...])
    y_ref[...] = _rms(x2, fg_ref[...])


def _post(x, o_gla, o_nsa, sma, smb, gate1, scale2, shift2, gate2, n2, fg, wo, wu, wd, *, tm):
    m = x.shape[0]
    nb, r, _ = gate1.shape
    rows_per_mod = m // nb
    mod_spec = pl.BlockSpec((1, r, D_MODEL), lambda i: (i * tm // rows_per_mod, 0, 0))
    row = pl.BlockSpec((tm, D_MODEL), lambda i: (i, 0))
    return pl.pallas_call(
        _post_kernel,
        out_shape=jax.ShapeDtypeStruct((m, D_MODEL), F32),
        grid=(m // tm,),
        in_specs=[row] * 5 + [mod_spec] * 4 + [_const_spec((1, D_MODEL))] * 2
        + [_const_spec(wo.shape), _const_spec(wu.shape), _const_spec(wd.shape)],
        out_specs=row,
        compiler_params=_cparams(("parallel",)),
        name="post",
    )(x, o_gla, o_nsa, sma, smb, gate1, scale2, shift2, gate2, n2, fg, wo, wu, wd)


def _block_diag_w1(w1):
    w = w1.reshape(CMP_RATIO, CMP_STRIDE, DH, CMP_HID)
    eye = jnp.eye(NSA_G, dtype=w1.dtype)
    bd = jnp.einsum('redh,pg->epdgrh', w, eye)
    return bd.reshape(CMP_STRIDE * NSA_G * DH, NSA_G * CMP_RATIO * CMP_HID).astype(BF16)


def _tap_w1(w1):
    w = w1.reshape(CMP_RATIO, CMP_STRIDE // CMP_TAPS, CMP_TAPS, DH, CMP_HID)
    return jnp.transpose(w, (1, 2, 3, 0, 4)).reshape(CMP_STRIDE // CMP_TAPS, CMP_TAPS * DH,
                                                     CMP_RATIO * CMP_HID).astype(BF16)


def _cmp_to_sel_np(n_cmp, n_sel):
    cs = np.arange(n_cmp)[:, None] * CMP_STRIDE
    ss = np.arange(n_sel)[None, :] * SEL
    ov = np.minimum(cs + CMP_BLOCK, ss + SEL) - np.maximum(cs, ss)
    return np.clip(ov, 0, None).astype(np.float32) / CMP_BLOCK


def _bias_at(rel_bias, dist):
    onehot = (_bucket_np(dist)[..., None] == np.arange(REL_BUCKETS)).astype(np.float32)
    return jnp.moveaxis(jnp.dot(jnp.asarray(onehot), rel_bias, precision=HIGHEST), -1, 0)


def _bias_by_dist(rel_bias, n):
    return _bias_at(rel_bias, np.arange(n))


def _toeplitz(v, n):
    lead = v.shape[:-1]
    a = jnp.broadcast_to(v[..., None, :], lead + (n, 2 * n)).reshape(lead + (2 * n * n,))
    a = a[..., :n * (2 * n - 1)].reshape(lead + (n, 2 * n - 1))
    return a[..., n - 1:]


def _mod_split(mod, rows):
    parts = jnp.split(mod, 6, axis=-1)
    return [p.reshape(-1, rows, D_MODEL) for p in parts]


def kernel(x_prompt, x_sample, c_prompt, c_sample, state_gla, cache_k_cmp, cache_v_cmp, cache_k_slc, cache_v_slc, cache_k_win, cache_v_win, page_table, ada_w, ada_b, norm1_g, norm2_g, w_in, gla_a2, gla_a_b, gla_norm_g, cmp_pe_k, cmp_w1_k, cmp_w2_k, cmp_pe_v, cmp_w1_v, cmp_w2_v, w_o, w_up, w_down, rel_bias, final_g):
    depth = ada_w.shape[0]
    assert depth == 1, "single-layer trunk"
    bp, tp, _ = x_prompt.shape
    bs, ts, _ = x_sample.shape
    n_pages = page_table.shape[1]
    past_len = n_pages * PAGE
    mp, ms = bp * tp, bs * ts
    tm = 256
    tm_s = min(tm, ms)

    w = w_in[0]
    w_perm = jnp.concatenate([w[:, :3072], w[:, 3088:5648], w[:, 5696:7744], w[:, 5648:5696], w[:, 3072:3088],
                              jnp.zeros((D_MODEL, LANES - 64), w.dtype)], axis=1).astype(BF16)
    a2p = jnp.zeros((LANES, QK_W), F32).at[Z_GA:Z_GA + GLA_RANK].set(gla_a2[0]).astype(BF16)
    ab = gla_a_b[0].reshape(1, QK_W)
    g1 = norm1_g[0].reshape(1, D_MODEL)
    n2 = norm2_g[0].reshape(1, D_MODEL)
    fg = final_g.reshape(1, D_MODEL)
    gn = gla_norm_g[0].reshape(1, GLA_DV)
    wo, wu, wd = w_o[0].astype(BF16), w_up[0].astype(BF16), w_down[0].astype(BF16)
    cmp_k = (_block_diag_w1(cmp_w1_k[0]), jnp.broadcast_to(cmp_pe_k[0].reshape(1, -1), (8, CMP_BLOCK * DH)),
             cmp_w1_k[0].reshape(CMP_BLOCK * DH, CMP_HID), cmp_w2_k[0])
    cmp_v = (_block_diag_w1(cmp_w1_v[0]), jnp.broadcast_to(cmp_pe_v[0].reshape(1, -1), (8, CMP_BLOCK * DH)),
             cmp_w1_v[0].reshape(CMP_BLOCK * DH, CMP_HID), cmp_w2_v[0])

    mod = _ada(jnp.concatenate([c_prompt, c_sample], axis=0), ada_w[0], ada_b[0])
    sh1_p, sc1_p, gt1_p, sh2_p, sc2_p, gt2_p = _mod_split(mod[:bp], 1)
    sh1_s, sc1_s, gt1_s, sh2_s, sc2_s, gt2_s = _mod_split(jnp.repeat(mod[bp:], ts, axis=0), tm_s)

    xp = x_prompt.reshape(mp, D_MODEL)
    (gq, gk, gv, sgr, la, kc, vc, _, _, kw, vw, _, sma, smb, qt, ksh, vst, kwh, vwt, gtt,
     kc_t, vc_t, ks_t, vs_t) = _inproj(xp, sc1_p, sh1_p, g1, w_perm, a2p, ab, head_major=True, tm=tm, seq=tp)

    r3 = lambda a: a.reshape(bp, tp, a.shape[-1])
    o_gla_p, s_gla_p = _gla(r3(gq), r3(gk), r3(gv), r3(la), r3(sgr), gn, None, tb=256,
                            n_seq=2 if bp % 2 == 0 else 1)

    n_seg = tp // CMP_STRIDE
    assert n_seg == LANES, "prompt attention keeps one compressed block per lane"
    n_cmp_p = n_seg - CMP_RATIO + 1
    seg = lambda a: a.reshape(mp // CMP_STRIDE, CMP_STRIDE * KV_W)
    tm_seg = min(tm, mp // CMP_STRIDE)
    kcc = _cmpfin(_mm(seg(kc), cmp_k[0], tm=tm_seg).reshape(bp, n_seg, -1), None, *cmp_k[1:], layout='rows')
    vcc = _cmpfin(_mm(seg(vc), cmp_v[0], tm=tm_seg).reshape(bp, n_seg, -1), None, *cmp_v[1:3],
                  jnp.transpose(cmp_v[3]), layout='lanes')

    n_sel_p = tp // SEL
    f = _bias_by_dist(rel_bias, 2 * TQ)
    v_same = jnp.concatenate([jnp.repeat(f[:, :1], TQ - 1, axis=1), f[:, :TQ + 1]], axis=1)
    v_prev = jnp.concatenate([f[:, 1:], f[:, -1:]], axis=1)
    tiles = _toeplitz(jnp.stack([v_same, v_prev]), TQ).reshape(2, NSA_G, NSA_HG, TQ, TQ)
    tiles = jnp.transpose(tiles, (0, 1, 3, 2, 4)).reshape(2, NSA_G, TQ, NSA_HG * TQ)
    far_cols = jnp.broadcast_to(f[:, -1].reshape(NSA_G, NSA_HG, 1), (NSA_G, NSA_HG, TQ)).reshape(NSA_G, 1, -1)
    bias_t = (tiles - far_cols[None]) * LOG2E
    d0 = [-CMP_STRIDE * (r - CMP_WIN_PAD) - (CMP_BLOCK - 1) for r in range(CMP_WIN_ROWS)]
    lo_pad, hi_pad = max(0, -min(d0)), max(0, max(d0) + TQ - 2 * TQ)
    f_pad = jnp.concatenate([jnp.repeat(f[:, :1], lo_pad, axis=1), f, jnp.repeat(f[:, -1:], hi_pad, axis=1)], axis=1)
    bias_wc = jnp.stack([f_pad[:, d + lo_pad:d + lo_pad + TQ] for d in d0], axis=1)
    bias_wc = bias_wc.reshape(NSA_G, NSA_HG, CMP_WIN_ROWS, TQ)
    bias_wc = jnp.transpose(bias_wc, (0, 2, 1, 3)).reshape(NSA_G, CMP_WIN_ROWS, NSA_HG * TQ)
    bias_wc = (bias_wc - far_cols) * LOG2E
    c2s_t = np.zeros((n_sel_p, n_seg), np.float32)
    c2s_t[:, :n_cmp_p] = _cmp_to_sel_np(n_cmp_p, n_sel_p).T
    k4 = lambda a: a.reshape(NSA_G, bp, tp, DH)
    o_nsa_p = _nsa_prompt(qt, kcc, vcc, k4(ksh), vst, k4(kwh), vwt, gtt, bias_t, bias_wc,
                          jnp.asarray(c2s_t), bsz=bp, n_cmp=n_cmp_p)

    y_p = _post(xp, o_gla_p.reshape(mp, V_W), o_nsa_p.reshape(mp, D_MODEL), sma, smb, gt1_p, sc2_p, sh2_p, gt2_p,
                n2, fg, wo, wu, wd, tm=tm)

    kv5 = lambda a, b, t: a.reshape(1, b, t, NSA_G, DH)
    wp = min(WIN, tp)
    rows5 = lambda a: jnp.transpose(a, (0, 3, 1, 2))[None]
    p_out = (s_gla_p[None], rows5(kc_t), rows5(vc_t), rows5(ks_t), rows5(vs_t),
             kv5(kw, bp, tp)[:, :, tp - wp:], kv5(vw, bp, tp)[:, :, tp - wp:])

    xs = x_sample.reshape(ms, D_MODEL)
    (gq, gk, gv, sgr, la, kc, vc, ks, vs, kw, vw, gates, sma, smb, nq) = _inproj(
        xs, sc1_s, sh1_s, g1, w_perm, a2p, ab, head_major=False, tm=tm_s)

    t_pad = 8
    pad_t = lambda a: jnp.pad(a.reshape(bs, ts, a.shape[-1]), ((0, 0), (0, t_pad - ts), (0, 0)))
    o_gla_s, s_gla_s = _gla(pad_t(gq), pad_t(gk), pad_t(gv), pad_t(la), pad_t(sgr), gn, state_gla[0], tb=t_pad,
                            n_seq=2 if bs % 2 == 0 else 1)
    o_gla_s = o_gla_s[:, :ts].reshape(ms, V_W)

    pool_t = lambda a: jnp.transpose(a[0], (0, 2, 3, 1))
    new_seg = lambda a: jnp.pad(a.reshape(bs, ts, KV_W), ((0, 0), (0, CMP_STRIDE - ts), (0, 0))).reshape(
        bs, CMP_STRIDE * KV_W)
    n_cmp_s = past_len // CMP_STRIDE
    pps_cmp = min(64, n_pages)
    kcc = _cmpfin(_cmp_paged(pool_t(cache_k_cmp), page_table, _tap_w1(cmp_w1_k[0]), pages_per_step=pps_cmp),
                  _mm(new_seg(kc), cmp_k[0], tm=bs).reshape(bs, 1, -1), *cmp_k[1:3], jnp.transpose(cmp_k[3]),
                  layout='lanes').reshape(bs, KV_W, n_cmp_s)
    vcc = _cmpfin(_cmp_paged(pool_t(cache_v_cmp), page_table, _tap_w1(cmp_w1_v[0]), pages_per_step=pps_cmp),
                  _mm(new_seg(vc), cmp_v[0], tm=bs).reshape(bs, 1, -1), *cmp_v[1:], layout='flat')

    q5 = nq.reshape(bs, ts, NSA_G, NSA_HG, DH)
    q_bd = jnp.einsum('btghd,pg->bghtpd', q5, jnp.eye(NSA_G, dtype=F32)).reshape(bs, NSA_HEADS * ts, KV_W)
    q_bd = q_bd.astype(BF16)

    ncol = NSA_HEADS * ts
    col_tok = np.arange(ncol) % ts

    def bias_rows(key_pos):
        dist = past_len + col_tok[:, None] - key_pos[None, :]
        near = np.nonzero(dist.min(axis=0) < REL_MAX)[0]
        lo = int(near.min()) if near.size else len(key_pos)
        far_part = jnp.broadcast_to(jnp.repeat(f[:, -1], ts)[:, None], (ncol, lo))
        near = jnp.stack([_bias_at(rel_bias, dist[t, lo:]) for t in range(ts)], axis=1)
        return jnp.concatenate([far_part, near.reshape(ncol, -1)], axis=1)

    pps = min(32, n_pages)
    bias_c_s = bias_rows(np.arange(n_cmp_s) * CMP_STRIDE + CMP_BLOCK - 1)
    bias_s_s = jnp.transpose(bias_rows(np.arange(past_len)).reshape(ncol, n_pages // pps, pps * PAGE), (1, 0, 2))
    bias_n_s = bias_rows(past_len + np.arange(PAGE))
    bias_w_s = bias_rows(past_len - WIN + np.arange(WIN))
    n_sel_s = past_len // SEL + 1
    c2s_s = np.zeros((n_cmp_s, -(-n_sel_s // LANES) * LANES), np.float32)
    c2s_s[:, :n_sel_s] = _cmp_to_sel_np(n_cmp_s, n_sel_s)
    gt_col = np.arange(LANES) // (NSA_HG * ts) * ts + np.arange(LANES) % ts
    hsum = ((gt_col[:, None] == np.arange(LANES)[None, :]) & (np.arange(LANES)[:, None] < ncol)).astype(np.float32)
    expand_s = (np.arange(pps * PAGE)[None, :] // SEL == np.arange(pps * PAGE // SEL)[:, None]).astype(np.float32)
    gates_s = jnp.transpose(gates[:, :3 * NSA_HEADS].reshape(bs, ts, 3, NSA_HEADS), (0, 3, 1, 2)).reshape(bs, ncol, 3)
    new_rows = lambda a: jnp.pad(a.reshape(bs, ts, KV_W), ((0, 0), (0, PAGE - ts), (0, 0)))
    wb = cache_k_win.shape[2]
    win_t = lambda a: pool_t(a).reshape(bs, KV_W, wb)
    o_s = _nsa_sample(page_table, pool_t(cache_k_slc), pool_t(cache_v_slc), q_bd, kcc, vcc, new_rows(ks),
                      new_rows(vs), win_t(cache_k_win), win_t(cache_v_win), new_rows(kw),
                      new_rows(vw), gates_s, bias_c_s, bias_s_s, bias_n_s, bias_w_s, jnp.asarray(c2s_s),
                      jnp.asarray(hsum.T), jnp.asarray(hsum.T, dtype=BF16), jnp.asarray(expand_s, dtype=BF16),
                      pps=pps, t_new=ts)
    o_nsa_s = jnp.transpose(o_s.reshape(bs, NSA_HEADS, ts, DH), (0, 2, 1, 3)).reshape(ms, D_MODEL)

    y_s = _post(xs, o_gla_s, o_nsa_s, sma, smb, gt1_s, sc2_s, sh2_s, gt2_s, n2, fg, wo, wu, wd, tm=tm_s)

    kw_new = jnp.concatenate([cache_k_win[0], kv5(kw, bs, ts)[0]], axis=1)[:, ts:][None]
    vw_new = jnp.concatenate([cache_v_win[0], kv5(vw, bs, ts)[0]], axis=1)[:, ts:][None]
    s_out = (s_gla_s[None], kv5(kc, bs, ts), kv5(vc, bs, ts), kv5(ks, bs, ts), kv5(vs, bs, ts), kw_new, vw_new)

    return (y_p.reshape(bp, tp, D_MODEL), y_s.reshape(bs, ts, D_MODEL)) + p_out + s_out
```
